```python
import math
import jax, jax.numpy as jnp
from jax import lax
import numpy as np

D_MODEL = 1024
BATCH = 4
SEQ = 4096
DEPTH = 1

CHUNK = 64
Q_BLOCK = 128
D_MIX = D_MODEL
RW_WIDTH = D_MIX // 2
RW_HEAD = 64
RW_HEADS = RW_WIDTH // RW_HEAD
RW_DECAY_RANK = 64
RW_A_RANK = 64
DA_WIDTH = D_MIX - RW_WIDTH
DA_HEADS = 4
DA_VDIM = DA_WIDTH // DA_HEADS
DA_QKDIM = DA_VDIM // 2
RW_SHIFTED = 3 * RW_WIDTH + RW_DECAY_RANK + RW_A_RANK
RW_COLS = RW_SHIFTED + RW_WIDTH
DA_COLS = 4 * DA_WIDTH
D_IN = RW_COLS + DA_COLS
RMS_EPS = 1e-6
GN_EPS = 64e-5
SUBLN_EPS = 1e-5

kernel_name = "hymba_rwkv7_diffattn_block"


def rms_norm(x, g, eps=RMS_EPS):
    xf = x.astype(jnp.float32)
    y = xf * lax.rsqrt(jnp.mean(xf * xf, axis=-1, keepdims=True) + eps)
    return (y * g.astype(jnp.float32)).astype(x.dtype)


def token_shift(u, mu):
    prev = jnp.pad(u, ((0, 0), (1, 0), (0, 0)))[:, :-1]
    return u + (prev - u) * mu


def rwkv7_branch(feat, gate, mu, w0, w_w2, a0, w_a2, k_k, k_a, r_k, lnx_w, lnx_b):
    B, S, _ = feat.shape
    H, N = RW_HEADS, RW_HEAD
    feat = token_shift(feat, mu).astype(jnp.float32)
    o1, o2, o3, o4 = RW_WIDTH, 2 * RW_WIDTH, 3 * RW_WIDTH, 3 * RW_WIDTH + RW_DECAY_RANK
    r, k, v = feat[..., :o1], feat[..., o1:o2], feat[..., o2:o3]
    w_lo, a_lo = feat[..., o3:o4], feat[..., o4:]
    w_log = -jax.nn.softplus(-(w0 + jnp.tanh(w_lo) @ w_w2)) - 0.5
    decay = jnp.exp(-jnp.exp(w_log))
    a = jax.nn.sigmoid(a0 + a_lo @ w_a2)
    kk = (k * k_k).reshape(B, S, H, N)
    kk = kk / jnp.maximum(jnp.linalg.norm(kk, axis=-1, keepdims=True), 1e-12)
    k = k * (1.0 + (a - 1.0) * k_a)
    hd = lambda t: t.reshape(B, S, H, N)
    r, k, v, decay, a = hd(r), hd(k), hd(v), hd(decay), hd(a)

    def step(state, inp):
        r_t, w_t, k_t, v_t, kk_t, a_t = inp
        s_kk = jnp.einsum('bhvk,bhk->bhv', state, kk_t)
        state = (state * w_t[:, :, None, :]
                 - s_kk[..., None] * (a_t * kk_t)[:, :, None, :]
                 + v_t[..., None] * k_t[:, :, None, :])
        return state, jnp.einsum('bhvk,bhk->bhv', state, r_t)

    tm = lambda t: jnp.swapaxes(t, 0, 1)
    s0 = jnp.zeros((B, H, N, N), jnp.float32)
    _, o = lax.scan(step, s0, (tm(r), tm(decay), tm(k), tm(v), tm(kk), tm(a)))
    o = tm(o)
    mean = jnp.mean(o, axis=-1, keepdims=True)
    var = jnp.mean(jnp.square(o - mean), axis=-1, keepdims=True)
    o = ((o - mean) * lax.rsqrt(var + GN_EPS)).reshape(B, S, RW_WIDTH) * lnx_w + lnx_b
    bonus = jnp.sum(r * k * r_k, axis=-1, keepdims=True) * v
    out = (o + bonus.reshape(B, S, RW_WIDTH)).astype(gate.dtype)
    return out * jax.nn.silu(gate)


def diff_attention_branch(q, k, v, gate, lam_q1, lam_k1, lam_q2, lam_k2, subln_w, lambda_init):
    B, S, _ = q.shape
    H, dqk, dv = DA_HEADS, DA_QKDIM, DA_VDIM
    q = q.reshape(B, S, H, 2, dqk)
    k = k.reshape(B, S, H, 2, dqk)
    v = v.reshape(B, S, H, dv)
    f32 = jnp.float32
    lam = (jnp.exp(jnp.sum(lam_q1.astype(f32) * lam_k1.astype(f32)))
           - jnp.exp(jnp.sum(lam_q2.astype(f32) * lam_k2.astype(f32))) + lambda_init)
    scale = 1.0 / math.sqrt(dqk)
    nb = S // Q_BLOCK
    qb = jnp.moveaxis(q.reshape(B, nb, Q_BLOCK, H, 2, dqk), 1, 0)
    k_chunk = jnp.arange(S) // CHUNK

    def block(args):
        q_blk, bi = args
        q_chunk = (bi * Q_BLOCK + jnp.arange(Q_BLOCK)) // CHUNK
        s = jnp.einsum('bqhcd,bkhcd->bhcqk', q_blk, k).astype(f32) * scale
        mask = q_chunk[:, None] >= k_chunk[None, :]
        p = jax.nn.softmax(jnp.where(mask, s, -jnp.inf), axis=-1)
        attn = p[:, :, 0] - lam * p[:, :, 1]
        return jnp.einsum('bhqk,bkhe->bqhe', attn.astype(v.dtype), v)

    o = lax.map(block, (qb, jnp.arange(nb)))
    o = jnp.moveaxis(o, 0, 1).reshape(B, S, H, dv)
    o = rms_norm(o, subln_w, SUBLN_EPS) * (1.0 - lambda_init)
    return o.reshape(B, S, DA_WIDTH) * jax.nn.silu(gate)


def setup_inputs(seed: int = 0) -> dict:
    key = jax.random.key(seed)
    ks = jax.random.split(key, 24)
    f32 = jnp.float32
    nrm = lambda k, shape, s: (jax.random.normal(k, shape, f32) * s)
    L = DEPTH
    return {
        "x": nrm(ks[0], (BATCH, SEQ, D_MODEL), 1.0),
        "c": nrm(ks[1], (BATCH, D_MODEL), 1.0),
        "w_ada": nrm(ks[2], (L, D_MODEL, 3 * D_MODEL), D_MODEL ** -0.5),
        "b_ada": nrm(ks[3], (L, 3 * D_MODEL), 0.02),
        "g_pre": 1.0 + nrm(ks[4], (L, D_MODEL), 0.02),
        "g_post": 1.0 + nrm(ks[5], (L, D_MODEL), 0.02),
        "w_in": nrm(ks[6], (L, D_MODEL, D_IN), D_MODEL ** -0.5),
        "w_out": nrm(ks[7], (L, D_MIX, D_MODEL), D_MIX ** -0.5),
        "rw_mu": jax.random.uniform(ks[8], (L, RW_SHIFTED), f32),
        "rw_w0": jax.random.uniform(ks[9], (L, RW_WIDTH), f32, -6.0, -1.0),
        "rw_w_w2": nrm(ks[10], (L, RW_DECAY_RANK, RW_WIDTH), 0.1 * RW_DECAY_RANK ** -0.5),
        "rw_a0": nrm(ks[11], (L, RW_WIDTH), 0.1),
        "rw_w_a2": nrm(ks[12], (L, RW_A_RANK, RW_WIDTH), 0.5 * RW_A_RANK ** -0.5),
        "rw_k_k": 0.85 + nrm(ks[13], (L, RW_WIDTH), 0.02),
        "rw_k_a": 1.0 + nrm(ks[14], (L, RW_WIDTH), 0.02),
        "rw_r_k": nrm(ks[15], (L, RW_HEADS, RW_HEAD), 0.1),
        "rw_lnx_w": 1.0 + nrm(ks[16], (L, RW_WIDTH), 0.02),
        "rw_lnx_b": nrm(ks[17], (L, RW_WIDTH), 0.02),
        "da_lam_q1": nrm(ks[18], (L, DA_QKDIM), 0.1),
        "da_lam_k1": nrm(ks[19], (L, DA_QKDIM), 0.1),
        "da_lam_q2": nrm(ks[20], (L, DA_QKDIM), 0.1),
        "da_lam_k2": nrm(ks[21], (L, DA_QKDIM), 0.1),
        "da_subln_w": 1.0 + nrm(ks[22], (L, DA_VDIM), 0.02),
    }


def reference(x, c, w_ada, b_ada, g_pre, g_post, w_in, w_out, rw_mu, rw_w0, rw_w_w2,
              rw_a0, rw_w_a2, rw_k_k, rw_k_a, rw_r_k, rw_lnx_w, rw_lnx_b,
              da_lam_q1, da_lam_k1, da_lam_q2, da_lam_k2, da_subln_w):
    for l in range(DEPTH):
        lambda_init = 0.8 - 0.6 * math.exp(-0.3 * l)
        mod = jax.nn.silu(c) @ w_ada[l] + b_ada[l]
        shift, scale, gate = jnp.split(mod, 3, axis=-1)
        h = rms_norm(x, g_pre[l]) * (1.0 + scale[:, None, :]) + shift[:, None, :]
        u = h @ w_in[l]
        rw_feat, rw_gate = u[..., :RW_SHIFTED], u[..., RW_SHIFTED:RW_COLS]
        d0 = RW_COLS
        dq = u[..., d0:d0 + DA_WIDTH]
        dk = u[..., d0 + DA_WIDTH:d0 + 2 * DA_WIDTH]
        dv = u[..., d0 + 2 * DA_WIDTH:d0 + 3 * DA_WIDTH]
        dg = u[..., d0 + 3 * DA_WIDTH:]
        y_rw = rwkv7_branch(rw_feat, rw_gate, rw_mu[l], rw_w0[l], rw_w_w2[l], rw_a0[l],
                            rw_w_a2[l], rw_k_k[l], rw_k_a[l], rw_r_k[l], rw_lnx_w[l], rw_lnx_b[l])
        y_da = diff_attention_branch(dq, dk, dv, dg, da_lam_q1[l], da_lam_k1[l], da_lam_q2[l],
                                     da_lam_k2[l], da_subln_w[l], lambda_init)
        y = jnp.concatenate([y_rw.astype(x.dtype), y_da.astype(x.dtype)], axis=-1) @ w_out[l]
        y = rms_norm(y, g_post[l])
        x = x + gate[:, None, :] * y
    return x
```

```python
import functools
import math

import jax
import jax.numpy as jnp
from jax import lax
from jax.experimental import pallas as pl
from jax.experimental.pallas import tpu as pltpu

D_MODEL = 1024
SEQ = 4096
CHUNK = 64
RW_WIDTH = 512
RW_HEAD = 64
RW_HEADS = RW_WIDTH // RW_HEAD
RW_RANK = 64
RW_SHIFTED = 3 * RW_WIDTH + 2 * RW_RANK
RW_COLS = RW_SHIFTED + RW_WIDTH
DA_WIDTH = 512
DA_HEADS = 4
DA_VDIM = DA_WIDTH // DA_HEADS
DA_QKDIM = DA_VDIM // 2
D_IN = RW_COLS + 4 * DA_WIDTH
RMS_EPS = 1e-6
GN_EPS = 64e-5
SUBLN_EPS = 1e-5

PROJ_ROWS = 256
RW_CHUNK = 64
ATT_TILE = 256
VMEM_LIMIT = 48 * 1024 * 1024

F32 = jnp.float32
BF16 = jnp.bfloat16
HIGHEST = lax.Precision.HIGHEST


def _silu(t):
    return t * jax.nn.sigmoid(t)


def _dot(a, b, precision=None):
    return jnp.dot(a, b, preferred_element_type=F32, precision=precision)


def _dot_nt(a, b, precision=None):
    return lax.dot_general(a, b, (((1,), (1,)), ((), ())),
                           preferred_element_type=F32, precision=precision)


def _dot_tn(a, b, precision=None):
    return lax.dot_general(a, b, (((0,), (0,)), ((), ())),
                           preferred_element_type=F32, precision=precision)


def _mod_kernel(c_ref, w_ref, b_ref, o_ref):
    o_ref[...] = _dot(_silu(c_ref[...]), w_ref[...], HIGHEST) + b_ref[...]


def _modulation(c, w_ada, b_ada):
    batch = c.shape[0]
    return pl.pallas_call(
        _mod_kernel,
        grid=(3,),
        in_specs=[
            pl.BlockSpec((batch, D_MODEL), lambda j: (0, 0)),
            pl.BlockSpec((D_MODEL, D_MODEL), lambda j: (0, j)),
            pl.BlockSpec((1, D_MODEL), lambda j: (0, j)),
        ],
        out_specs=pl.BlockSpec((batch, D_MODEL), lambda j: (0, j)),
        out_shape=jax.ShapeDtypeStruct((batch, 3 * D_MODEL), F32),
        compiler_params=pltpu.CompilerParams(vmem_limit_bytes=VMEM_LIMIT),
    )(c, w_ada, b_ada.reshape(1, 3 * D_MODEL))


_IN_SEGMENTS = (
    (0, RW_SHIFTED),
    (RW_SHIFTED, RW_COLS),
    (RW_COLS, RW_COLS + DA_WIDTH),
    (RW_COLS + DA_WIDTH, RW_COLS + 2 * DA_WIDTH),
    (RW_COLS + 2 * DA_WIDTH, RW_COLS + 3 * DA_WIDTH),
    (RW_COLS + 3 * DA_WIDTH, D_IN),
)


def _in_proj_kernel(x_ref, shift_ref, scale_ref, g_ref, w_ref, *out_refs):
    x = x_ref[0]
    y = x * lax.rsqrt(jnp.mean(x * x, axis=-1, keepdims=True) + RMS_EPS) * g_ref[...]
    h = (y * (1.0 + scale_ref[0]) + shift_ref[0]).astype(BF16)
    for (lo, hi), o_ref in zip(_IN_SEGMENTS, out_refs):
        o_ref[0] = _dot(h, w_ref[:, lo:hi]).astype(o_ref.dtype)


def _in_projection(x, shift, scale, g_pre, w_in_bf16):
    batch, seq, _ = x.shape
    tm = PROJ_ROWS
    row_spec = lambda width: pl.BlockSpec((1, tm, width), lambda b, i: (b, i, 0))
    vec_spec = pl.BlockSpec((1, 1, D_MODEL), lambda b, i: (b, 0, 0))
    out_dtypes = (F32, BF16, BF16, BF16, BF16, BF16)
    return pl.pallas_call(
        _in_proj_kernel,
        grid=(batch, seq // tm),
        in_specs=[
            row_spec(D_MODEL), vec_spec, vec_spec,
            pl.BlockSpec((1, D_MODEL), lambda b, i: (0, 0)),
            pl.BlockSpec((D_MODEL, D_IN), lambda b, i: (0, 0)),
        ],
        out_specs=[row_spec(hi - lo) for lo, hi in _IN_SEGMENTS],
        out_shape=[jax.ShapeDtypeStruct((batch, seq, hi - lo), dt)
                   for (lo, hi), dt in zip(_IN_SEGMENTS, out_dtypes)],
        compiler_params=pltpu.CompilerParams(
            dimension_semantics=("parallel", "parallel"),
            vmem_limit_bytes=VMEM_LIMIT),
    )(x, shift, scale, g_pre, w_in_bf16)


def _unit_lower_inverse(a_strict, row, col, precision):
    n = a_strict.shape[0]
    eye = (row == col).astype(F32)
    t = eye
    s = 1
    while s < n:
        pair = ((row // (2 * s)) == (col // (2 * s))) \
            & (((row // s) % 2) == 1) & (((col // s) % 2) == 0)
        off = jnp.where(pair, a_strict, 0.0)
        if s == 1:
            t = eye - off
        else:
            t = t - _dot(_dot(t, off, precision), t, precision)
        s *= 2
    return t


def _rwkv_kernel(feat_ref, gate_ref, mu_ref, w0_ref, ww2_ref, a0_ref, wa2_ref,
                 kk_ref, ka_ref, rk_ref, lnw_ref, lnb_ref, y_ref, ubuf, state,
                 *, precision):
    C, N = RW_CHUNK, RW_HEAD
    ci = pl.program_id(1)

    @pl.when(ci == 0)
    def _():
        ubuf[0:8, :] = jnp.zeros((8, RW_SHIFTED), F32)
        state[...] = jnp.zeros(state.shape, F32)

    u = feat_ref[0]
    ubuf[8:8 + C, :] = u
    prev = ubuf[7:7 + C, :]
    ubuf[0:8, :] = u[C - 8:C, :]
    feat = u + (prev - u) * mu_ref[...]

    W = RW_WIDTH
    r, k, v = feat[:, 0:W], feat[:, W:2 * W], feat[:, 2 * W:3 * W]
    w_lo = feat[:, 3 * W:3 * W + RW_RANK]
    a_lo = feat[:, 3 * W + RW_RANK:]

    z = w0_ref[...] + _dot(jnp.tanh(w_lo), ww2_ref[...])
    logw = -math.exp(-0.5) * jax.nn.sigmoid(z)
    a = jax.nn.sigmoid(a0_ref[...] + _dot(a_lo, wa2_ref[...]))
    kk_raw = k * kk_ref[...]
    k2 = k * (1.0 + (a - 1.0) * ka_ref[...])

    row = lax.broadcasted_iota(jnp.int32, (C, C), 0)
    col = lax.broadcasted_iota(jnp.int32, (C, C), 1)
    strict = row > col
    incl = row >= col
    cs = _dot(incl.astype(F32), logw, HIGHEST)
    cs_last = cs[C - 1:C, :]
    g_incl = jnp.exp(cs)
    g_excl = jnp.exp(cs - logw)
    g_inv = jnp.exp(-cs)
    g_end = jnp.exp(cs_last - cs)
    g_all = jnp.exp(cs_last)

    outs = []
    for h in range(RW_HEADS):
        sl = slice(h * N, (h + 1) * N)
        kkr = kk_raw[:, sl]
        nrm = jnp.sqrt(jnp.sum(kkr * kkr, axis=-1, keepdims=True))
        kk = kkr / jnp.maximum(nrm, 1e-12)
        bb = a[:, sl] * kk
        r_h, k_h, v_h = r[:, sl], k2[:, sl], v[:, sl]
        kk_t = kk * g_excl[:, sl]
        r_t = r_h * g_incl[:, sl]
        k_inv = k_h * g_inv[:, sl]
        b_inv = bb * g_inv[:, sl]
        k_end = k_h * g_end[:, sl]
        b_end = bb * g_end[:, sl]
        st = state[h]

        a_kv = jnp.where(strict, _dot_nt(kk_t, k_inv, precision), 0.0)
        a_kb = jnp.where(strict, _dot_nt(kk_t, b_inv, precision), 0.0)
        a_rk = jnp.where(incl, _dot_nt(r_t, k_inv, precision), 0.0)
        a_rb = jnp.where(incl, _dot_nt(r_t, b_inv, precision), 0.0)
        t_inv = _unit_lower_inverse(a_kb, row, col, precision)

        rhs = _dot_nt(kk_t, st, precision) + _dot(a_kv, v_h, precision)
        uu = _dot(t_inv, rhs, precision)
        o = (_dot_nt(r_t, st, precision) + _dot(a_rk, v_h, precision)
             - _dot(a_rb, uu, precision))
        state[h] = (st * g_all[:, sl] + _dot_tn(v_h, k_end, precision)
                    - _dot_tn(uu, b_end, precision))

        mean = jnp.mean(o, axis=-1, keepdims=True)
        var = jnp.mean(jnp.square(o - mean), axis=-1, keepdims=True)
        o = (o - mean) * lax.rsqrt(var + GN_EPS)
        bonus = jnp.sum(r_h * k_h * rk_ref[:, sl], axis=-1, keepdims=True) * v_h
        outs.append(o * lnw_ref[:, sl] + lnb_ref[:, sl] + bonus)

    out = jnp.concatenate(outs, axis=-1)
    y_ref[0] = (out * _silu(gate_ref[0].astype(F32))).astype(y_ref.dtype)


def _rwkv_branch(feat, gate, mu, w0, w_w2, a0, w_a2, k_k, k_a, r_k, lnx_w, lnx_b,
                 precision=HIGHEST):
    batch, seq, _ = feat.shape
    C = RW_CHUNK
    full = lambda shape: pl.BlockSpec(shape, lambda b, i: (0,) * len(shape))
    vec = lambda t: t.reshape(1, -1)
    return pl.pallas_call(
        functools.partial(_rwkv_kernel, precision=precision),
        grid=(batch, seq // C),
        in_specs=[
            pl.BlockSpec((1, C, RW_SHIFTED), lambda b, i: (b, i, 0)),
            pl.BlockSpec((1, C, RW_WIDTH), lambda b, i: (b, i, 0)),
            full((1, RW_SHIFTED)), full((1, RW_WIDTH)), full((RW_RANK, RW_WIDTH)),
            full((1, RW_WIDTH)), full((RW_RANK, RW_WIDTH)),
            full((1, RW_WIDTH)), full((1, RW_WIDTH)), full((1, RW_WIDTH)),
            full((1, RW_WIDTH)), full((1, RW_WIDTH)),
        ],
        out_specs=pl.BlockSpec((1, C, RW_WIDTH), lambda b, i: (b, i, 0)),
        out_shape=jax.ShapeDtypeStruct((batch, seq, RW_WIDTH), BF16),
        scratch_shapes=[
            pltpu.VMEM((C + 8, RW_SHIFTED), F32),
            pltpu.VMEM((RW_HEADS, RW_HEAD, RW_HEAD), F32),
        ],
        compiler_params=pltpu.CompilerParams(
            dimension_semantics=("parallel", "arbitrary"),
            vmem_limit_bytes=VMEM_LIMIT),
    )(feat, gate, vec(mu), vec(w0), w_w2, vec(a0), w_a2, vec(k_k), vec(k_a),
      vec(r_k), vec(lnx_w), vec(lnx_b))


def _attn_kernel(lq1_ref, lk1_ref, lq2_ref, lk2_ref, sw_ref, q_ref, k_ref, v_ref,
                 g_ref, o_ref, *, lambda_init):
    T = ATT_TILE
    qi = pl.program_id(2)
    lane = lax.broadcasted_iota(jnp.int32, (T, DA_VDIM), 1)
    q = q_ref[0] * (1.0 / math.sqrt(DA_QKDIM))
    zero = jnp.zeros_like(q)
    q_c = (jnp.where(lane < DA_QKDIM, q, zero), jnp.where(lane >= DA_QKDIM, q, zero))

    def update(carry, j, mask):
        start = pl.multiple_of(j * T, T)
        kb = k_ref[0, pl.ds(start, T), :]
        vb = v_ref[0, pl.ds(start, T), :]
        new = []
        for c in range(2):
            m, l, acc = carry[c]
            s = _dot_nt(q_c[c], kb)
            if mask is not None:
                s = jnp.where(mask, s, -jnp.inf)
            m_new = jnp.maximum(m, jnp.max(s, axis=-1, keepdims=True))
            alpha = jnp.exp(m - m_new)
            p = jnp.exp(s - m_new)
            l = alpha * l + jnp.sum(p, axis=-1, keepdims=True)
            acc = alpha * acc + _dot(p.astype(BF16), vb)
            new.append((m_new, l, acc))
        return tuple(new)

    init = tuple((jnp.full((T, 1), -jnp.inf, F32), jnp.zeros((T, 1), F32),
                  jnp.zeros((T, DA_VDIM), F32)) for _ in range(2))
    carry = lax.fori_loop(0, qi, lambda j, cr: update(cr, j, None), init)
    row = lax.broadcasted_iota(jnp.int32, (T, T), 0)
    col = lax.broadcasted_iota(jnp.int32, (T, T), 1)
    carry = update(carry, qi, (row // CHUNK) >= (col // CHUNK))

    lam = (jnp.exp(jnp.sum(lq1_ref[...] * lk1_ref[...], axis=-1, keepdims=True))
           - jnp.exp(jnp.sum(lq2_ref[...] * lk2_ref[...], axis=-1, keepdims=True))
           + lambda_init)
    (_, l0, acc0), (_, l1, acc1) = carry
    o = acc0 / l0 - lam * (acc1 / l1)
    o = o * lax.rsqrt(jnp.mean(o * o, axis=-1, keepdims=True) + SUBLN_EPS)
    o = o * sw_ref[...] * (1.0 - lambda_init)
    o_ref[0] = (o * _silu(g_ref[0].astype(F32))).astype(o_ref.dtype)


def _diff_attention(q, k, v, gate, lam_q1, lam_k1, lam_q2, lam_k2, subln_w, lambda_init):
    batch, seq, _ = q.shape
    T = ATT_TILE
    small = lambda n: pl.BlockSpec((1, n), lambda b, h, i: (0, 0))
    tile = pl.BlockSpec((1, T, DA_VDIM), lambda b, h, i: (b, i, h))
    whole = pl.BlockSpec((1, seq, DA_VDIM), lambda b, h, i: (b, 0, h))
    vec = lambda t: t.reshape(1, -1)
    return pl.pallas_call(
        functools.partial(_attn_kernel, lambda_init=lambda_init),
        grid=(batch, DA_HEADS, seq // T),
        in_specs=[small(DA_QKDIM)] * 4 + [small(DA_VDIM), tile, whole, whole, tile],
        out_specs=tile,
        out_shape=jax.ShapeDtypeStruct((batch, seq, DA_WIDTH), BF16),
        compiler_params=pltpu.CompilerParams(
            dimension_semantics=("parallel", "parallel", "arbitrary"),
            vmem_limit_bytes=VMEM_LIMIT),
    )(vec(lam_q1), vec(lam_k1), vec(lam_q2), vec(lam_k2), vec(subln_w), q, k, v, gate)


def _out_proj_kernel(yr_ref, yd_ref, wr_ref, wd_ref, g_ref, gate_ref, x_ref, o_ref):
    y = _dot(yr_ref[0], wr_ref[...]) + _dot(yd_ref[0], wd_ref[...])
    y = y * lax.rsqrt(jnp.mean(y * y, axis=-1, keepdims=True) + RMS_EPS) * g_ref[...]
    o_ref[0] = x_ref[0] + gate_ref[0] * y


def _out_projection(y_rw, y_da, w_out_bf16, g_post, gate, x):
    batch, seq, _ = x.shape
    tm = PROJ_ROWS
    half = lambda j: pl.BlockSpec((RW_WIDTH, D_MODEL), lambda b, i: (j, 0))
    row_spec = lambda width: pl.BlockSpec((1, tm, width), lambda b, i: (b, i, 0))
    return pl.pallas_call(
        _out_proj_kernel,
        grid=(batch, seq // tm),
        in_specs=[
            row_spec(RW_WIDTH), row_spec(DA_WIDTH), half(0), half(1),
            pl.BlockSpec((1, D_MODEL), lambda b, i: (0, 0)),
            pl.BlockSpec((1, 1, D_MODEL), lambda b, i: (b, 0, 0)),
            row_spec(D_MODEL),
        ],
        out_specs=row_spec(D_MODEL),
        out_shape=jax.ShapeDtypeStruct(x.shape, x.dtype),
        compiler_params=pltpu.CompilerParams(
            dimension_semantics=("parallel", "parallel"),
            vmem_limit_bytes=VMEM_LIMIT),
    )(y_rw, y_da, w_out_bf16, w_out_bf16, g_post, gate, x)


def kernel(x, c, w_ada, b_ada, g_pre, g_post, w_in, w_out, rw_mu, rw_w0, rw_w_w2,
           rw_a0, rw_w_a2, rw_k_k, rw_k_a, rw_r_k, rw_lnx_w, rw_lnx_b,
           da_lam_q1, da_lam_k1, da_lam_q2, da_lam_k2, da_subln_w):
    batch = x.shape[0]
    depth = w_in.shape[0]
    for l in range(depth):
        lambda_init = 0.8 - 0.6 * math.exp(-0.3 * l)
        mod = _modulation(c, w_ada[l], b_ada[l]).reshape(batch, 3, 1, D_MODEL)
        shift, scale, gate = mod[:, 0], mod[:, 1], mod[:, 2]
        feat, rw_gate, dq, dk, dv, dg = _in_projection(
            x, shift, scale, g_pre[l].reshape(1, D_MODEL), w_in[l].astype(BF16))
        y_rw = _rwkv_branch(feat, rw_gate, rw_mu[l], rw_w0[l], rw_w_w2[l], rw_a0[l],
                            rw_w_a2[l], rw_k_k[l], rw_k_a[l], rw_r_k[l],
                            rw_lnx_w[l], rw_lnx_b[l])
        y_da = _diff_attention(dq, dk, dv, dg, da_lam_q1[l], da_lam_k1[l],
                               da_lam_q2[l], da_lam_k2[l], da_subln_w[l], lambda_init)
        x = _out_projection(y_rw, y_da, w_out[l].astype(BF16),
                            g_post[l].reshape(1, D_MODEL), gate, x)
    return x
```

```python
import functools
import math

import jax
import jax.numpy as jnp
from jax import lax
from jax.experimental import pallas as pl
from jax.experimental.pallas import tpu as pltpu

D_MODEL = 1024
SEQ = 4096
CHUNK = 64
RW_WIDTH = 512
RW_HEAD = 64
RW_HEADS = RW_WIDTH // RW_HEAD
RW_RANK = 64
RW_SHIFTED = 3 * RW_WIDTH + 2 * RW_RANK
RW_COLS = RW_SHIFTED + RW_WIDTH
DA_WIDTH = 512
DA_HEADS = 4
DA_VDIM = DA_WIDTH // DA_HEADS
DA_QKDIM = DA_VDIM // 2
D_IN = RW_COLS + 4 * DA_WIDTH
RMS_EPS = 1e-6
GN_EPS = 64e-5
SUBLN_EPS = 1e-5

PROJ_ROWS = 256
RW_CHUNK = 64
RW_PAIR = 2 * RW_HEAD
ATT_TILE = 256
VMEM_LIMIT = 48 * 1024 * 1024

F32 = jnp.float32
BF16 = jnp.bfloat16
HIGHEST = lax.Precision.HIGHEST


def _silu(t):
    return t * jax.nn.sigmoid(t)


def _dot(a, b, precision=None):
    return jnp.dot(a, b, preferred_element_type=F32, precision=precision)


def _dot_nt(a, b):
    return lax.dot_general(a, b, (((1,), (1,)), ((), ())), preferred_element_type=F32)


def _dot_tn(a, b):
    return lax.dot_general(a, b, (((0,), (0,)), ((), ())), preferred_element_type=F32)


def _split_dot(x, w_exact, terms, *, x_is_lhs):
    acc = None
    for _ in range(terms):
        piece = x.astype(BF16)
        part = _dot(piece, w_exact) if x_is_lhs else _dot(w_exact, piece)
        acc = part if acc is None else acc + part
        x = x - piece.astype(F32)
    return acc


def _mod_kernel(c_ref, w_ref, b_ref, o_ref):
    o_ref[...] = _dot(_silu(c_ref[...]), w_ref[...], HIGHEST) + b_ref[...]


def _modulation(c, w_ada, b_ada):
    batch = c.shape[0]
    return pl.pallas_call(
        _mod_kernel,
        grid=(3,),
        in_specs=[
            pl.BlockSpec((batch, D_MODEL), lambda j: (0, 0)),
            pl.BlockSpec((D_MODEL, D_MODEL), lambda j: (0, j)),
            pl.BlockSpec((1, D_MODEL), lambda j: (0, j)),
        ],
        out_specs=pl.BlockSpec((batch, D_MODEL), lambda j: (0, j)),
        out_shape=jax.ShapeDtypeStruct((batch, 3 * D_MODEL), F32),
        compiler_params=pltpu.CompilerParams(vmem_limit_bytes=VMEM_LIMIT),
        name="adaln_mod",
    )(c, w_ada, b_ada.reshape(1, 3 * D_MODEL))


_IN_SEGMENTS = (
    (0, RW_SHIFTED),
    (RW_SHIFTED, RW_COLS),
    (RW_COLS, RW_COLS + DA_WIDTH),
    (RW_COLS + DA_WIDTH, RW_COLS + 2 * DA_WIDTH),
    (RW_COLS + 2 * DA_WIDTH, RW_COLS + 3 * DA_WIDTH),
    (RW_COLS + 3 * DA_WIDTH, D_IN),
)


def _in_proj_kernel(x_ref, shift_ref, scale_ref, g_ref, w_ref, *out_refs):
    x = x_ref[0]
    y = x * lax.rsqrt(jnp.mean(x * x, axis=-1, keepdims=True) + RMS_EPS) * g_ref[...]
    h = (y * (1.0 + scale_ref[0]) + shift_ref[0]).astype(BF16)
    for (lo, hi), o_ref in zip(_IN_SEGMENTS, out_refs):
        o_ref[0] = _dot(h, w_ref[:, lo:hi]).astype(o_ref.dtype)


def _in_projection(x, shift, scale, g_pre, w_in_bf16):
    batch, seq, _ = x.shape
    tm = PROJ_ROWS
    row_spec = lambda width: pl.BlockSpec((1, tm, width), lambda b, i: (b, i, 0))
    vec_spec = pl.BlockSpec((1, 1, D_MODEL), lambda b, i: (b, 0, 0))
    out_dtypes = (F32, BF16, BF16, BF16, BF16, BF16)
    return pl.pallas_call(
        _in_proj_kernel,
        grid=(batch, seq // tm),
        in_specs=[
            row_spec(D_MODEL), vec_spec, vec_spec,
            pl.BlockSpec((1, D_MODEL), lambda b, i: (0, 0)),
            pl.BlockSpec((D_MODEL, D_IN), lambda b, i: (0, 0)),
        ],
        out_specs=[row_spec(hi - lo) for lo, hi in _IN_SEGMENTS],
        out_shape=[jax.ShapeDtypeStruct((batch, seq, hi - lo), dt)
                   for (lo, hi), dt in zip(_IN_SEGMENTS, out_dtypes)],
        compiler_params=pltpu.CompilerParams(
            dimension_semantics=("parallel", "parallel"),
            vmem_limit_bytes=VMEM_LIMIT),
        name="in_proj",
    )(x, shift, scale, g_pre, w_in_bf16)


def _rwkv_kernel(feat_ref, gate_ref, mu_ref, w0_ref, ww2_ref, a0_ref, wa2_ref,
                 kk_ref, ka_ref, rk_ref, lnw_ref, lnb_ref, seg_ref, y_ref, ubuf, state):
    C, W, N, P2 = RW_CHUNK, RW_WIDTH, RW_HEAD, RW_PAIR
    batch = feat_ref.shape[0]
    ci = pl.program_id(0)

    @pl.when(ci == 0)
    def _():
        ubuf[:, 0:8, :] = jnp.zeros((batch, 8, RW_SHIFTED), F32)
        state[...] = jnp.zeros(state.shape, F32)

    iota = lambda shape, axis: lax.broadcasted_iota(jnp.int32, shape, axis)
    row, col = iota((2 * P2, 2 * P2), 0), iota((2 * P2, 2 * P2), 1)
    ti, tj = row & (C - 1), col & (C - 1)
    tri = (ti > tj) | ((row >= P2) & (ti == tj))
    r1, c1 = iota((P2, P2), 0), iota((P2, P2), 1)
    eye = (r1 == c1).astype(F32)
    levels = []
    for bit in range(C.bit_length() - 1):
        levels.append(((r1 >> (bit + 1)) == (c1 >> (bit + 1)))
                      & (((r1 >> bit) & 1) == 1) & (((c1 >> bit) & 1) == 0))
    lo = iota((C, P2), 1) < N
    cum = (iota((C, C), 0) >= iota((C, C), 1)).astype(BF16)
    seg = seg_ref[...]

    def stack(t):
        return jnp.concatenate([jnp.where(lo, t, 0.0), jnp.where(lo, 0.0, t)],
                               axis=0).astype(BF16)

    pairs = range(W // P2)
    prep = []
    chains = []
    for b in range(batch):
        u = feat_ref[b]
        ubuf[b, 8:8 + C, :] = u
        prev = ubuf[b, 7:7 + C, :]
        ubuf[b, 0:8, :] = u[C - 8:C, :]
        feat = u + (prev - u) * mu_ref[...]
        r, k, v = feat[:, 0:W], feat[:, W:2 * W], feat[:, 2 * W:3 * W]
        w_lo = feat[:, 3 * W:3 * W + RW_RANK]
        a_lo = feat[:, 3 * W + RW_RANK:]

        z = w0_ref[...] + _dot(jnp.tanh(w_lo).astype(BF16), ww2_ref[...])
        logw = -math.exp(-0.5) * jax.nn.sigmoid(z)
        a = jax.nn.sigmoid(a0_ref[...] + _dot(a_lo.astype(BF16), wa2_ref[...]))
        kk_raw = k * kk_ref[...]
        norm = jnp.sqrt(_split_dot(kk_raw * kk_raw, seg, 2, x_is_lhs=True))
        kk = kk_raw / jnp.maximum(norm, 1e-12)
        bb = a * kk
        k2 = k * (1.0 + (a - 1.0) * ka_ref[...])

        cs = _split_dot(logw, cum, 3, x_is_lhs=False)
        cs_last = cs[C - 1:C, :]
        g_all = jnp.exp(cs_last)
        g_inv = jnp.exp(-cs)
        g_end = jnp.exp(cs_last - cs)
        kk_t = kk * jnp.exp(cs - logw)
        r_t = r * jnp.exp(cs)
        k_inv, b_inv = k2 * g_inv, bb * g_inv
        k_end, nb_end = k2 * g_end, -(bb * g_end)

        prep.append((r, k2, v))
        for p in pairs:
            tl = slice(p * P2, (p + 1) * P2)
            chains.append(dict(
                b=b, p=p, g_all=g_all[:, tl], v_s=stack(v[:, tl]),
                lhs=jnp.concatenate([stack(kk_t[:, tl]), stack(r_t[:, tl])], axis=0),
                rhs=jnp.concatenate([stack(b_inv[:, tl]), stack(k_inv[:, tl])], axis=0),
                end=jnp.concatenate([stack(k_end[:, tl]), stack(nb_end[:, tl])], axis=0)))

    for ch in chains:
        m = jnp.where(tri, _dot_nt(ch["lhs"], ch["rhs"]), 0.0)
        ch["a_kb"], ch["a_rb"] = m[:P2, :P2], m[P2:, :P2].astype(BF16)
        ch["a_v"] = m[:, P2:].astype(BF16)
        ch["t"] = eye - jnp.where(levels[0], ch["a_kb"], 0.0)
    for ch in chains:
        ch["st"] = state[ch["b"], ch["p"]]
        ch["x"] = _dot(ch["a_v"], ch["v_s"]) + _dot_nt(ch["lhs"], ch["st"].astype(BF16))
    for level in levels[1:]:
        for ch in chains:
            ch["tb"] = ch["t"].astype(BF16)
            off = jnp.where(level, ch["a_kb"], 0.0).astype(BF16)
            ch["inner"] = _dot(ch["tb"], off).astype(BF16)
        for ch in chains:
            ch["t"] = ch["t"] - _dot(ch["inner"], ch["tb"])
    for ch in chains:
        ch["uu"] = _dot(ch["t"].astype(BF16), ch["x"][:P2].astype(BF16)).astype(BF16)
    for ch in chains:
        o2 = ch["x"][P2:] - _dot(ch["a_rb"], ch["uu"])
        ch["o"] = o2[:C] + o2[C:]
    for ch in chains:
        state[ch["b"], ch["p"]] = ch["st"] * ch["g_all"] + _dot_tn(
            jnp.concatenate([ch["v_s"], ch["uu"]], axis=0), ch["end"])

    for b in range(batch):
        r, k2, v = prep[b]
        o = jnp.concatenate([ch["o"] for ch in chains if ch["b"] == b], axis=-1)
        mean = _split_dot(o, seg, 2, x_is_lhs=True) * (1.0 / N)
        d = o - mean
        var = _split_dot(d * d, seg, 2, x_is_lhs=True) * (1.0 / N)
        o = d * lax.rsqrt(var + GN_EPS) * lnw_ref[...] + lnb_ref[...]
        bonus = _split_dot(r * k2 * rk_ref[...], seg, 2, x_is_lhs=True) * v
        y_ref[b] = ((o + bonus) * _silu(gate_ref[b].astype(F32))).astype(y_ref.dtype)


def _rwkv_branch(feat, gate, mu, w0, w_w2, a0, w_a2, k_k, k_a, r_k, lnx_w, lnx_b):
    batch, seq, _ = feat.shape
    C = RW_CHUNK
    full = lambda shape: pl.BlockSpec(shape, lambda i: (0,) * len(shape))
    vec = lambda t: t.reshape(1, -1)
    head_of = jnp.arange(RW_WIDTH) // RW_HEAD
    seg = (head_of[:, None] == head_of[None, :]).astype(BF16)
    return pl.pallas_call(
        _rwkv_kernel,
        grid=(seq // C,),
        in_specs=[
            pl.BlockSpec((batch, C, RW_SHIFTED), lambda i: (0, i, 0)),
            pl.BlockSpec((batch, C, RW_WIDTH), lambda i: (0, i, 0)),
            full((1, RW_SHIFTED)), full((1, RW_WIDTH)), full((RW_RANK, RW_WIDTH)),
            full((1, RW_WIDTH)), full((RW_RANK, RW_WIDTH)),
            full((1, RW_WIDTH)), full((1, RW_WIDTH)), full((1, RW_WIDTH)),
            full((1, RW_WIDTH)), full((1, RW_WIDTH)), full((RW_WIDTH, RW_WIDTH)),
        ],
        out_specs=pl.BlockSpec((batch, C, RW_WIDTH), lambda i: (0, i, 0)),
        out_shape=jax.ShapeDtypeStruct((batch, seq, RW_WIDTH), BF16),
        scratch_shapes=[
            pltpu.VMEM((batch, C + 8, RW_SHIFTED), F32),
            pltpu.VMEM((batch, RW_WIDTH // RW_PAIR, RW_PAIR, RW_PAIR), F32),
        ],
        compiler_params=pltpu.CompilerParams(
            dimension_semantics=("arbitrary",),
            vmem_limit_bytes=VMEM_LIMIT),
        name="rwkv7_chunked",
    )(feat, gate, vec(mu), vec(w0), w_w2.astype(BF16), vec(a0), w_a2.astype(BF16),
      vec(k_k), vec(k_a), vec(r_k), vec(lnx_w), vec(lnx_b), seg)


def _attn_kernel(lq1_ref, lk1_ref, lq2_ref, lk2_ref, sw_ref, q_ref, k_ref, v_ref,
                 g_ref, o_ref, *, lambda_init):
    T = ATT_TILE
    qi = pl.program_id(2)
    lane = lax.broadcasted_iota(jnp.int32, (T, DA_VDIM), 1)
    q = q_ref[0] * (1.0 / math.sqrt(DA_QKDIM))
    zero = jnp.zeros_like(q)
    q_c = (jnp.where(lane < DA_QKDIM, q, zero), jnp.where(lane >= DA_QKDIM, q, zero))

    def update(carry, j, mask):
        start = pl.multiple_of(j * T, T)
        kb = k_ref[0, pl.ds(start, T), :]
        vb = v_ref[0, pl.ds(start, T), :]
        new = []
        for c in range(2):
            m, l, acc = carry[c]
            s = _dot_nt(q_c[c], kb)
            if mask is not None:
                s = jnp.where(mask, s, -jnp.inf)
            m_new = jnp.maximum(m, jnp.max(s, axis=-1, keepdims=True))
            alpha = jnp.exp(m - m_new)
            p = jnp.exp(s - m_new)
            l = alpha * l + jnp.sum(p, axis=-1, keepdims=True)
            acc = alpha * acc + _dot(p.astype(BF16), vb)
            new.append((m_new, l, acc))
        return tuple(new)

    init = tuple((jnp.full((T, 1), -jnp.inf, F32), jnp.zeros((T, 1), F32),
                  jnp.zeros((T, DA_VDIM), F32)) for _ in range(2))
    carry = lax.fori_loop(0, qi, lambda j, cr: update(cr, j, None), init)
    row = lax.broadcasted_iota(jnp.int32, (T, T), 0)
    col = lax.broadcasted_iota(jnp.int32, (T, T), 1)
    carry = update(carry, qi, (row // CHUNK) >= (col // CHUNK))

    lam = (jnp.exp(jnp.sum(lq1_ref[...] * lk1_ref[...], axis=-1, keepdims=True))
           - jnp.exp(jnp.sum(lq2_ref[...] * lk2_ref[...], axis=-1, keepdims=True))
           + lambda_init)
    (_, l0, acc0), (_, l1, acc1) = carry
    o = acc0 / l0 - lam * (acc1 / l1)
    o = o * lax.rsqrt(jnp.mean(o * o, axis=-1, keepdims=True) + SUBLN_EPS)
    o = o * sw_ref[...] * (1.0 - lambda_init)
    o_ref[0] = (o * _silu(g_ref[0].astype(F32))).astype(o_ref.dtype)


def _diff_attention(q, k, v, gate, lam_q1, lam_k1, lam_q2, lam_k2, subln_w, lambda_init):
    batch, seq, _ = q.shape
    T = ATT_TILE
    small = lambda n: pl.BlockSpec((1, n), lambda b, h, i: (0, 0))
    tile = pl.BlockSpec((1, T, DA_VDIM), lambda b, h, i: (b, i, h))
    whole = pl.BlockSpec((1, seq, DA_VDIM), lambda b, h, i: (b, 0, h))
    vec = lambda t: t.reshape(1, -1)
    return pl.pallas_call(
        functools.partial(_attn_kernel, lambda_init=lambda_init),
        grid=(batch, DA_HEADS, seq // T),
        in_specs=[small(DA_QKDIM)] * 4 + [small(DA_VDIM), tile, whole, whole, tile],
        out_specs=tile,
        out_shape=jax.ShapeDtypeStruct((batch, seq, DA_WIDTH), BF16),
        compiler_params=pltpu.CompilerParams(
            dimension_semantics=("parallel", "parallel", "arbitrary"),
            vmem_limit_bytes=VMEM_LIMIT),
        name="diff_attn",
    )(vec(lam_q1), vec(lam_k1), vec(lam_q2), vec(lam_k2), vec(subln_w), q, k, v, gate)


def _out_proj_kernel(yr_ref, yd_ref, wr_ref, wd_ref, g_ref, gate_ref, x_ref, o_ref):
    y = _dot(yr_ref[0], wr_ref[...]) + _dot(yd_ref[0], wd_ref[...])
    y = y * lax.rsqrt(jnp.mean(y * y, axis=-1, keepdims=True) + RMS_EPS) * g_ref[...]
    o_ref[0] = x_ref[0] + gate_ref[0] * y


def _out_projection(y_rw, y_da, w_out_bf16, g_post, gate, x):
    batch, seq, _ = x.shape
    tm = PROJ_ROWS
    half = lambda j: pl.BlockSpec((RW_WIDTH, D_MODEL), lambda b, i: (j, 0))
    row_spec = lambda width: pl.BlockSpec((1, tm, width), lambda b, i: (b, i, 0))
    return pl.pallas_call(
        _out_proj_kernel,
        grid=(batch, seq // tm),
        in_specs=[
            row_spec(RW_WIDTH), row_spec(DA_WIDTH), half(0), half(1),
            pl.BlockSpec((1, D_MODEL), lambda b, i: (0, 0)),
            pl.BlockSpec((1, 1, D_MODEL), lambda b, i: (b, 0, 0)),
            row_spec(D_MODEL),
        ],
        out_specs=row_spec(D_MODEL),
        out_shape=jax.ShapeDtypeStruct(x.shape, x.dtype),
        compiler_params=pltpu.CompilerParams(
            dimension_semantics=("parallel", "parallel"),
            vmem_limit_bytes=VMEM_LIMIT),
        name="out_proj",
    )(y_rw, y_da, w_out_bf16, w_out_bf16, g_post, gate, x)


def kernel(x, c, w_ada, b_ada, g_pre, g_post, w_in, w_out, rw_mu, rw_w0, rw_w_w2,
           rw_a0, rw_w_a2, rw_k_k, rw_k_a, rw_r_k, rw_lnx_w, rw_lnx_b,
           da_lam_q1, da_lam_k1, da_lam_q2, da_lam_k2, da_subln_w):
    batch = x.shape[0]
    depth = w_in.shape[0]
    for l in range(depth):
        lambda_init = 0.8 - 0.6 * math.exp(-0.3 * l)
        mod = _modulation(c, w_ada[l], b_ada[l]).reshape(batch, 3, 1, D_MODEL)
        shift, scale, gate = mod[:, 0], mod[:, 1], mod[:, 2]
        feat, rw_gate, dq, dk, dv, dg = _in_projection(
            x, shift, scale, g_pre[l].reshape(1, D_MODEL), w_in[l].astype(BF16))
        y_rw = _rwkv_branch(feat, rw_gate, rw_mu[l], rw_w0[l], rw_w_w2[l], rw_a0[l],
                            rw_w_a2[l], rw_k_k[l], rw_k_a[l], rw_r_k[l],
                            rw_lnx_w[l], rw_lnx_b[l])
        y_da = _diff_attention(dq, dk, dv, dg, da_lam_q1[l], da_lam_k1[l],
                               da_lam_q2[l], da_lam_k2[l], da_subln_w[l], lambda_init)
        x = _out_projection(y_rw, y_da, w_out[l].astype(BF16),
                            g_post[l].reshape(1, D_MODEL), gate, x)
    return x
```

```python
import functools
import math

import jax
import jax.numpy as jnp
from jax import lax
from jax.experimental import pallas as pl
from jax.experimental.pallas import tpu as pltpu

D_MODEL = 1024
SEQ = 4096
CHUNK = 64
RW_WIDTH = 512
RW_HEAD = 64
RW_HEADS = RW_WIDTH // RW_HEAD
RW_RANK = 64
RW_SHIFTED = 3 * RW_WIDTH + 2 * RW_RANK
RW_COLS = RW_SHIFTED + RW_WIDTH
DA_WIDTH = 512
DA_HEADS = 4
DA_VDIM = DA_WIDTH // DA_HEADS
DA_QKDIM = DA_VDIM // 2
D_IN = RW_COLS + 4 * DA_WIDTH
RMS_EPS = 1e-6
GN_EPS = 64e-5
SUBLN_EPS = 1e-5

PROJ_ROWS = 256
RW_CHUNK = 64
RW_PAIR = 2 * RW_HEAD
ATT_TILE = 512
VMEM_LIMIT = 48 * 1024 * 1024

F32 = jnp.float32
BF16 = jnp.bfloat16
HIGHEST = lax.Precision.HIGHEST


def _silu(t):
    return t * jax.nn.sigmoid(t)


def _dot(a, b, precision=None):
    return jnp.dot(a, b, preferred_element_type=F32, precision=precision)


def _dot_nt(a, b):
    return lax.dot_general(a, b, (((1,), (1,)), ((), ())), preferred_element_type=F32)


def _dot_tn(a, b):
    return lax.dot_general(a, b, (((0,), (0,)), ((), ())), preferred_element_type=F32)


def _split_dot(x, w_exact, terms, *, x_is_lhs):
    acc = None
    for _ in range(terms):
        piece = x.astype(BF16)
        part = _dot(piece, w_exact) if x_is_lhs else _dot(w_exact, piece)
        acc = part if acc is None else acc + part
        x = x - piece.astype(F32)
    return acc


def _mod_kernel(c_ref, w_ref, b_ref, o_ref):
    o_ref[...] = _dot(_silu(c_ref[...]), w_ref[...], HIGHEST) + b_ref[...]


def _modulation(c, w_ada, b_ada):
    batch = c.shape[0]
    return pl.pallas_call(
        _mod_kernel,
        grid=(3,),
        in_specs=[
            pl.BlockSpec((batch, D_MODEL), lambda j: (0, 0)),
            pl.BlockSpec((D_MODEL, D_MODEL), lambda j: (0, j)),
            pl.BlockSpec((1, D_MODEL), lambda j: (0, j)),
        ],
        out_specs=pl.BlockSpec((batch, D_MODEL), lambda j: (0, j)),
        out_shape=jax.ShapeDtypeStruct((batch, 3 * D_MODEL), F32),
        compiler_params=pltpu.CompilerParams(vmem_limit_bytes=VMEM_LIMIT),
        name="adaln_mod",
    )(c, w_ada, b_ada.reshape(1, 3 * D_MODEL))


_Q_COLS = (RW_COLS, RW_COLS + DA_WIDTH)
_V_COLS = (RW_COLS + 2 * DA_WIDTH, RW_COLS + 3 * DA_WIDTH)
_ROW_SEGMENTS = (
    (0, RW_SHIFTED),
    (RW_SHIFTED, RW_COLS),
    (RW_COLS + DA_WIDTH, RW_COLS + 2 * DA_WIDTH),
    (RW_COLS + 3 * DA_WIDTH, D_IN),
)
_ROW_WIDTHS = tuple(hi - lo for lo, hi in _ROW_SEGMENTS)
_ROW_DTYPES = (F32, BF16, BF16, BF16)
Q_SCALE = math.log2(math.e) / math.sqrt(DA_QKDIM)


def _in_proj_kernel(x_ref, shift_ref, scale_ref, g_ref, w_ref, wt_ref, *out_refs):
    x = x_ref[0]
    y = x * lax.rsqrt(jnp.mean(x * x, axis=-1, keepdims=True) + RMS_EPS) * g_ref[...]
    h = (y * (1.0 + scale_ref[0]) + shift_ref[0]).astype(BF16)
    lo = 0
    for width, o_ref in zip(_ROW_WIDTHS, out_refs):
        o_ref[0] = _dot(h, w_ref[:, lo:lo + width]).astype(o_ref.dtype)
        lo += width
    qt_ref, vt_ref = out_refs[len(_ROW_WIDTHS):]
    tm = h.shape[0]
    qt = _dot_nt(wt_ref[0:DA_WIDTH, :], h) * Q_SCALE
    vt = _dot_nt(wt_ref[DA_WIDTH:, :], h)
    qt_ref[0, :, 0] = qt.astype(BF16).reshape(DA_HEADS, DA_VDIM, tm)
    vt_ref[0, :, 0] = vt.astype(BF16).reshape(DA_HEADS, DA_VDIM, tm)


def _in_projection(x, shift, scale, g_pre, w_in):
    batch, seq, _ = x.shape
    tm = PROJ_ROWS
    w_rows = jnp.concatenate([w_in[:, lo:hi] for lo, hi in _ROW_SEGMENTS], axis=1).astype(BF16)
    w_t = jnp.concatenate([w_in[:, lo:hi] for lo, hi in (_Q_COLS, _V_COLS)], axis=1).T.astype(BF16)
    row_spec = lambda width: pl.BlockSpec((1, tm, width), lambda b, i: (b, i, 0))
    vec_spec = pl.BlockSpec((1, 1, D_MODEL), lambda b, i: (b, 0, 0))
    per_tile = ATT_TILE // tm
    t_spec = pl.BlockSpec((1, DA_HEADS, 1, DA_VDIM, tm),
                          lambda b, i: (b, 0, i // per_tile, 0, i % per_tile))
    t_shape = jax.ShapeDtypeStruct((batch, DA_HEADS, seq // ATT_TILE, DA_VDIM, ATT_TILE), BF16)
    return pl.pallas_call(
        _in_proj_kernel,
        grid=(batch, seq // tm),
        in_specs=[
            row_spec(D_MODEL), vec_spec, vec_spec,
            pl.BlockSpec((1, D_MODEL), lambda b, i: (0, 0)),
            pl.BlockSpec(w_rows.shape, lambda b, i: (0, 0)),
            pl.BlockSpec(w_t.shape, lambda b, i: (0, 0)),
        ],
        out_specs=[row_spec(w) for w in _ROW_WIDTHS] + [t_spec, t_spec],
        out_shape=[jax.ShapeDtypeStruct((batch, seq, w), dt)
                   for w, dt in zip(_ROW_WIDTHS, _ROW_DTYPES)] + [t_shape, t_shape],
        compiler_params=pltpu.CompilerParams(
            dimension_semantics=("parallel", "parallel"),
            vmem_limit_bytes=VMEM_LIMIT),
        name="in_proj",
    )(x, shift, scale, g_pre, w_rows, w_t)


def _rwkv_kernel(feat_ref, gate_ref, mu_ref, w0_ref, ww2_ref, a0_ref, wa2_ref,
                 kk_ref, ka_ref, rk_ref, lnw_ref, lnb_ref, seg_ref, y_ref, ubuf, state):
    C, W, N, P2 = RW_CHUNK, RW_WIDTH, RW_HEAD, RW_PAIR
    batch = feat_ref.shape[0]
    ci = pl.program_id(0)

    @pl.when(ci == 0)
    def _():
        ubuf[:, 0:8, :] = jnp.zeros((batch, 8, RW_SHIFTED), F32)
        state[...] = jnp.zeros(state.shape, F32)

    iota = lambda shape, axis: lax.broadcasted_iota(jnp.int32, shape, axis)
    row, col = iota((2 * P2, 2 * P2), 0), iota((2 * P2, 2 * P2), 1)
    ti, tj = row & (C - 1), col & (C - 1)
    tri = (ti > tj) | ((row >= P2) & (ti == tj))
    r1, c1 = iota((P2, P2), 0), iota((P2, P2), 1)
    eye = (r1 == c1).astype(F32)
    levels = []
    for bit in range(C.bit_length() - 1):
        levels.append(((r1 >> (bit + 1)) == (c1 >> (bit + 1)))
                      & (((r1 >> bit) & 1) == 1) & (((c1 >> bit) & 1) == 0))
    lo = iota((C, P2), 1) < N
    cum = (iota((C, C), 0) >= iota((C, C), 1)).astype(BF16)
    seg = seg_ref[...]

    def stack(t):
        return jnp.concatenate([jnp.where(lo, t, 0.0), jnp.where(lo, 0.0, t)],
                               axis=0).astype(BF16)

    pairs = range(W // P2)
    prep = []
    chains = []
    for b in range(batch):
        u = feat_ref[b]
        ubuf[b, 8:8 + C, :] = u
        prev = ubuf[b, 7:7 + C, :]
        ubuf[b, 0:8, :] = u[C - 8:C, :]
        feat = u + (prev - u) * mu_ref[...]
        r, k, v = feat[:, 0:W], feat[:, W:2 * W], feat[:, 2 * W:3 * W]
        w_lo = feat[:, 3 * W:3 * W + RW_RANK]
        a_lo = feat[:, 3 * W + RW_RANK:]

        z = w0_ref[...] + _dot(jnp.tanh(w_lo).astype(BF16), ww2_ref[...])
        logw = -math.exp(-0.5) * jax.nn.sigmoid(z)
        a = jax.nn.sigmoid(a0_ref[...] + _dot(a_lo.astype(BF16), wa2_ref[...]))
        kk_raw = k * kk_ref[...]
        norm = jnp.sqrt(_split_dot(kk_raw * kk_raw, seg, 2, x_is_lhs=True))
        kk = kk_raw / jnp.maximum(norm, 1e-12)
        bb = a * kk
        k2 = k * (1.0 + (a - 1.0) * ka_ref[...])

        cs = _split_dot(logw, cum, 3, x_is_lhs=False)
        cs_last = cs[C - 1:C, :]
        g_all = jnp.exp(cs_last)
        g_inv = jnp.exp(-cs)
        g_end = jnp.exp(cs_last - cs)
        kk_t = kk * jnp.exp(cs - logw)
        r_t = r * jnp.exp(cs)
        k_inv, b_inv = k2 * g_inv, bb * g_inv
        k_end, nb_end = k2 * g_end, -(bb * g_end)

        prep.append((r, k2, v))
        for p in pairs:
            tl = slice(p * P2, (p + 1) * P2)
            chains.append(dict(
                b=b, p=p, g_all=g_all[:, tl], v_s=stack(v[:, tl]),
                lhs=jnp.concatenate([stack(kk_t[:, tl]), stack(r_t[:, tl])], axis=0),
                rhs=jnp.concatenate([stack(b_inv[:, tl]), stack(k_inv[:, tl])], axis=0),
                end=jnp.concatenate([stack(k_end[:, tl]), stack(nb_end[:, tl])], axis=0)))

    for ch in chains:
        m = jnp.where(tri, _dot_nt(ch["lhs"], ch["rhs"]), 0.0)
        ch["a_kb"], ch["a_rb"] = m[:P2, :P2], m[P2:, :P2].astype(BF16)
        ch["a_v"] = m[:, P2:].astype(BF16)
        ch["t"] = eye - jnp.where(levels[0], ch["a_kb"], 0.0)
    for ch in chains:
        ch["st"] = state[ch["b"], ch["p"]]
        ch["x"] = _dot(ch["a_v"], ch["v_s"]) + _dot_nt(ch["lhs"], ch["st"].astype(BF16))
    for level in levels[1:]:
        for ch in chains:
            ch["tb"] = ch["t"].astype(BF16)
            off = jnp.where(level, ch["a_kb"], 0.0).astype(BF16)
            ch["inner"] = _dot(ch["tb"], off).astype(BF16)
        for ch in chains:
            ch["t"] = ch["t"] - _dot(ch["inner"], ch["tb"])
    for ch in chains:
        ch["uu"] = _dot(ch["t"].astype(BF16), ch["x"][:P2].astype(BF16)).astype(BF16)
    for ch in chains:
        o2 = ch["x"][P2:] - _dot(ch["a_rb"], ch["uu"])
        ch["o"] = o2[:C] + o2[C:]
    for ch in chains:
        state[ch["b"], ch["p"]] = ch["st"] * ch["g_all"] + _dot_tn(
            jnp.concatenate([ch["v_s"], ch["uu"]], axis=0), ch["end"])

    for b in range(batch):
        r, k2, v = prep[b]
        o = jnp.concatenate([ch["o"] for ch in chains if ch["b"] == b], axis=-1)
        mean = _split_dot(o, seg, 2, x_is_lhs=True) * (1.0 / N)
        d = o - mean
        var = _split_dot(d * d, seg, 2, x_is_lhs=True) * (1.0 / N)
        o = d * lax.rsqrt(var + GN_EPS) * lnw_ref[...] + lnb_ref[...]
        bonus = _split_dot(r * k2 * rk_ref[...], seg, 2, x_is_lhs=True) * v
        y_ref[b] = ((o + bonus) * _silu(gate_ref[b].astype(F32))).astype(y_ref.dtype)


def _rwkv_branch(feat, gate, mu, w0, w_w2, a0, w_a2, k_k, k_a, r_k, lnx_w, lnx_b):
    batch, seq, _ = feat.shape
    C = RW_CHUNK
    full = lambda shape: pl.BlockSpec(shape, lambda i: (0,) * len(shape))
    vec = lambda t: t.reshape(1, -1)
    head_of = jnp.arange(RW_WIDTH) // RW_HEAD
    seg = (head_of[:, None] == head_of[None, :]).astype(BF16)
    return pl.pallas_call(
        _rwkv_kernel,
        grid=(seq // C,),
        in_specs=[
            pl.BlockSpec((batch, C, RW_SHIFTED), lambda i: (0, i, 0)),
            pl.BlockSpec((batch, C, RW_WIDTH), lambda i: (0, i, 0)),
            full((1, RW_SHIFTED)), full((1, RW_WIDTH)), full((RW_RANK, RW_WIDTH)),
            full((1, RW_WIDTH)), full((RW_RANK, RW_WIDTH)),
            full((1, RW_WIDTH)), full((1, RW_WIDTH)), full((1, RW_WIDTH)),
            full((1, RW_WIDTH)), full((1, RW_WIDTH)), full((RW_WIDTH, RW_WIDTH)),
        ],
        out_specs=pl.BlockSpec((batch, C, RW_WIDTH), lambda i: (0, i, 0)),
        out_shape=jax.ShapeDtypeStruct((batch, seq, RW_WIDTH), BF16),
        scratch_shapes=[
            pltpu.VMEM((batch, C + 8, RW_SHIFTED), F32),
            pltpu.VMEM((batch, RW_WIDTH // RW_PAIR, RW_PAIR, RW_PAIR), F32),
        ],
        compiler_params=pltpu.CompilerParams(
            dimension_semantics=("arbitrary",),
            vmem_limit_bytes=VMEM_LIMIT),
        name="rwkv7_chunked",
    )(feat, gate, vec(mu), vec(w0), w_w2.astype(BF16), vec(a0), w_a2.astype(BF16),
      vec(k_k), vec(k_a), vec(r_k), vec(lnx_w), vec(lnx_b), seg)


def _attn_kernel(lq1_ref, lk1_ref, lq2_ref, lk2_ref, sw_ref, qt_ref, k_ref, vt_ref,
                 g_ref, o_ref, acc_ref, m_ref, l_ref, sa_ref, sb_ref, *, lambda_init):
    T = ATT_TILE
    qi = pl.program_id(2)
    qt = qt_ref[0, 0, 0]
    comp = lax.broadcasted_iota(jnp.int32, qt.shape, 0) < DA_QKDIM
    zero = jnp.zeros_like(qt)
    qt_c = (jnp.where(comp, qt, zero), jnp.where(comp, zero, qt))
    acc_ref[...] = jnp.zeros(acc_ref.shape, F32)
    m_ref[...] = jnp.full(m_ref.shape, -jnp.inf, F32)
    l_ref[...] = jnp.zeros(l_ref.shape, F32)

    def scores(j, dst):
        kb = k_ref[0, pl.ds(pl.multiple_of(j * T, T), T), :]
        for c in range(2):
            dst[c] = _dot(kb, qt_c[c])

    def update(j, src, mask):
        vtb = vt_ref[0, 0, j]
        for c in range(2):
            s = src[c]
            if mask is not None:
                s = jnp.where(mask, s, -jnp.inf)
            m = m_ref[c]
            m_new = jnp.maximum(m, jnp.max(s, axis=0, keepdims=True))
            alpha = jnp.exp2(m - m_new)
            p = jnp.exp2(s - m_new)
            m_ref[c] = m_new
            l_ref[c] = alpha * l_ref[c] + jnp.sum(p, axis=0, keepdims=True)
            acc_ref[c] = alpha * acc_ref[c] + _dot(vtb, p.astype(BF16))

    def diagonal():
        key = lax.broadcasted_iota(jnp.int32, (T, T), 0)
        query = lax.broadcasted_iota(jnp.int32, (T, T), 1)
        return (query // CHUNK) >= (key // CHUNK)

    def two_blocks(i, carry):
        j = 2 * i
        scores(j + 1, sb_ref)
        update(j, sa_ref, None)
        scores(j + 2, sa_ref)
        update(j + 1, sb_ref, None)
        return carry

    scores(0, sa_ref)
    lax.fori_loop(0, qi // 2, two_blocks, 0)

    @pl.when(qi % 2 == 1)
    def _():
        scores(qi, sb_ref)
        update(qi - 1, sa_ref, None)
        update(qi, sb_ref, diagonal())

    @pl.when(qi % 2 == 0)
    def _():
        update(qi, sa_ref, diagonal())

    l0, l1 = l_ref[0], l_ref[1]

    lam = (jnp.exp(jnp.sum(lq1_ref[...] * lk1_ref[...], axis=-1, keepdims=True))
           - jnp.exp(jnp.sum(lq2_ref[...] * lk2_ref[...], axis=-1, keepdims=True))
           + lambda_init)
    o = (acc_ref[0] / l0 - lam * (acc_ref[1] / l1)).T
    o = o * lax.rsqrt(jnp.mean(o * o, axis=-1, keepdims=True) + SUBLN_EPS)
    o = o * sw_ref[...] * (1.0 - lambda_init)
    o_ref[0] = (o * _silu(g_ref[0].astype(F32))).astype(o_ref.dtype)


def _diff_attention(qt, k, vt, gate, lam_q1, lam_k1, lam_q2, lam_k2, subln_w, lambda_init):
    batch, seq, _ = k.shape
    T = ATT_TILE
    small = lambda n: pl.BlockSpec((1, n), lambda b, h, i: (0, 0))
    tile = pl.BlockSpec((1, T, DA_VDIM), lambda b, h, i: (b, i, h))
    return pl.pallas_call(
        functools.partial(_attn_kernel, lambda_init=lambda_init),
        grid=(batch, DA_HEADS, seq // T),
        in_specs=[small(DA_QKDIM)] * 4 + [
            small(DA_VDIM),
            pl.BlockSpec((1, 1, 1, DA_VDIM, T), lambda b, h, i: (b, h, i, 0, 0)),
            pl.BlockSpec((1, seq, DA_VDIM), lambda b, h, i: (b, 0, h)),
            pl.BlockSpec((1, 1, seq // T, DA_VDIM, T), lambda b, h, i: (b, h, 0, 0, 0)),
            tile,
        ],
        out_specs=tile,
        out_shape=jax.ShapeDtypeStruct((batch, seq, DA_WIDTH), BF16),
        scratch_shapes=[pltpu.VMEM((2, DA_VDIM, T), F32),
                        pltpu.VMEM((2, 1, T), F32),
                        pltpu.VMEM((2, 1, T), F32),
                        pltpu.VMEM((2, T, T), F32),
                        pltpu.VMEM((2, T, T), F32)],
        compiler_params=pltpu.CompilerParams(
            dimension_semantics=("parallel", "parallel", "arbitrary"),
            vmem_limit_bytes=VMEM_LIMIT),
        name="diff_attn",
    )(*(t.reshape(1, -1) for t in (lam_q1, lam_k1, lam_q2, lam_k2, subln_w)),
      qt, k, vt, gate)


def _out_proj_kernel(yr_ref, yd_ref, wr_ref, wd_ref, g_ref, gate_ref, x_ref, o_ref):
    y = _dot(yr_ref[0], wr_ref[...]) + _dot(yd_ref[0], wd_ref[...])
    y = y * lax.rsqrt(jnp.mean(y * y, axis=-1, keepdims=True) + RMS_EPS) * g_ref[...]
    o_ref[0] = x_ref[0] + gate_ref[0] * y


def _out_projection(y_rw, y_da, w_out_bf16, g_post, gate, x):
    batch, seq, _ = x.shape
    tm = PROJ_ROWS
    half = lambda j: pl.BlockSpec((RW_WIDTH, D_MODEL), lambda b, i: (j, 0))
    row_spec = lambda width: pl.BlockSpec((1, tm, width), lambda b, i: (b, i, 0))
    return pl.pallas_call(
        _out_proj_kernel,
        grid=(batch, seq // tm),
        in_specs=[
            row_spec(RW_WIDTH), row_spec(DA_WIDTH), half(0), half(1),
            pl.BlockSpec((1, D_MODEL), lambda b, i: (0, 0)),
            pl.BlockSpec((1, 1, D_MODEL), lambda b, i: (b, 0, 0)),
            row_spec(D_MODEL),
        ],
        out_specs=row_spec(D_MODEL),
        out_shape=jax.ShapeDtypeStruct(x.shape, x.dtype),
        compiler_params=pltpu.CompilerParams(
            dimension_semantics=("parallel", "parallel"),
            vmem_limit_bytes=VMEM_LIMIT),
        name="out_proj",
    )(y_rw, y_da, w_out_bf16, w_out_bf16, g_post, gate, x)


def kernel(x, c, w_ada, b_ada, g_pre, g_post, w_in, w_out, rw_mu, rw_w0, rw_w_w2,
           rw_a0, rw_w_a2, rw_k_k, rw_k_a, rw_r_k, rw_lnx_w, rw_lnx_b,
           da_lam_q1, da_lam_k1, da_lam_q2, da_lam_k2, da_subln_w):
    batch = x.shape[0]
    depth = w_in.shape[0]
    for l in range(depth):
        lambda_init = 0.8 - 0.6 * math.exp(-0.3 * l)
        mod = _modulation(c, w_ada[l], b_ada[l]).reshape(batch, 3, 1, D_MODEL)
        shift, scale, gate = mod[:, 0], mod[:, 1], mod[:, 2]
        feat, rw_gate, dk, dg, dqt, dvt = _in_projection(
            x, shift, scale, g_pre[l].reshape(1, D_MODEL), w_in[l])
        y_rw = _rwkv_branch(feat, rw_gate, rw_mu[l], rw_w0[l], rw_w_w2[l], rw_a0[l],
                            rw_w_a2[l], rw_k_k[l], rw_k_a[l], rw_r_k[l],
                            rw_lnx_w[l], rw_lnx_b[l])
        y_da = _diff_attention(dqt, dk, dvt, dg, da_lam_q1[l], da_lam_k1[l],
                               da_lam_q2[l], da_lam_k2[l], da_subln_w[l], lambda_init)
        x = _out_projection(y_rw, y_da, w_out[l].astype(BF16),
                            g_post[l].reshape(1, D_MODEL), gate, x)
    return x
```

```python
import functools
import math

import jax
import jax.numpy as jnp
from jax import lax
from jax.experimental import pallas as pl
from jax.experimental.pallas import tpu as pltpu

D_MODEL = 1024
SEQ = 4096
CHUNK = 64
RW_WIDTH = 512
RW_HEAD = 64
RW_HEADS = RW_WIDTH // RW_HEAD
RW_RANK = 64
RW_SHIFTED = 3 * RW_WIDTH + 2 * RW_RANK
RW_COLS = RW_SHIFTED + RW_WIDTH
DA_WIDTH = 512
DA_HEADS = 4
DA_VDIM = DA_WIDTH // DA_HEADS
DA_QKDIM = DA_VDIM // 2
D_IN = RW_COLS + 4 * DA_WIDTH
RMS_EPS = 1e-6
GN_EPS = 64e-5
SUBLN_EPS = 1e-5

PROJ_ROWS = 256
RW_CHUNK = 64
RW_GROUP = 4 * RW_HEAD
ATT_TILE = 512
VMEM_LIMIT = 48 * 1024 * 1024

F32 = jnp.float32
BF16 = jnp.bfloat16
HIGHEST = lax.Precision.HIGHEST


def _silu(t):
    return t * jax.nn.sigmoid(t)


def _dot(a, b, precision=None):
    return jnp.dot(a, b, preferred_element_type=F32, precision=precision)


def _dot_nt(a, b):
    return lax.dot_general(a, b, (((1,), (1,)), ((), ())), preferred_element_type=F32)


def _dot_tn(a, b):
    return lax.dot_general(a, b, (((0,), (0,)), ((), ())), preferred_element_type=F32)


def _split_dot(x, w_exact, terms, *, x_is_lhs):
    acc = None
    for _ in range(terms):
        piece = x.astype(BF16)
        part = _dot(piece, w_exact) if x_is_lhs else _dot(w_exact, piece)
        acc = part if acc is None else acc + part
        x = x - piece.astype(F32)
    return acc


def _mod_kernel(c_ref, w_ref, b_ref, o_ref):
    o_ref[...] = _dot(_silu(c_ref[...]), w_ref[...], HIGHEST) + b_ref[...]


def _modulation(c, w_ada, b_ada):
    batch = c.shape[0]
    return pl.pallas_call(
        _mod_kernel,
        grid=(3,),
        in_specs=[
            pl.BlockSpec((batch, D_MODEL), lambda j: (0, 0)),
            pl.BlockSpec((D_MODEL, D_MODEL), lambda j: (0, j)),
            pl.BlockSpec((1, D_MODEL), lambda j: (0, j)),
        ],
        out_specs=pl.BlockSpec((batch, D_MODEL), lambda j: (0, j)),
        out_shape=jax.ShapeDtypeStruct((batch, 3 * D_MODEL), F32),
        compiler_params=pltpu.CompilerParams(vmem_limit_bytes=VMEM_LIMIT),
        name="adaln_mod",
    )(c, w_ada, b_ada.reshape(1, 3 * D_MODEL))


_Q_COLS = (RW_COLS, RW_COLS + DA_WIDTH)
_V_COLS = (RW_COLS + 2 * DA_WIDTH, RW_COLS + 3 * DA_WIDTH)
_ROW_SEGMENTS = (
    (0, RW_SHIFTED),
    (RW_SHIFTED, RW_COLS),
    (RW_COLS + DA_WIDTH, RW_COLS + 2 * DA_WIDTH),
    (RW_COLS + 3 * DA_WIDTH, D_IN),
)
_ROW_WIDTHS = tuple(hi - lo for lo, hi in _ROW_SEGMENTS)
_ROW_DTYPES = (F32, BF16, BF16, BF16)
Q_SCALE = math.log2(math.e) / math.sqrt(DA_QKDIM)


def _in_proj_kernel(x_ref, shift_ref, scale_ref, g_ref, w_ref, wt_ref, *out_refs):
    x = x_ref[0]
    y = x * lax.rsqrt(jnp.mean(x * x, axis=-1, keepdims=True) + RMS_EPS) * g_ref[...]
    h = (y * (1.0 + scale_ref[0]) + shift_ref[0]).astype(BF16)
    lo = 0
    for width, o_ref in zip(_ROW_WIDTHS, out_refs):
        o_ref[0] = _dot(h, w_ref[:, lo:lo + width]).astype(o_ref.dtype)
        lo += width
    qt_ref, vt_ref = out_refs[len(_ROW_WIDTHS):]
    tm = h.shape[0]
    qt = _dot_nt(wt_ref[0:DA_WIDTH, :], h) * Q_SCALE
    vt = _dot_nt(wt_ref[DA_WIDTH:, :], h)
    qt_ref[0, :, 0] = qt.astype(BF16).reshape(DA_HEADS, DA_VDIM, tm)
    vt_ref[0, :, 0] = vt.astype(BF16).reshape(DA_HEADS, DA_VDIM, tm)


def _in_projection(x, shift, scale, g_pre, w_in):
    batch, seq, _ = x.shape
    tm = PROJ_ROWS
    w_rows = jnp.concatenate([w_in[:, lo:hi] for lo, hi in _ROW_SEGMENTS], axis=1).astype(BF16)
    w_t = jnp.concatenate([w_in[:, lo:hi] for lo, hi in (_Q_COLS, _V_COLS)], axis=1).T.astype(BF16)
    row_spec = lambda width: pl.BlockSpec((1, tm, width), lambda b, i: (b, i, 0))
    vec_spec = pl.BlockSpec((1, 1, D_MODEL), lambda b, i: (b, 0, 0))
    per_tile = ATT_TILE // tm
    t_spec = pl.BlockSpec((1, DA_HEADS, 1, DA_VDIM, tm),
                          lambda b, i: (b, 0, i // per_tile, 0, i % per_tile))
    t_shape = jax.ShapeDtypeStruct((batch, DA_HEADS, seq // ATT_TILE, DA_VDIM, ATT_TILE), BF16)
    return pl.pallas_call(
        _in_proj_kernel,
        grid=(batch, seq // tm),
        in_specs=[
            row_spec(D_MODEL), vec_spec, vec_spec,
            pl.BlockSpec((1, D_MODEL), lambda b, i: (0, 0)),
            pl.BlockSpec(w_rows.shape, lambda b, i: (0, 0)),
            pl.BlockSpec(w_t.shape, lambda b, i: (0, 0)),
        ],
        out_specs=[row_spec(w) for w in _ROW_WIDTHS] + [t_spec, t_spec],
        out_shape=[jax.ShapeDtypeStruct((batch, seq, w), dt)
                   for w, dt in zip(_ROW_WIDTHS, _ROW_DTYPES)] + [t_shape, t_shape],
        compiler_params=pltpu.CompilerParams(
            dimension_semantics=("parallel", "parallel"),
            vmem_limit_bytes=VMEM_LIMIT),
        name="in_proj",
    )(x, shift, scale, g_pre, w_rows, w_t)


def _rwkv_kernel(feat_ref, gate_ref, mu_ref, w0_ref, ww2_ref, a0_ref, wa2_ref,
                 kk_ref, ka_ref, rk_ref, lnw_ref, lnb_ref, y_ref, ubuf, state):
    C, W, N, L = RW_CHUNK, RW_WIDTH, RW_HEAD, RW_GROUP
    heads = L // N
    batch = feat_ref.shape[0]
    ci = pl.program_id(0)

    @pl.when(ci == 0)
    def _():
        ubuf[:, 0:8, :] = jnp.zeros((batch, 8, RW_SHIFTED), F32)
        state[...] = jnp.zeros(state.shape, F32)

    iota = lambda shape, axis: lax.broadcasted_iota(jnp.int32, shape, axis)
    row2, col2 = iota((2 * C, L), 0), iota((2 * C, L), 1) & (N - 1)
    tri = ((row2 & (C - 1)) > col2) | ((row2 >= C) & ((row2 & (C - 1)) == col2))
    r1, c1 = iota((C, L), 0), iota((C, L), 1) & (N - 1)
    eye = (r1 == c1).astype(F32)
    levels = []
    for bit in range(C.bit_length() - 1):
        levels.append(((r1 >> (bit + 1)) == (c1 >> (bit + 1)))
                      & (((r1 >> bit) & 1) == 1) & (((c1 >> bit) & 1) == 0))
    head_bits = N.bit_length() - 1
    same_head = (iota((L, L), 0) >> head_bits) == (iota((L, L), 1) >> head_bits)
    seg = same_head.astype(BF16)
    cum = (iota((C, C), 0) >= iota((C, C), 1)).astype(BF16)

    def block_diag(t):
        return jnp.where(same_head, jnp.concatenate([t] * heads, axis=0), 0.0).astype(BF16)

    def head_sums(t):
        return jnp.concatenate(
            [_dot(t[:, g * L:(g + 1) * L].astype(BF16), seg) for g in range(W // L)], axis=-1)

    prep = []
    groups = []
    for b in range(batch):
        u = feat_ref[b]
        ubuf[b, 8:8 + C, :] = u
        prev = ubuf[b, 7:7 + C, :]
        ubuf[b, 0:8, :] = u[C - 8:C, :]
        feat = u + (prev - u) * mu_ref[...]
        r, k, v = feat[:, 0:W], feat[:, W:2 * W], feat[:, 2 * W:3 * W]
        w_lo = feat[:, 3 * W:3 * W + RW_RANK]
        a_lo = feat[:, 3 * W + RW_RANK:]

        z = w0_ref[...] + _dot(jnp.tanh(w_lo).astype(BF16), ww2_ref[...])
        logw = -math.exp(-0.5) * jax.nn.sigmoid(z)
        a = jax.nn.sigmoid(a0_ref[...] + _dot(a_lo.astype(BF16), wa2_ref[...]))
        kk_raw = k * kk_ref[...]
        kk = kk_raw * lax.rsqrt(jnp.maximum(head_sums(kk_raw * kk_raw), 1e-24))
        bb = a * kk
        k2 = k * (1.0 + (a - 1.0) * ka_ref[...])

        cs = _split_dot(logw, cum, 2, x_is_lhs=False)
        cs_last = cs[C - 1:C, :]
        g_all = jnp.exp(cs_last)
        g_inv = jnp.exp(-cs)
        g_end = jnp.exp(cs_last - cs)
        kk_t = (kk * jnp.exp(cs - logw)).astype(BF16)
        r_t = (r * jnp.exp(cs)).astype(BF16)
        k_inv, b_inv = k2 * g_inv, bb * g_inv
        k_end, nb_end = (k2 * g_end).astype(BF16), (-(bb * g_end)).astype(BF16)
        prep.append((r, k2, v))
        for g in range(W // L):
            gl = slice(g * L, (g + 1) * L)
            groups.append(dict(
                b=b, g=g, g_all=g_all[:, gl], v=v[:, gl],
                lhs=jnp.concatenate([kk_t[:, gl], r_t[:, gl]], axis=0),
                rhs=jnp.concatenate([block_diag(b_inv[:, gl]), block_diag(k_inv[:, gl])], axis=0),
                end=jnp.concatenate([k_end[:, gl], nb_end[:, gl]], axis=0)))

    for gr in groups:
        m = _dot_nt(gr["lhs"], gr["rhs"])
        gr["a_b"] = jnp.where(tri, m[:, :L], 0.0)
        gr["a_k"] = jnp.where(tri, m[:, L:], 0.0).astype(BF16)
        gr["t"] = eye - jnp.where(levels[0], gr["a_b"][:C], 0.0)
    for gr in groups:
        gr["st"] = state[gr["b"], gr["g"]]
        gr["x"] = (_dot(gr["a_k"], block_diag(gr["v"]))
                   + _dot_nt(gr["lhs"], block_diag(gr["st"])))
    for level in levels[1:]:
        for gr in groups:
            off = jnp.where(level, gr["a_b"][:C], 0.0)
            gr["inner"] = _dot(gr["t"].astype(BF16), block_diag(off)).astype(BF16)
        for gr in groups:
            gr["t"] = gr["t"] - _dot(gr["inner"], block_diag(gr["t"]))
    for gr in groups:
        gr["uu"] = _dot(gr["t"].astype(BF16), block_diag(gr["x"][:C]))
    for gr in groups:
        gr["o"] = gr["x"][C:] - _dot(gr["a_b"][C:].astype(BF16), block_diag(gr["uu"]))
    for gr in groups:
        vu = jnp.concatenate([gr["v"], gr["uu"]], axis=0).astype(BF16)
        delta = jnp.where(same_head, _dot_tn(vu, gr["end"]), 0.0)
        state[gr["b"], gr["g"]] = gr["st"] * gr["g_all"] + sum(
            delta[h * N:(h + 1) * N] for h in range(heads))

    for b in range(batch):
        r, k2, v = prep[b]
        o = jnp.concatenate([gr["o"] for gr in groups if gr["b"] == b], axis=-1)
        d = o - head_sums(o) * (1.0 / N)
        var = head_sums(d * d) * (1.0 / N)
        o = d * lax.rsqrt(var + GN_EPS) * lnw_ref[...] + lnb_ref[...]
        bonus = head_sums(r * k2 * rk_ref[...]) * v
        y_ref[b] = ((o + bonus) * _silu(gate_ref[b].astype(F32))).astype(y_ref.dtype)


def _rwkv_branch(feat, gate, mu, w0, w_w2, a0, w_a2, k_k, k_a, r_k, lnx_w, lnx_b):
    batch, seq, _ = feat.shape
    C = RW_CHUNK
    full = lambda shape: pl.BlockSpec(shape, lambda i: (0,) * len(shape))
    vec = lambda t: t.reshape(1, -1)
    return pl.pallas_call(
        _rwkv_kernel,
        grid=(seq // C,),
        in_specs=[
            pl.BlockSpec((batch, C, RW_SHIFTED), lambda i: (0, i, 0)),
            pl.BlockSpec((batch, C, RW_WIDTH), lambda i: (0, i, 0)),
            full((1, RW_SHIFTED)), full((1, RW_WIDTH)), full((RW_RANK, RW_WIDTH)),
            full((1, RW_WIDTH)), full((RW_RANK, RW_WIDTH)),
            full((1, RW_WIDTH)), full((1, RW_WIDTH)), full((1, RW_WIDTH)),
            full((1, RW_WIDTH)), full((1, RW_WIDTH)),
        ],
        out_specs=pl.BlockSpec((batch, C, RW_WIDTH), lambda i: (0, i, 0)),
        out_shape=jax.ShapeDtypeStruct((batch, seq, RW_WIDTH), BF16),
        scratch_shapes=[
            pltpu.VMEM((batch, C + 8, RW_SHIFTED), F32),
            pltpu.VMEM((batch, RW_WIDTH // RW_GROUP, RW_HEAD, RW_GROUP), F32),
        ],
        compiler_params=pltpu.CompilerParams(
            dimension_semantics=("arbitrary",),
            vmem_limit_bytes=VMEM_LIMIT),
        name="rwkv7_chunked",
    )(feat, gate, vec(mu), vec(w0), w_w2.astype(BF16), vec(a0), w_a2.astype(BF16),
      vec(k_k), vec(k_a), vec(r_k), vec(lnx_w), vec(lnx_b))


def _attn_kernel(lq1_ref, lk1_ref, lq2_ref, lk2_ref, sw_ref, qt_ref, k_ref, vt_ref,
                 g_ref, o_ref, acc_ref, m_ref, l_ref, sa_ref, sb_ref, *, lambda_init):
    T = ATT_TILE
    qi = pl.program_id(2)
    qt = qt_ref[0, 0, 0]
    comp = lax.broadcasted_iota(jnp.int32, qt.shape, 0) < DA_QKDIM
    zero = jnp.zeros_like(qt)
    qt_c = (jnp.where(comp, qt, zero), jnp.where(comp, zero, qt))
    acc_ref[...] = jnp.zeros(acc_ref.shape, F32)
    m_ref[...] = jnp.full(m_ref.shape, -jnp.inf, F32)
    l_ref[...] = jnp.zeros(l_ref.shape, F32)

    def scores(j, dst):
        kb = k_ref[0, pl.ds(pl.multiple_of(j * T, T), T), :]
        for c in range(2):
            dst[c] = _dot(kb, qt_c[c])

    def update(j, src, mask):
        vtb = vt_ref[0, 0, j]
        for c in range(2):
            s = src[c]
            if mask is not None:
                s = jnp.where(mask, s, -jnp.inf)
            m = m_ref[c]
            m_new = jnp.maximum(m, jnp.max(s, axis=0, keepdims=True))
            alpha = jnp.exp2(m - m_new)
            p = jnp.exp2(s - m_new)
            m_ref[c] = m_new
            l_ref[c] = alpha * l_ref[c] + jnp.sum(p, axis=0, keepdims=True)
            acc_ref[c] = alpha * acc_ref[c] + _dot(vtb, p.astype(BF16))

    def diagonal():
        key = lax.broadcasted_iota(jnp.int32, (T, T), 0)
        query = lax.broadcasted_iota(jnp.int32, (T, T), 1)
        return (query // CHUNK) >= (key // CHUNK)

    def two_blocks(i, carry):
        j = 2 * i
        scores(j + 1, sb_ref)
        update(j, sa_ref, None)
        scores(j + 2, sa_ref)
        update(j + 1, sb_ref, None)
        return carry

    scores(0, sa_ref)
    lax.fori_loop(0, qi // 2, two_blocks, 0)

    @pl.when(qi % 2 == 1)
    def _():
        scores(qi, sb_ref)
        update(qi - 1, sa_ref, None)
        update(qi, sb_ref, diagonal())

    @pl.when(qi % 2 == 0)
    def _():
        update(qi, sa_ref, diagonal())

    l0, l1 = l_ref[0], l_ref[1]
    lam = (jnp.exp(jnp.sum(lq1_ref[...] * lk1_ref[...], axis=-1, keepdims=True))
           - jnp.exp(jnp.sum(lq2_ref[...] * lk2_ref[...], axis=-1, keepdims=True))
           + lambda_init)
    o = (acc_ref[0] / l0 - lam * (acc_ref[1] / l1)).T
    o = o * lax.rsqrt(jnp.mean(o * o, axis=-1, keepdims=True) + SUBLN_EPS)
    o = o * sw_ref[...] * (1.0 - lambda_init)
    o_ref[0] = (o * _silu(g_ref[0].astype(F32))).astype(o_ref.dtype)


def _diff_attention(qt, k, vt, gate, lam_q1, lam_k1, lam_q2, lam_k2, subln_w, lambda_init):
    batch, seq, _ = k.shape
    T = ATT_TILE
    small = lambda n: pl.BlockSpec((1, n), lambda b, h, i: (0, 0))
    tile = pl.BlockSpec((1, T, DA_VDIM), lambda b, h, i: (b, i, h))
    return pl.pallas_call(
        functools.partial(_attn_kernel, lambda_init=lambda_init),
        grid=(batch, DA_HEADS, seq // T),
        in_specs=[small(DA_QKDIM)] * 4 + [
            small(DA_VDIM),
            pl.BlockSpec((1, 1, 1, DA_VDIM, T), lambda b, h, i: (b, h, i, 0, 0)),
            pl.BlockSpec((1, seq, DA_VDIM), lambda b, h, i: (b, 0, h)),
            pl.BlockSpec((1, 1, seq // T, DA_VDIM, T), lambda b, h, i: (b, h, 0, 0, 0)),
            tile,
        ],
        out_specs=tile,
        out_shape=jax.ShapeDtypeStruct((batch, seq, DA_WIDTH), BF16),
        scratch_shapes=[pltpu.VMEM((2, DA_VDIM, T), F32),
                        pltpu.VMEM((2, 1, T), F32),
                        pltpu.VMEM((2, 1, T), F32),
                        pltpu.VMEM((2, T, T), F32),
                        pltpu.VMEM((2, T, T), F32)],
        compiler_params=pltpu.CompilerParams(
            dimension_semantics=("parallel", "parallel", "arbitrary"),
            vmem_limit_bytes=VMEM_LIMIT),
        name="diff_attn",
    )(*(t.reshape(1, -1) for t in (lam_q1, lam_k1, lam_q2, lam_k2, subln_w)),
      qt, k, vt, gate)


def _out_proj_kernel(yr_ref, yd_ref, wr_ref, wd_ref, g_ref, gate_ref, x_ref, o_ref):
    y = _dot(yr_ref[0], wr_ref[...]) + _dot(yd_ref[0], wd_ref[...])
    y = y * lax.rsqrt(jnp.mean(y * y, axis=-1, keepdims=True) + RMS_EPS) * g_ref[...]
    o_ref[0] = x_ref[0] + gate_ref[0] * y


def _out_projection(y_rw, y_da, w_out_bf16, g_post, gate, x):
    batch, seq, _ = x.shape
    tm = PROJ_ROWS
    half = lambda j: pl.BlockSpec((RW_WIDTH, D_MODEL), lambda b, i: (j, 0))
    row_spec = lambda width: pl.BlockSpec((1, tm, width), lambda b, i: (b, i, 0))
    return pl.pallas_call(
        _out_proj_kernel,
        grid=(batch, seq // tm),
        in_specs=[
            row_spec(RW_WIDTH), row_spec(DA_WIDTH), half(0), half(1),
            pl.BlockSpec((1, D_MODEL), lambda b, i: (0, 0)),
            pl.BlockSpec((1, 1, D_MODEL), lambda b, i: (b, 0, 0)),
            row_spec(D_MODEL),
        ],
        out_specs=row_spec(D_MODEL),
        out_shape=jax.ShapeDtypeStruct(x.shape, x.dtype),
        compiler_params=pltpu.CompilerParams(
            dimension_semantics=("parallel", "parallel"),
            vmem_limit_bytes=VMEM_LIMIT),
        name="out_proj",
    )(y_rw, y_da, w_out_bf16, w_out_bf16, g_post, gate, x)


def kernel(x, c, w_ada, b_ada, g_pre, g_post, w_in, w_out, rw_mu, rw_w0, rw_w_w2,
           rw_a0, rw_w_a2, rw_k_k, rw_k_a, rw_r_k, rw_lnx_w, rw_lnx_b,
           da_lam_q1, da_lam_k1, da_lam_q2, da_lam_k2, da_subln_w):
    batch = x.shape[0]
    depth = w_in.shape[0]
    for l in range(depth):
        lambda_init = 0.8 - 0.6 * math.exp(-0.3 * l)
        mod = _modulation(c, w_ada[l], b_ada[l]).reshape(batch, 3, 1, D_MODEL)
        shift, scale, gate = mod[:, 0], mod[:, 1], mod[:, 2]
        feat, rw_gate, dk, dg, dqt, dvt = _in_projection(
            x, shift, scale, g_pre[l].reshape(1, D_MODEL), w_in[l])
        y_rw = _rwkv_branch(feat, rw_gate, rw_mu[l], rw_w0[l], rw_w_w2[l], rw_a0[l],
                            rw_w_a2[l], rw_k_k[l], rw_k_a[l], rw_r_k[l],
                            rw_lnx_w[l], rw_lnx_b[l])
        y_da = _diff_attention(dqt, dk, dvt, dg, da_lam_q1[l], da_lam_k1[l],
                               da_lam_q2[l], da_lam_k2[l], da_subln_w[l], lambda_init)
        x = _out_projection(y_rw, y_da, w_out[l].astype(BF16),
                            g_post[l].reshape(1, D_MODEL), gate, x)
    return x
```

```python
import functools
import math

import jax
import jax.numpy as jnp
from jax import lax
from jax.experimental import pallas as pl
from jax.experimental.pallas import tpu as pltpu

D_MODEL = 1024
SEQ = 4096
CHUNK = 64
RW_WIDTH = 512
RW_HEAD = 64
RW_HEADS = RW_WIDTH // RW_HEAD
RW_RANK = 64
RW_SHIFTED = 3 * RW_WIDTH + 2 * RW_RANK
RW_COLS = RW_SHIFTED + RW_WIDTH
DA_WIDTH = 512
DA_HEADS = 4
DA_VDIM = DA_WIDTH // DA_HEADS
DA_QKDIM = DA_VDIM // 2
D_IN = RW_COLS + 4 * DA_WIDTH
RMS_EPS = 1e-6
GN_EPS = 64e-5
SUBLN_EPS = 1e-5

IN_ROWS = 512
OUT_ROWS = 512
RW_CHUNK = 64
RW_GROUP = 4 * RW_HEAD
ATT_TILE = 512
VMEM_LIMIT = 48 * 1024 * 1024

F32 = jnp.float32
BF16 = jnp.bfloat16
HIGHEST = lax.Precision.HIGHEST


def _silu(t):
    return t * jax.nn.sigmoid(t)


def _dot(a, b, precision=None):
    return jnp.dot(a, b, preferred_element_type=F32, precision=precision)


def _dot_nt(a, b):
    return lax.dot_general(a, b, (((1,), (1,)), ((), ())), preferred_element_type=F32)


def _dot_tn(a, b):
    return lax.dot_general(a, b, (((0,), (0,)), ((), ())), preferred_element_type=F32)


def _split_dot(x, w_exact, terms, *, x_is_lhs):
    acc = None
    for _ in range(terms):
        piece = x.astype(BF16)
        part = _dot(piece, w_exact) if x_is_lhs else _dot(w_exact, piece)
        acc = part if acc is None else acc + part
        x = x - piece.astype(F32)
    return acc


def _mod_kernel(c_ref, w_ref, b_ref, o_ref):
    o_ref[...] = _dot(_silu(c_ref[...]), w_ref[...], HIGHEST) + b_ref[...]


def _modulation(c, w_ada, b_ada):
    batch = c.shape[0]
    return pl.pallas_call(
        _mod_kernel,
        grid=(3,),
        in_specs=[
            pl.BlockSpec((batch, D_MODEL), lambda j: (0, 0)),
            pl.BlockSpec((D_MODEL, D_MODEL), lambda j: (0, j)),
            pl.BlockSpec((1, D_MODEL), lambda j: (0, j)),
        ],
        out_specs=pl.BlockSpec((batch, D_MODEL), lambda j: (0, j)),
        out_shape=jax.ShapeDtypeStruct((batch, 3 * D_MODEL), F32),
        compiler_params=pltpu.CompilerParams(vmem_limit_bytes=VMEM_LIMIT),
        name="adaln_mod",
    )(c, w_ada, b_ada.reshape(1, 3 * D_MODEL))


_Q_COLS = (RW_COLS, RW_COLS + DA_WIDTH)
_V_COLS = (RW_COLS + 2 * DA_WIDTH, RW_COLS + 3 * DA_WIDTH)
_ROW_SEGMENTS = (
    (0, RW_SHIFTED),
    (RW_SHIFTED, RW_COLS),
    (RW_COLS + DA_WIDTH, RW_COLS + 2 * DA_WIDTH),
    (RW_COLS + 3 * DA_WIDTH, D_IN),
)
_ROW_WIDTHS = tuple(hi - lo for lo, hi in _ROW_SEGMENTS)
_ROW_DTYPES = (F32, BF16, BF16, BF16)
Q_SCALE = math.log2(math.e) / math.sqrt(DA_QKDIM)


def _in_proj_kernel(x_ref, shift_ref, scale_ref, g_ref, w_ref, wt_ref, *out_refs):
    x = x_ref[0]
    y = x * lax.rsqrt(jnp.mean(x * x, axis=-1, keepdims=True) + RMS_EPS) * g_ref[...]
    h = (y * (1.0 + scale_ref[0]) + shift_ref[0]).astype(BF16)
    lo = 0
    for width, o_ref in zip(_ROW_WIDTHS, out_refs):
        o_ref[0] = _dot(h, w_ref[:, lo:lo + width]).astype(o_ref.dtype)
        lo += width
    qt_ref, vt_ref = out_refs[len(_ROW_WIDTHS):]
    tm = h.shape[0]
    qt = _dot_nt(wt_ref[0:DA_WIDTH, :], h) * Q_SCALE
    vt = _dot_nt(wt_ref[DA_WIDTH:, :], h)
    qt_ref[0, :, 0] = qt.astype(BF16).reshape(DA_HEADS, DA_VDIM, tm)
    vt_ref[0, :, 0] = vt.astype(BF16).reshape(DA_HEADS, DA_VDIM, tm)


def _in_projection(x, shift, scale, g_pre, w_in):
    batch, seq, _ = x.shape
    tm = IN_ROWS
    w_rows =jnp.concatenate([w_in[:, lo:hi] for lo, hi in _ROW_SEGMENTS], axis=1).astype(BF16)
    w_t = jnp.concatenate([w_in[:, lo:hi] for lo, hi in (_Q_COLS, _V_COLS)], axis=1).T.astype(BF16)
    row_spec = lambda width: pl.BlockSpec((1, tm, width), lambda b, i: (b, i, 0))
    vec_spec = pl.BlockSpec((1, 1, D_MODEL), lambda b, i: (b, 0, 0))
    per_tile = ATT_TILE // tm
    t_spec = pl.BlockSpec((1, DA_HEADS, 1, DA_VDIM, tm),
                          lambda b, i: (b, 0, i // per_tile, 0, i % per_tile))
    t_shape = jax.ShapeDtypeStruct((batch, DA_HEADS, seq // ATT_TILE, DA_VDIM, ATT_TILE), BF16)
    return pl.pallas_call(
        _in_proj_kernel,
        grid=(batch, seq // tm),
        in_specs=[
            row_spec(D_MODEL), vec_spec, vec_spec,
            pl.BlockSpec((1, D_MODEL), lambda b, i: (0, 0)),
            pl.BlockSpec(w_rows.shape, lambda b, i: (0, 0), pipeline_mode=pl.Buffered(1)),
            pl.BlockSpec(w_t.shape, lambda b, i: (0, 0), pipeline_mode=pl.Buffered(1)),
        ],
        out_specs=[row_spec(w) for w in _ROW_WIDTHS] + [t_spec, t_spec],
        out_shape=[jax.ShapeDtypeStruct((batch, seq, w), dt)
                   for w, dt in zip(_ROW_WIDTHS, _ROW_DTYPES)] + [t_shape, t_shape],
        compiler_params=pltpu.CompilerParams(
            dimension_semantics=("parallel", "parallel"),
            vmem_limit_bytes=VMEM_LIMIT),
        name="in_proj",
    )(x, shift, scale, g_pre, w_rows, w_t)


def _rwkv_kernel(feat_ref, gate_ref, mu_ref, w0_ref, ww2_ref, a0_ref, wa2_ref,
                 kk_ref, ka_ref, rk_ref, lnw_ref, lnb_ref, y_ref, ubuf, state):
    C, W, N, L = RW_CHUNK, RW_WIDTH, RW_HEAD, RW_GROUP
    heads = L // N
    batch = feat_ref.shape[0]
    ci = pl.program_id(0)

    @pl.when(ci == 0)
    def _():
        ubuf[:, 0:8, :] = jnp.zeros((batch, 8, RW_SHIFTED), F32)
        state[...] = jnp.zeros(state.shape, F32)

    iota = lambda shape, axis: lax.broadcasted_iota(jnp.int32, shape, axis)
    row2, col2 = iota((2 * C, L), 0), iota((2 * C, L), 1) & (N - 1)
    tri = ((row2 & (C - 1)) > col2) | ((row2 >= C) & ((row2 & (C - 1)) == col2))
    r1, c1 = iota((C, L), 0), iota((C, L), 1) & (N - 1)
    eye = (r1 == c1).astype(F32)
    levels = []
    for bit in range(C.bit_length() - 1):
        levels.append(((r1 >> (bit + 1)) == (c1 >> (bit + 1)))
                      & (((r1 >> bit) & 1) == 1) & (((c1 >> bit) & 1) == 0))
    head_bits = N.bit_length() - 1
    same_head = (iota((L, L), 0) >> head_bits) == (iota((L, L), 1) >> head_bits)
    seg = same_head.astype(BF16)
    cum = (iota((C, C), 0) >= iota((C, C), 1)).astype(BF16)

    def block_diag(t):
        return jnp.where(same_head, jnp.concatenate([t] * heads, axis=0), 0.0).astype(BF16)

    def head_sums(t):
        return jnp.concatenate(
            [_dot(t[:, g * L:(g + 1) * L].astype(BF16), seg) for g in range(W // L)], axis=-1)

    prep = []
    groups = []
    for b in range(batch):
        u = feat_ref[b]
        ubuf[b, 8:8 + C, :] = u
        prev = ubuf[b, 7:7 + C, :]
        ubuf[b, 0:8, :] = u[C - 8:C, :]
        feat = u + (prev - u) * mu_ref[...]
        r, k, v = feat[:, 0:W], feat[:, W:2 * W], feat[:, 2 * W:3 * W]
        w_lo = feat[:, 3 * W:3 * W + RW_RANK]
        a_lo = feat[:, 3 * W + RW_RANK:]

        z = w0_ref[...] + _dot(jnp.tanh(w_lo).astype(BF16), ww2_ref[...])
        logw = -math.exp(-0.5) * jax.nn.sigmoid(z)
        a = jax.nn.sigmoid(a0_ref[...] + _dot(a_lo.astype(BF16), wa2_ref[...]))
        kk_raw = k * kk_ref[...]
        kk = kk_raw * lax.rsqrt(jnp.maximum(head_sums(kk_raw * kk_raw), 1e-24))
        bb = a * kk
        k2 = k * (1.0 + (a - 1.0) * ka_ref[...])

        cs = _split_dot(logw, cum, 2, x_is_lhs=False)
        cs_last = cs[C - 1:C, :]
        g_all = jnp.exp(cs_last)
        g_inv = jnp.exp(-cs)
        g_end = jnp.exp(cs_last - cs)
        kk_t = (kk * jnp.exp(cs - logw)).astype(BF16)
        r_t = (r * jnp.exp(cs)).astype(BF16)
        k_inv, b_inv = k2 * g_inv, bb * g_inv
        k_end, nb_end = (k2 * g_end).astype(BF16), (-(bb * g_end)).astype(BF16)
        prep.append((r, k2, v))
        for g in range(W // L):
            gl = slice(g * L, (g + 1) * L)
            groups.append(dict(
                b=b, g=g, g_all=g_all[:, gl], v=v[:, gl],
                lhs=jnp.concatenate([kk_t[:, gl], r_t[:, gl]], axis=0),
                rhs=jnp.concatenate([block_diag(b_inv[:, gl]), block_diag(k_inv[:, gl])], axis=0),
                end=jnp.concatenate([k_end[:, gl], nb_end[:, gl]], axis=0)))

    for gr in groups:
        m = _dot_nt(gr["lhs"], gr["rhs"])
        gr["a_b"] = jnp.where(tri, m[:, :L], 0.0)
        gr["a_k"] = jnp.where(tri, m[:, L:], 0.0).astype(BF16)
        gr["t"] = eye - jnp.where(levels[0], gr["a_b"][:C], 0.0)
    for gr in groups:
        gr["st"] = state[gr["b"], gr["g"]]
        gr["x"] = (_dot(gr["a_k"], block_diag(gr["v"]))
                   + _dot_nt(gr["lhs"], block_diag(gr["st"])))
    for level in levels[1:]:
        for gr in groups:
            off = jnp.where(level, gr["a_b"][:C], 0.0)
            gr["inner"] = _dot(gr["t"].astype(BF16), block_diag(off)).astype(BF16)
        for gr in groups:
            gr["t"] = gr["t"] - _dot(gr["inner"], block_diag(gr["t"]))
    for gr in groups:
        gr["uu"] = _dot(gr["t"].astype(BF16), block_diag(gr["x"][:C]))
    for gr in groups:
        gr["o"] = gr["x"][C:] - _dot(gr["a_b"][C:].astype(BF16), block_diag(gr["uu"]))
    for gr in groups:
        vu = jnp.concatenate([gr["v"], gr["uu"]], axis=0).astype(BF16)
        delta = jnp.where(same_head, _dot_tn(vu, gr["end"]), 0.0)
        state[gr["b"], gr["g"]] = gr["st"] * gr["g_all"] + sum(
            delta[h * N:(h + 1) * N] for h in range(heads))

    for b in range(batch):
        r, k2, v = prep[b]
        o = jnp.concatenate([gr["o"] for gr in groups if gr["b"] == b], axis=-1)
        d = o - head_sums(o) * (1.0 / N)
        var = head_sums(d * d) * (1.0 / N)
        o = d * lax.rsqrt(var + GN_EPS) * lnw_ref[...] + lnb_ref[...]
        bonus = head_sums(r * k2 * rk_ref[...]) * v
        y_ref[b] = ((o + bonus) * _silu(gate_ref[b].astype(F32))).astype(y_ref.dtype)


def _rwkv_branch(feat, gate, mu, w0, w_w2, a0, w_a2, k_k, k_a, r_k, lnx_w, lnx_b):
    batch, seq, _ = feat.shape
    C = RW_CHUNK
    full = lambda shape: pl.BlockSpec(shape, lambda i: (0,) * len(shape))
    vec = lambda t: t.reshape(1, -1)
    return pl.pallas_call(
        _rwkv_kernel,
        grid=(seq // C,),
        in_specs=[
            pl.BlockSpec((batch, C, RW_SHIFTED), lambda i: (0, i, 0)),
            pl.BlockSpec((batch, C, RW_WIDTH), lambda i: (0, i, 0)),
            full((1, RW_SHIFTED)), full((1, RW_WIDTH)), full((RW_RANK, RW_WIDTH)),
            full((1, RW_WIDTH)), full((RW_RANK, RW_WIDTH)),
            full((1, RW_WIDTH)), full((1, RW_WIDTH)), full((1, RW_WIDTH)),
            full((1, RW_WIDTH)), full((1, RW_WIDTH)),
        ],
        out_specs=pl.BlockSpec((batch, C, RW_WIDTH), lambda i: (0, i, 0)),
        out_shape=jax.ShapeDtypeStruct((batch, seq, RW_WIDTH), BF16),
        scratch_shapes=[
            pltpu.VMEM((batch, C + 8, RW_SHIFTED), F32),
            pltpu.VMEM((batch, RW_WIDTH // RW_GROUP, RW_HEAD, RW_GROUP), F32),
        ],
        compiler_params=pltpu.CompilerParams(
            dimension_semantics=("arbitrary",),
            vmem_limit_bytes=VMEM_LIMIT),
        name="rwkv7_chunked",
    )(feat, gate, vec(mu), vec(w0), w_w2.astype(BF16), vec(a0), w_a2.astype(BF16),
      vec(k_k), vec(k_a), vec(r_k), vec(lnx_w), vec(lnx_b))


def _attn_kernel(lq1_ref, lk1_ref, lq2_ref, lk2_ref, sw_ref, qt_ref, k_ref, vt_ref,
                 g_ref, o_ref, acc_ref, m_ref, l_ref, sa_ref, sb_ref, *, lambda_init):
    T = ATT_TILE
    qi = pl.program_id(2)
    qt = qt_ref[0, 0, 0]
    comp = lax.broadcasted_iota(jnp.int32, qt.shape, 0) < DA_QKDIM
    zero = jnp.zeros_like(qt)
    qt_c = (jnp.where(comp, qt, zero), jnp.where(comp, zero, qt))
    acc_ref[...] = jnp.zeros(acc_ref.shape, F32)
    m_ref[...] = jnp.full(m_ref.shape, -jnp.inf, F32)
    l_ref[...] = jnp.zeros(l_ref.shape, F32)

    def scores(j, dst):
        kb = k_ref[0, pl.ds(pl.multiple_of(j * T, T), T), :]
        for c in range(2):
            dst[c] = _dot(kb, qt_c[c])

    def update(j, src, mask):
        vtb = vt_ref[0, 0, j]
        for c in range(2):
            s = src[c]
            if mask is not None:
                s = jnp.where(mask, s, -jnp.inf)
            m = m_ref[c]
            m_new = jnp.maximum(m, jnp.max(s, axis=0, keepdims=True))
            alpha = jnp.exp2(m - m_new)
            p = jnp.exp2(s - m_new)
            m_ref[c] = m_new
            l_ref[c] = alpha * l_ref[c] + jnp.sum(p, axis=0, keepdims=True)
            acc_ref[c] = alpha * acc_ref[c] + _dot(vtb, p.astype(BF16))

    def diagonal():
        key = lax.broadcasted_iota(jnp.int32, (T, T), 0)
        query = lax.broadcasted_iota(jnp.int32, (T, T), 1)
        return (query // CHUNK) >= (key // CHUNK)

    def two_blocks(i, carry):
        j = 2 * i
        scores(j + 1, sb_ref)
        update(j, sa_ref, None)
        scores(j + 2, sa_ref)
        update(j + 1, sb_ref, None)
        return carry

    scores(0, sa_ref)
    lax.fori_loop(0, qi // 2, two_blocks, 0)

    @pl.when(qi % 2 == 1)
    def _():
        scores(qi, sb_ref)
        update(qi - 1, sa_ref, None)
        update(qi, sb_ref, diagonal())

    @pl.when(qi % 2 == 0)
    def _():
        update(qi, sa_ref, diagonal())

    l0, l1 = l_ref[0], l_ref[1]
    lam = (jnp.exp(jnp.sum(lq1_ref[...] * lk1_ref[...], axis=-1, keepdims=True))
           - jnp.exp(jnp.sum(lq2_ref[...] * lk2_ref[...], axis=-1, keepdims=True))
           + lambda_init)
    o = (acc_ref[0] / l0 - lam * (acc_ref[1] / l1)).T
    o = o * lax.rsqrt(jnp.mean(o * o, axis=-1, keepdims=True) + SUBLN_EPS)
    o = o * sw_ref[...] * (1.0 - lambda_init)
    o_ref[0] = (o * _silu(g_ref[0].astype(F32))).astype(o_ref.dtype)


def _diff_attention(qt, k, vt, gate, lam_q1, lam_k1, lam_q2, lam_k2, subln_w, lambda_init):
    batch, seq, _ = k.shape
    T = ATT_TILE
    small = lambda n: pl.BlockSpec((1, n), lambda b, h, i: (0, 0))
    tile = pl.BlockSpec((1, T, DA_VDIM), lambda b, h, i: (b, i, h))
    return pl.pallas_call(
        functools.partial(_attn_kernel, lambda_init=lambda_init),
        grid=(batch, DA_HEADS, seq // T),
        in_specs=[small(DA_QKDIM)] * 4 + [
            small(DA_VDIM),
            pl.BlockSpec((1, 1, 1, DA_VDIM, T), lambda b, h, i: (b, h, i, 0, 0)),
            pl.BlockSpec((1, seq, DA_VDIM), lambda b, h, i: (b, 0, h)),
            pl.BlockSpec((1, 1, seq // T, DA_VDIM, T), lambda b, h, i: (b, h, 0, 0, 0)),
            tile,
        ],
        out_specs=tile,
        out_shape=jax.ShapeDtypeStruct((batch, seq, DA_WIDTH), BF16),
        scratch_shapes=[pltpu.VMEM((2, DA_VDIM, T), F32),
                        pltpu.VMEM((2, 1, T), F32),
                        pltpu.VMEM((2, 1, T), F32),
                        pltpu.VMEM((2, T, T), F32),
                        pltpu.VMEM((2, T, T), F32)],
        compiler_params=pltpu.CompilerParams(
            dimension_semantics=("parallel", "parallel", "arbitrary"),
            vmem_limit_bytes=VMEM_LIMIT),
        name="diff_attn",
    )(*(t.reshape(1, -1) for t in (lam_q1, lam_k1, lam_q2, lam_k2, subln_w)),
      qt, k, vt, gate)


def _out_proj_kernel(yr_ref, yd_ref, wr_ref, wd_ref, g_ref, gate_ref, x_ref, o_ref):
    y = _dot(yr_ref[0], wr_ref[...]) + _dot(yd_ref[0], wd_ref[...])
    y = y * lax.rsqrt(jnp.mean(y * y, axis=-1, keepdims=True) + RMS_EPS) * g_ref[...]
    o_ref[0] = x_ref[0] + gate_ref[0] * y


def _out_projection(y_rw, y_da, w_out_bf16, g_post, gate, x):
    batch, seq, _ = x.shape
    tm = OUT_ROWS
    half = lambda j: pl.BlockSpec((RW_WIDTH, D_MODEL), lambda b, i: (j, 0),
                                  pipeline_mode=pl.Buffered(1))
    row_spec = lambda width: pl.BlockSpec((1, tm, width), lambda b, i: (b, i, 0))
    return pl.pallas_call(
        _out_proj_kernel,
        grid=(batch, seq // tm),
        in_specs=[
            row_spec(RW_WIDTH), row_spec(DA_WIDTH), half(0), half(1),
            pl.BlockSpec((1, D_MODEL), lambda b, i: (0, 0)),
            pl.BlockSpec((1, 1, D_MODEL), lambda b, i: (b, 0, 0)),
            row_spec(D_MODEL),
        ],
        out_specs=row_spec(D_MODEL),
        out_shape=jax.ShapeDtypeStruct(x.shape, x.dtype),
        compiler_params=pltpu.CompilerParams(
            dimension_semantics=("parallel", "parallel"),
            vmem_limit_bytes=VMEM_LIMIT),
        name="out_proj",
    )(y_rw, y_da, w_out_bf16, w_out_bf16, g_post, gate, x)


def kernel(x, c, w_ada, b_ada, g_pre, g_post, w_in, w_out, rw_mu, rw_w0, rw_w_w2,
           rw_a0, rw_w_a2, rw_k_k, rw_k_a, rw_r_k, rw_lnx_w, rw_lnx_b,
           da_lam_q1, da_lam_k1, da_lam_q2, da_lam_k2, da_subln_w):
    batch = x.shape[0]
    depth = w_in.shape[0]
    for l in range(depth):
        lambda_init = 0.8 - 0.6 * math.exp(-0.3 * l)
        mod = _modulation(c, w_ada[l], b_ada[l]).reshape(batch, 3, 1, D_MODEL)
        shift, scale, gate = mod[:, 0], mod[:, 1], mod[:, 2]
        feat, rw_gate, dk, dg, dqt, dvt = _in_projection(
            x, shift, scale, g_pre[l].reshape(1, D_MODEL), w_in[l])
        y_rw = _rwkv_branch(feat, rw_gate, rw_mu[l], rw_w0[l], rw_w_w2[l], rw_a0[l],
                            rw_w_a2[l], rw_k_k[l], rw_k_a[l], rw_r_k[l],
                            rw_lnx_w[l], rw_lnx_b[l])
        y_da = _diff_attention(dqt, dk, dvt, dg, da_lam_q1[l], da_lam_k1[l],
                               da_lam_q2[l], da_lam_k2[l], da_subln_w[l], lambda_init)
        x = _out_projection(y_rw, y_da, w_out[l].astype(BF16),
                            g_post[l].reshape(1, D_MODEL), gate, x)
    return x
```

```python
import functools
import math

import jax
import jax.numpy as jnp
from jax import lax
from jax.experimental import pallas as pl
from jax.experimental.pallas import tpu as pltpu

D_MODEL = 1024
SEQ = 4096
CHUNK = 64
RW_WIDTH = 512
RW_HEAD = 64
RW_HEADS = RW_WIDTH // RW_HEAD
RW_RANK = 64
RW_SHIFTED = 3 * RW_WIDTH + 2 * RW_RANK
RW_COLS = RW_SHIFTED + RW_WIDTH
DA_WIDTH = 512
DA_HEADS = 4
DA_VDIM = DA_WIDTH // DA_HEADS
DA_QKDIM = DA_VDIM // 2
D_IN = RW_COLS + 4 * DA_WIDTH
RMS_EPS = 1e-6
GN_EPS = 64e-5
SUBLN_EPS = 1e-5

IN_ROWS = 512
OUT_ROWS = 512
RW_CHUNK = 64
RW_GROUP = 4 * RW_HEAD
ATT_TILE = 512
VMEM_LIMIT = 48 * 1024 * 1024

F32 = jnp.float32
BF16 = jnp.bfloat16
HIGHEST = lax.Precision.HIGHEST


def _silu(t):
    return t * jax.nn.sigmoid(t)


def _dot(a, b, precision=None):
    return jnp.dot(a, b, preferred_element_type=F32, precision=precision)


def _dot_nt(a, b):
    return lax.dot_general(a, b, (((1,), (1,)), ((), ())), preferred_element_type=F32)


def _dot_tn(a, b):
    return lax.dot_general(a, b, (((0,), (0,)), ((), ())), preferred_element_type=F32)


def _split_dot(x, w_exact, terms, *, x_is_lhs):
    acc = None
    for _ in range(terms):
        piece = x.astype(BF16)
        part = _dot(piece, w_exact) if x_is_lhs else _dot(w_exact, piece)
        acc = part if acc is None else acc + part
        x = x - piece.astype(F32)
    return acc


def _mod_kernel(c_ref, w_ref, b_ref, o_ref):
    o_ref[...] = _dot(_silu(c_ref[...]), w_ref[...], HIGHEST) + b_ref[...]


def _modulation(c, w_ada, b_ada):
    batch = c.shape[0]
    return pl.pallas_call(
        _mod_kernel,
        grid=(3,),
        in_specs=[
            pl.BlockSpec((batch, D_MODEL), lambda j: (0, 0)),
            pl.BlockSpec((D_MODEL, D_MODEL), lambda j: (0, j)),
            pl.BlockSpec((1, D_MODEL), lambda j: (0, j)),
        ],
        out_specs=pl.BlockSpec((batch, D_MODEL), lambda j: (0, j)),
        out_shape=jax.ShapeDtypeStruct((batch, 3 * D_MODEL), F32),
        compiler_params=pltpu.CompilerParams(vmem_limit_bytes=VMEM_LIMIT),
        name="adaln_mod",
    )(c, w_ada, b_ada.reshape(1, 3 * D_MODEL))


_Q_COLS = (RW_COLS, RW_COLS + DA_WIDTH)
_V_COLS = (RW_COLS + 2 * DA_WIDTH, RW_COLS + 3 * DA_WIDTH)
_ROW_SEGMENTS = (
    (0, RW_SHIFTED),
    (RW_SHIFTED, RW_COLS),
    (RW_COLS + DA_WIDTH, RW_COLS + 2 * DA_WIDTH),
    (RW_COLS + 3 * DA_WIDTH, D_IN),
)
_ROW_WIDTHS = tuple(hi - lo for lo, hi in _ROW_SEGMENTS)
_ROW_DTYPES = (F32, BF16, BF16, BF16)
Q_SCALE = math.log2(math.e) / math.sqrt(DA_QKDIM)


def _in_proj_kernel(x_ref, shift_ref, scale_ref, g_ref, w_ref, wt_ref, *out_refs):
    x = x_ref[0]
    y = x * lax.rsqrt(jnp.mean(x * x, axis=-1, keepdims=True) + RMS_EPS) * g_ref[...]
    h = (y * (1.0 + scale_ref[0]) + shift_ref[0]).astype(BF16)
    lo = 0
    for width, o_ref in zip(_ROW_WIDTHS, out_refs):
        o_ref[0] = _dot(h, w_ref[:, lo:lo + width]).astype(o_ref.dtype)
        lo += width
    qt_ref, vt_ref = out_refs[len(_ROW_WIDTHS):]
    tm = h.shape[0]
    qt = _dot_nt(wt_ref[0:DA_WIDTH, :], h) * Q_SCALE
    vt = _dot_nt(wt_ref[DA_WIDTH:, :], h)
    qt_ref[0, :, 0] = qt.astype(BF16).reshape(DA_HEADS, DA_VDIM, tm)
    vt_ref[0, :, 0] = vt.astype(BF16).reshape(DA_HEADS, DA_VDIM, tm)


def _in_projection(x, shift, scale, g_pre, w_in):
    batch, seq, _ = x.shape
    tm = IN_ROWS
    w_rows =jnp.concatenate([w_in[:, lo:hi] for lo, hi in _ROW_SEGMENTS], axis=1).astype(BF16)
    w_t = jnp.concatenate([w_in[:, lo:hi] for lo, hi in (_Q_COLS, _V_COLS)], axis=1).T.astype(BF16)
    row_spec = lambda width: pl.BlockSpec((1, tm, width), lambda b, i: (b, i, 0))
    vec_spec = pl.BlockSpec((1, 1, D_MODEL), lambda b, i: (b, 0, 0))
    per_tile = ATT_TILE // tm
    t_spec = pl.BlockSpec((1, DA_HEADS, 1, DA_VDIM, tm),
                          lambda b, i: (b, 0, i // per_tile, 0, i % per_tile))
    t_shape = jax.ShapeDtypeStruct((batch, DA_HEADS, seq // ATT_TILE, DA_VDIM, ATT_TILE), BF16)
    return pl.pallas_call(
        _in_proj_kernel,
        grid=(batch, seq // tm),
        in_specs=[
            row_spec(D_MODEL), vec_spec, vec_spec,
            pl.BlockSpec((1, D_MODEL), lambda b, i: (0, 0)),
            pl.BlockSpec(w_rows.shape, lambda b, i: (0, 0), pipeline_mode=pl.Buffered(1)),
            pl.BlockSpec(w_t.shape, lambda b, i: (0, 0), pipeline_mode=pl.Buffered(1)),
        ],
        out_specs=[row_spec(w) for w in _ROW_WIDTHS] + [t_spec, t_spec],
        out_shape=[jax.ShapeDtypeStruct((batch, seq, w), dt)
                   for w, dt in zip(_ROW_WIDTHS, _ROW_DTYPES)] + [t_shape, t_shape],
        compiler_params=pltpu.CompilerParams(
            dimension_semantics=("parallel", "parallel"),
            vmem_limit_bytes=VMEM_LIMIT),
        name="in_proj",
    )(x, shift, scale, g_pre, w_rows, w_t)


def _rwkv_kernel(feat_ref, gate_ref, mu_ref, w0_ref, ww2_ref, a0_ref, wa2_ref,
                 kk_ref, ka_ref, rk_ref, lnw_ref, lnb_ref, y_ref, ubuf, state):
    C, W, N, L = RW_CHUNK, RW_WIDTH, RW_HEAD, RW_GROUP
    heads = L // N
    batch = feat_ref.shape[0]
    ci = pl.program_id(0)

    @pl.when(ci == 0)
    def _():
        ubuf[:, 0:8, :] = jnp.zeros((batch, 8, RW_SHIFTED), F32)
        state[...] = jnp.zeros(state.shape, F32)

    iota = lambda shape, axis: lax.broadcasted_iota(jnp.int32, shape, axis)
    row2, col2 = iota((2 * C, L), 0), iota((2 * C, L), 1) & (N - 1)
    tri = ((row2 & (C - 1)) > col2) | ((row2 >= C) & ((row2 & (C - 1)) == col2))
    r1, c1 = iota((C, L), 0), iota((C, L), 1) & (N - 1)
    eye = (r1 == c1).astype(F32)
    levels = []
    for bit in range(C.bit_length() - 1):
        levels.append(((r1 >> (bit + 1)) == (c1 >> (bit + 1)))
                      & (((r1 >> bit) & 1) == 1) & (((c1 >> bit) & 1) == 0))
    head_bits = N.bit_length() - 1
    same_head = (iota((L, L), 0) >> head_bits) == (iota((L, L), 1) >> head_bits)
    seg = same_head.astype(BF16)
    cum = (iota((C, C), 0) >= iota((C, C), 1)).astype(BF16)

    def block_diag(t):
        return jnp.where(same_head, jnp.concatenate([t] * heads, axis=0), 0.0).astype(BF16)

    def head_sums(t):
        return jnp.concatenate(
            [_dot(t[:, g * L:(g + 1) * L].astype(BF16), seg) for g in range(W // L)], axis=-1)

    prep = []
    groups = []
    for b in range(batch):
        u = feat_ref[b]
        ubuf[b, 8:8 + C, :] = u
        prev = ubuf[b, 7:7 + C, :]
        ubuf[b, 0:8, :] = u[C - 8:C, :]
        feat = u + (prev - u) * mu_ref[...]
        r, k, v = feat[:, 0:W], feat[:, W:2 * W], feat[:, 2 * W:3 * W]
        w_lo = feat[:, 3 * W:3 * W + RW_RANK]
        a_lo = feat[:, 3 * W + RW_RANK:]

        z = w0_ref[...] + _dot(jnp.tanh(w_lo).astype(BF16), ww2_ref[...])
        logw = -math.exp(-0.5) * jax.nn.sigmoid(z)
        a = jax.nn.sigmoid(a0_ref[...] + _dot(a_lo.astype(BF16), wa2_ref[...]))
        kk_raw = k * kk_ref[...]
        kk = kk_raw * lax.rsqrt(jnp.maximum(head_sums(kk_raw * kk_raw), 1e-24))
        bb = a * kk
        k2 = k * (1.0 + (a - 1.0) * ka_ref[...])

        cs = _split_dot(logw, cum, 2, x_is_lhs=False)
        cs_last = cs[C - 1:C, :]
        g_all = jnp.exp(cs_last)
        g_inv = jnp.exp(-cs)
        g_end = jnp.exp(cs_last - cs)
        kk_t = (kk * jnp.exp(cs - logw)).astype(BF16)
        r_t = (r * jnp.exp(cs)).astype(BF16)
        k_inv, b_inv = k2 * g_inv, bb * g_inv
        k_end, nb_end = (k2 * g_end).astype(BF16), (-(bb * g_end)).astype(BF16)
        prep.append((r, k2, v))
        for g in range(W // L):
            gl = slice(g * L, (g + 1) * L)
            groups.append(dict(
                b=b, g=g, g_all=g_all[:, gl], v=v[:, gl],
                lhs=jnp.concatenate([kk_t[:, gl], r_t[:, gl]], axis=0),
                rhs=jnp.concatenate([block_diag(b_inv[:, gl]), block_diag(k_inv[:, gl])], axis=0),
                end=jnp.concatenate([k_end[:, gl], nb_end[:, gl]], axis=0)))

    for gr in groups:
        m = _dot_nt(gr["lhs"], gr["rhs"])
        gr["a_b"] = jnp.where(tri, m[:, :L], 0.0)
        gr["a_k"] = jnp.where(tri, m[:, L:], 0.0).astype(BF16)
        gr["t"] = eye - jnp.where(levels[0], gr["a_b"][:C], 0.0)
    for gr in groups:
        gr["st"] = state[gr["b"], gr["g"]]
        gr["x"] = (_dot(gr["a_k"], block_diag(gr["v"]))
                   + _dot_nt(gr["lhs"], block_diag(gr["st"])))
    for level in levels[1:]:
        for gr in groups:
            off = jnp.where(level, gr["a_b"][:C], 0.0)
            gr["inner"] = _dot(gr["t"].astype(BF16), block_diag(off)).astype(BF16)
        for gr in groups:
            gr["t"] = gr["t"] - _dot(gr["inner"], block_diag(gr["t"]))
    for gr in groups:
        gr["uu"] = _dot(gr["t"].astype(BF16), block_diag(gr["x"][:C]))
    for gr in groups:
        gr["o"] = gr["x"][C:] - _dot(gr["a_b"][C:].astype(BF16), block_diag(gr["uu"]))
    for gr in groups:
        vu = jnp.concatenate([gr["v"], gr["uu"]], axis=0).astype(BF16)
        delta = jnp.where(same_head, _dot_tn(vu, gr["end"]), 0.0)
        state[gr["b"], gr["g"]] = gr["st"] * gr["g_all"] + sum(
            delta[h * N:(h + 1) * N] for h in range(heads))

    for b in range(batch):
        r, k2, v = prep[b]
        o = jnp.concatenate([gr["o"] for gr in groups if gr["b"] == b], axis=-1)
        d = o - head_sums(o) * (1.0 / N)
        var = head_sums(d * d) * (1.0 / N)
        o = d * lax.rsqrt(var + GN_EPS) * lnw_ref[...] + lnb_ref[...]
        bonus = head_sums(r * k2 * rk_ref[...]) * v
        y_ref[b] = ((o + bonus) * _silu(gate_ref[b].astype(F32))).astype(y_ref.dtype)


def _rwkv_branch(feat, gate, mu, w0, w_w2, a0, w_a2, k_k, k_a, r_k, lnx_w, lnx_b):
    batch, seq, _ = feat.shape
    C = RW_CHUNK
    full = lambda shape: pl.BlockSpec(shape, lambda i: (0,) * len(shape))
    vec = lambda t: t.reshape(1, -1)
    return pl.pallas_call(
        _rwkv_kernel,
        grid=(seq // C,),
        in_specs=[
            pl.BlockSpec((batch, C, RW_SHIFTED), lambda i: (0, i, 0)),
            pl.BlockSpec((batch, C, RW_WIDTH), lambda i: (0, i, 0)),
            full((1, RW_SHIFTED)), full((1, RW_WIDTH)), full((RW_RANK, RW_WIDTH)),
            full((1, RW_WIDTH)), full((RW_RANK, RW_WIDTH)),
            full((1, RW_WIDTH)), full((1, RW_WIDTH)), full((1, RW_WIDTH)),
            full((1, RW_WIDTH)), full((1, RW_WIDTH)),
        ],
        out_specs=pl.BlockSpec((batch, C, RW_WIDTH), lambda i: (0, i, 0)),
        out_shape=jax.ShapeDtypeStruct((batch, seq, RW_WIDTH), BF16),
        scratch_shapes=[
            pltpu.VMEM((batch, C + 8, RW_SHIFTED), F32),
            pltpu.VMEM((batch, RW_WIDTH // RW_GROUP, RW_HEAD, RW_GROUP), F32),
        ],
        compiler_params=pltpu.CompilerParams(
            dimension_semantics=("arbitrary",),
            vmem_limit_bytes=VMEM_LIMIT),
        name="rwkv7_chunked",
    )(feat, gate, vec(mu), vec(w0), w_w2.astype(BF16), vec(a0), w_a2.astype(BF16),
      vec(k_k), vec(k_a), vec(r_k), vec(lnx_w), vec(lnx_b))


def _attn_kernel(lq1_ref, lk1_ref, lq2_ref, lk2_ref, sw_ref, qt_ref, k_ref, vt_ref,
                 g_ref, o_ref, s_buf, p_buf, acc_buf, *, lambda_init):
    T, HALF = ATT_TILE, ATT_TILE // 2
    tiles = k_ref.shape[1] // T

    comp = lax.broadcasted_iota(jnp.int32, (DA_VDIM, T), 0) < DA_QKDIM
    key = lax.broadcasted_iota(jnp.int32, (HALF, HALF), 0)
    query = lax.broadcasted_iota(jnp.int32, (HALF, HALF), 1)
    visible = (query // CHUNK) >= (key // CHUNK)
    lam = (jnp.exp(jnp.sum(lq1_ref[...] * lk1_ref[...], axis=-1, keepdims=True))
           - jnp.exp(jnp.sum(lq2_ref[...] * lk2_ref[...], axis=-1, keepdims=True))
           + lambda_init)

    for step in range(tiles // 2):
        pl.when(pl.program_id(2) == step)(functools.partial(
            _attn_tasks, [(i, j) for i in (tiles - 1 - step, step) for j in range(i + 1)],
            comp, visible, lam, sw_ref, qt_ref, k_ref, vt_ref, g_ref, o_ref,
            s_buf, p_buf, acc_buf, lambda_init))


def _attn_tasks(tasks, comp, visible, lam, sw_ref, qt_ref, k_ref, vt_ref, g_ref, o_ref,
                s_buf, p_buf, acc_buf, lambda_init):
    T, HALF = ATT_TILE, ATT_TILE // 2
    qt_parts = {}

    def qt_c(i):
        if i not in qt_parts:
            qt = qt_ref[0, 0, i]
            zero = jnp.zeros_like(qt)
            qt_parts[i] = (jnp.where(comp, qt, zero), jnp.where(comp, zero, qt))
        return qt_parts[i]

    def colmax(s):
        return jnp.max(s, axis=0, keepdims=True)

    def colsum(p):
        return jnp.sum(p, axis=0, keepdims=True)

    def scores(n):
        i, j = tasks[n]
        kb = k_ref[0, j * T:(j + 1) * T, :]
        blk_max = []
        for c in range(2):
            q = qt_c(i)[c]
            if j < i:
                s = _dot(kb, q)
                s_buf[n % 2, c] = s
                blk_max.append(colmax(s))
            else:
                s_r = _dot(kb, q[:, HALF:])
                s_b = jnp.where(visible, s_r[HALF:], -jnp.inf)
                s_l = jnp.where(visible, _dot(kb[:HALF], q[:, :HALF]), -jnp.inf)
                s_buf[n % 2, c, :HALF, HALF:] = s_r[:HALF]
                s_buf[n % 2, c, HALF:, HALF:] = s_b
                s_buf[n % 2, c, :HALF, :HALF] = s_l
                blk_max.append(jnp.concatenate(
                    [colmax(s_l), jnp.maximum(colmax(s_r[:HALF]), colmax(s_b))], axis=1))
        return blk_max

    def softmax(n, blk_maxes, stats):
        i, j = tasks[n]
        new, alphas = [], []
        for c in range(2):
            blk_max = blk_maxes[c]
            if j < i:
                s = s_buf[n % 2, c]
            else:
                s_l = s_buf[n % 2, c, :HALF, :HALF]
                s_t = s_buf[n % 2, c, :HALF, HALF:]
                s_b = s_buf[n % 2, c, HALF:, HALF:]
            if j == 0:
                m_new, alpha = blk_max, None
            else:
                m, l = stats[c]
                m_new = jnp.maximum(m, blk_max)
                alpha = jnp.exp2(m - m_new)
            if j < i:
                p = jnp.exp2(s - m_new)
                blk_sum = colsum(p)
                p_buf[n % 2, c] = p.astype(BF16)
            else:
                p_l = jnp.exp2(s_l - m_new[:, :HALF])
                p_t = jnp.exp2(s_t - m_new[:, HALF:])
                p_b = jnp.exp2(s_b - m_new[:, HALF:])
                blk_sum = jnp.concatenate([colsum(p_l), colsum(p_t) + colsum(p_b)], axis=1)
                p_buf[n % 2, c, :HALF, :HALF] = p_l.astype(BF16)
                p_buf[n % 2, c, :HALF, HALF:] = p_t.astype(BF16)
                p_buf[n % 2, c, HALF:, HALF:] = p_b.astype(BF16)
            new.append((m_new, blk_sum if j == 0 else alpha * l + blk_sum))
            alphas.append(alpha)
        return tuple(new), alphas

    def values(n, alphas):
        i, j = tasks[n]
        vtb = vt_ref[0, 0, j]
        for c in range(2):
            if j < i:
                pv = _dot(vtb, p_buf[n % 2, c])
            else:
                pv = jnp.concatenate(
                    [_dot(vtb[:, :HALF], p_buf[n % 2, c, :HALF, :HALF]),
                     _dot(vtb, p_buf[n % 2, c, :, HALF:])], axis=1)
            acc_buf[i % 2, c] = pv if j == 0 else alphas[c] * acc_buf[i % 2, c] + pv

    def finish(i, stats):
        (_, l0), (_, l1) = stats
        o = (acc_buf[i % 2, 0] * (1.0 / l0) - acc_buf[i % 2, 1] * (lam / l1)).T
        o = o * lax.rsqrt(jnp.mean(o * o, axis=-1, keepdims=True) + SUBLN_EPS)
        o = o * sw_ref[...] * (1.0 - lambda_init)
        rows = slice(i * T, (i + 1) * T)
        o_ref[0, rows, :] = (o * _silu(g_ref[0, rows, :].astype(F32))).astype(o_ref.dtype)

    blk_max = scores(0)
    stats, pending = None, None
    for n, (i, j) in enumerate(tasks):
        next_max = scores(n + 1) if n + 1 < len(tasks) else None
        if pending is not None:
            values(pending[0], pending[1])
            if pending[2] is not None:
                finish(tasks[pending[0]][0], pending[2])
        stats, alphas = softmax(n, blk_max, stats)
        pending = (n, alphas, stats if j == i else None)
        blk_max = next_max
    values(pending[0], pending[1])
    finish(tasks[-1][0], pending[2])


def _diff_attention(qt, k, vt, gate, lam_q1, lam_k1, lam_q2, lam_k2, subln_w, lambda_init):
    batch, seq, _ = k.shape
    T = ATT_TILE
    small = lambda n: pl.BlockSpec((1, n), lambda b, h, s: (0, 0))
    rows = pl.BlockSpec((1, seq, DA_VDIM), lambda b, h, s: (b, 0, h))
    cols = pl.BlockSpec((1, 1, seq // T, DA_VDIM, T), lambda b, h, s: (b, h, 0, 0, 0))
    return pl.pallas_call(
        functools.partial(_attn_kernel, lambda_init=lambda_init),
        grid=(batch, DA_HEADS, seq // T // 2),
        in_specs=[small(DA_QKDIM)] * 4 + [small(DA_VDIM), cols, rows, cols, rows],
        out_specs=rows,
        out_shape=jax.ShapeDtypeStruct((batch, seq, DA_WIDTH), BF16),
        scratch_shapes=[pltpu.VMEM((2, 2, T, T), F32),
                        pltpu.VMEM((2, 2, T, T), BF16),
                        pltpu.VMEM((2, 2, DA_VDIM, T), F32)],
        compiler_params=pltpu.CompilerParams(
            dimension_semantics=("parallel", "parallel", "arbitrary"),
            vmem_limit_bytes=VMEM_LIMIT),
        name="diff_attn",
    )(*(t.reshape(1, -1) for t in (lam_q1, lam_k1, lam_q2, lam_k2, subln_w)),
      qt, k, vt, gate)


def _out_proj_kernel(yr_ref, yd_ref, wr_ref, wd_ref, g_ref, gate_ref, x_ref, o_ref):
    y = _dot(yr_ref[0], wr_ref[...]) + _dot(yd_ref[0], wd_ref[...])
    y = y * lax.rsqrt(jnp.mean(y * y, axis=-1, keepdims=True) + RMS_EPS) * g_ref[...]
    o_ref[0] = x_ref[0] + gate_ref[0] * y


def _out_projection(y_rw, y_da, w_out_bf16, g_post, gate, x):
    batch, seq, _ = x.shape
    tm = OUT_ROWS
    half = lambda j: pl.BlockSpec((RW_WIDTH, D_MODEL), lambda b, i: (j, 0),
                                  pipeline_mode=pl.Buffered(1))
    row_spec = lambda width: pl.BlockSpec((1, tm, width), lambda b, i: (b, i, 0))
    return pl.pallas_call(
        _out_proj_kernel,
        grid=(batch, seq // tm),
        in_specs=[
            row_spec(RW_WIDTH), row_spec(DA_WIDTH), half(0), half(1),
            pl.BlockSpec((1, D_MODEL), lambda b, i: (0, 0)),
            pl.BlockSpec((1, 1, D_MODEL), lambda b, i: (b, 0, 0)),
            row_spec(D_MODEL),
        ],
        out_specs=row_spec(D_MODEL),
        out_shape=jax.ShapeDtypeStruct(x.shape, x.dtype),
        compiler_params=pltpu.CompilerParams(
            dimension_semantics=("parallel", "parallel"),
            vmem_limit_bytes=VMEM_LIMIT),
        name="out_proj",
    )(y_rw, y_da, w_out_bf16, w_out_bf16, g_post, gate, x)


def kernel(x, c, w_ada, b_ada, g_pre, g_post, w_in, w_out, rw_mu, rw_w0, rw_w_w2,
           rw_a0, rw_w_a2, rw_k_k, rw_k_a, rw_r_k, rw_lnx_w, rw_lnx_b,
           da_lam_q1, da_lam_k1, da_lam_q2, da_lam_k2, da_subln_w):
    batch = x.shape[0]
    depth = w_in.shape[0]
    for l in range(depth):
        lambda_init = 0.8 - 0.6 * math.exp(-0.3 * l)
        mod = _modulation(c, w_ada[l], b_ada[l]).reshape(batch, 3, 1, D_MODEL)
        shift, scale, gate = mod[:, 0], mod[:, 1], mod[:, 2]
        feat, rw_gate, dk, dg, dqt, dvt = _in_projection(
            x, shift, scale, g_pre[l].reshape(1, D_MODEL), w_in[l])
        y_rw = _rwkv_branch(feat, rw_gate, rw_mu[l], rw_w0[l], rw_w_w2[l], rw_a0[l],
                            rw_w_a2[l], rw_k_k[l], rw_k_a[l], rw_r_k[l],
                            rw_lnx_w[l], rw_lnx_b[l])
        y_da = _diff_attention(dqt, dk, dvt, dg, da_lam_q1[l], da_lam_k1[l],
                               da_lam_q2[l], da_lam_k2[l], da_subln_w[l], lambda_init)
        x = _out_projection(y_rw, y_da, w_out[l].astype(BF16),
                            g_post[l].reshape(1, D_MODEL), gate, x)
    return x
```

```python
import functools
import math

import jax
import jax.numpy as jnp
from jax import lax
from jax.experimental import pallas as pl
from jax.experimental.pallas import tpu as pltpu

D_MODEL = 1024
SEQ = 4096
CHUNK = 64
RW_WIDTH = 512
RW_HEAD = 64
RW_HEADS = RW_WIDTH // RW_HEAD
RW_RANK = 64
RW_SHIFTED = 3 * RW_WIDTH + 2 * RW_RANK
RW_COLS = RW_SHIFTED + RW_WIDTH
DA_WIDTH = 512
DA_HEADS = 4
DA_VDIM = DA_WIDTH // DA_HEADS
DA_QKDIM = DA_VDIM // 2
D_IN = RW_COLS + 4 * DA_WIDTH
RMS_EPS = 1e-6
GN_EPS = 64e-5
SUBLN_EPS = 1e-5

IN_ROWS = 512
OUT_ROWS = 512
RW_CHUNK = 64
RW_STEP_CHUNKS = 2
RW_GROUP = 4 * RW_HEAD
ATT_TILE = 512
VMEM_LIMIT = 48 * 1024 * 1024

F32 = jnp.float32
BF16 = jnp.bfloat16
HIGHEST = lax.Precision.HIGHEST


def _silu(t):
    return t * jax.nn.sigmoid(t)


def _dot(a, b, precision=None):
    return jnp.dot(a, b, preferred_element_type=F32, precision=precision)


def _dot_nt(a, b):
    return lax.dot_general(a, b, (((1,), (1,)), ((), ())), preferred_element_type=F32)


def _dot_tn(a, b):
    return lax.dot_general(a, b, (((0,), (0,)), ((), ())), preferred_element_type=F32)


def _split_dot(x, w_exact, terms, *, x_is_lhs):
    acc = None
    for _ in range(terms):
        piece = x.astype(BF16)
        part = _dot(piece, w_exact) if x_is_lhs else _dot(w_exact, piece)
        acc = part if acc is None else acc + part
        x = x - piece.astype(F32)
    return acc


def _mod_kernel(c_ref, w_ref, b_ref, o_ref):
    o_ref[...] = _dot(_silu(c_ref[...]), w_ref[...], HIGHEST) + b_ref[...]


def _modulation(c, w_ada, b_ada):
    batch = c.shape[0]
    return pl.pallas_call(
        _mod_kernel,
        grid=(3,),
        in_specs=[
            pl.BlockSpec((batch, D_MODEL), lambda j: (0, 0)),
            pl.BlockSpec((D_MODEL, D_MODEL), lambda j: (0, j)),
            pl.BlockSpec((1, D_MODEL), lambda j: (0, j)),
        ],
        out_specs=pl.BlockSpec((batch, D_MODEL), lambda j: (0, j)),
        out_shape=jax.ShapeDtypeStruct((batch, 3 * D_MODEL), F32),
        compiler_params=pltpu.CompilerParams(vmem_limit_bytes=VMEM_LIMIT),
        name="adaln_mod",
    )(c, w_ada, b_ada.reshape(1, 3 * D_MODEL))


_Q_COLS = (RW_COLS, RW_COLS + DA_WIDTH)
_V_COLS = (RW_COLS + 2 * DA_WIDTH, RW_COLS + 3 * DA_WIDTH)
_ROW_SEGMENTS = (
    (0, RW_SHIFTED),
    (RW_SHIFTED, RW_COLS),
    (RW_COLS + DA_WIDTH, RW_COLS + 2 * DA_WIDTH),
    (RW_COLS + 3 * DA_WIDTH, D_IN),
)
_ROW_WIDTHS = tuple(hi - lo for lo, hi in _ROW_SEGMENTS)
_ROW_DTYPES = (F32, BF16, BF16, BF16)
Q_SCALE = math.log2(math.e) / math.sqrt(DA_QKDIM)


def _in_proj_kernel(x_ref, shift_ref, scale_ref, g_ref, w_ref, wt_ref, *out_refs):
    x = x_ref[0]
    y = x * lax.rsqrt(jnp.mean(x * x, axis=-1, keepdims=True) + RMS_EPS) * g_ref[...]
    h = (y * (1.0 + scale_ref[0]) + shift_ref[0]).astype(BF16)
    lo = 0
    for width, o_ref in zip(_ROW_WIDTHS, out_refs):
        o_ref[0] = _dot(h, w_ref[:, lo:lo + width]).astype(o_ref.dtype)
        lo += width
    qt_ref, vt_ref = out_refs[len(_ROW_WIDTHS):]
    tm = h.shape[0]
    qt = _dot_nt(wt_ref[0:DA_WIDTH, :], h) * Q_SCALE
    vt = _dot_nt(wt_ref[DA_WIDTH:, :], h)
    qt_ref[0, :, 0] = qt.astype(BF16).reshape(DA_HEADS, DA_VDIM, tm)
    vt_ref[0, :, 0] = vt.astype(BF16).reshape(DA_HEADS, DA_VDIM, tm)


def _in_projection(x, shift, scale, g_pre, w_in):
    batch, seq, _ = x.shape
    tm = IN_ROWS
    w_rows =jnp.concatenate([w_in[:, lo:hi] for lo, hi in _ROW_SEGMENTS], axis=1).astype(BF16)
    w_t = jnp.concatenate([w_in[:, lo:hi] for lo, hi in (_Q_COLS, _V_COLS)], axis=1).T.astype(BF16)
    row_spec = lambda width: pl.BlockSpec((1, tm, width), lambda b, i: (b, i, 0))
    vec_spec = pl.BlockSpec((1, 1, D_MODEL), lambda b, i: (b, 0, 0))
    per_tile = ATT_TILE // tm
    t_spec = pl.BlockSpec((1, DA_HEADS, 1, DA_VDIM, tm),
                          lambda b, i: (b, 0, i // per_tile, 0, i % per_tile))
    t_shape = jax.ShapeDtypeStruct((batch, DA_HEADS, seq // ATT_TILE, DA_VDIM, ATT_TILE), BF16)
    return pl.pallas_call(
        _in_proj_kernel,
        grid=(batch, seq // tm),
        in_specs=[
            row_spec(D_MODEL), vec_spec, vec_spec,
            pl.BlockSpec((1, D_MODEL), lambda b, i: (0, 0)),
            pl.BlockSpec(w_rows.shape, lambda b, i: (0, 0), pipeline_mode=pl.Buffered(1)),
            pl.BlockSpec(w_t.shape, lambda b, i: (0, 0), pipeline_mode=pl.Buffered(1)),
        ],
        out_specs=[row_spec(w) for w in _ROW_WIDTHS] + [t_spec, t_spec],
        out_shape=[jax.ShapeDtypeStruct((batch, seq, w), dt)
                   for w, dt in zip(_ROW_WIDTHS, _ROW_DTYPES)] + [t_shape, t_shape],
        compiler_params=pltpu.CompilerParams(
            dimension_semantics=("parallel", "parallel"),
            vmem_limit_bytes=VMEM_LIMIT),
        name="in_proj",
    )(x, shift, scale, g_pre, w_rows, w_t)


def _rwkv_kernel(feat_ref, gate_ref, mu_ref, w0_ref, ww2_ref, a0_ref, wa2_ref,
                 kk_ref, ka_ref, rk_ref, lnw_ref, lnb_ref, y_ref, ubuf, state):
    C, W, N, L = RW_CHUNK, RW_WIDTH, RW_HEAD, RW_GROUP
    heads = L // N
    batch, rows = feat_ref.shape[0], feat_ref.shape[1]
    chunks = rows // C
    ci = pl.program_id(0)

    @pl.when(ci == 0)
    def _():
        ubuf[:, 0:8, :] = jnp.zeros((batch, 8, RW_SHIFTED), F32)
        state[...] = jnp.zeros(state.shape, F32)

    iota = lambda shape, axis: lax.broadcasted_iota(jnp.int32, shape, axis)
    row2, col2 = iota((2 * C, L), 0), iota((2 * C, L), 1) & (N - 1)
    tri = ((row2 & (C - 1)) > col2) | ((row2 >= C) & ((row2 & (C - 1)) == col2))
    r1, c1 = iota((C, L), 0), iota((C, L), 1) & (N - 1)
    eye = (r1 == c1).astype(F32)
    levels = []
    for bit in range(C.bit_length() - 1):
        levels.append(((r1 >> (bit + 1)) == (c1 >> (bit + 1)))
                      & (((r1 >> bit) & 1) == 1) & (((c1 >> bit) & 1) == 0))
    head_bits = N.bit_length() - 1
    same_head = (iota((L, L), 0) >> head_bits) == (iota((L, L), 1) >> head_bits)
    seg = same_head.astype(BF16)
    cum = (iota((C, C), 0) >= iota((C, C), 1)).astype(BF16)

    def block_diag(t):
        return jnp.where(same_head, jnp.concatenate([t] * heads, axis=0), 0.0).astype(BF16)

    def head_sums(t):
        return jnp.concatenate(
            [_dot(t[:, g * L:(g + 1) * L].astype(BF16), seg) for g in range(W // L)], axis=-1)

    roots = []
    for b in range(batch):
        u = feat_ref[b]
        ubuf[b, 8:8 + rows, :] = u
        prev = ubuf[b, 7:7 + rows, :]
        ubuf[b, 0:8, :] = u[rows - 8:rows, :]
        feat = u + (prev - u) * mu_ref[...]
        r, k, v = feat[:, 0:W], feat[:, W:2 * W], feat[:, 2 * W:3 * W]
        w_lo = feat[:, 3 * W:3 * W + RW_RANK]
        a_lo = feat[:, 3 * W + RW_RANK:]
        z = w0_ref[...] + _dot(jnp.tanh(w_lo).astype(BF16), ww2_ref[...])
        logw = -math.exp(-0.5) * jax.nn.sigmoid(z)
        a_pre = a0_ref[...] + _dot(a_lo.astype(BF16), wa2_ref[...])
        kk_raw = k * kk_ref[...]
        norm2 = head_sums(kk_raw * kk_raw)
        cs = [_split_dot(logw[ch * C:(ch + 1) * C], cum, 2, x_is_lhs=False)
              for ch in range(chunks)]
        roots.append((r, k, v, logw, a_pre, kk_raw, norm2, cs))

    prep = {}

    def prepare(ch, anchor=None):
        cr = slice(ch * C, (ch + 1) * C)
        zero = 0.0
        if anchor is not None:
            bits = pltpu.bitcast(anchor[0:1, :], jnp.uint32)
            half_word = jnp.uint32(16)
            bits = lax.shift_right_logical(lax.shift_right_logical(bits, half_word), half_word)
            zero = jnp.concatenate([pltpu.bitcast(bits, F32)] * (W // L), axis=1)
        grs = []
        for b in range(batch):
            r, k, v, logw, a_pre, kk_raw, norm2, cs_all = roots[b]
            r, k, v, logw, cs = r[cr], k[cr], v[cr], logw[cr], cs_all[ch] + zero
            a = jax.nn.sigmoid(a_pre[cr] + zero)
            kk = kk_raw[cr] * lax.rsqrt(jnp.maximum(norm2[cr], 1e-24))
            bb = a * kk
            k2 = k * (1.0 + (a - 1.0) * ka_ref[...])
            cs_last = cs[C - 1:C, :]
            g_all = jnp.exp(cs_last)
            g_inv = jnp.exp(-cs)
            g_end = jnp.exp(cs_last - cs)
            kk_t = (kk * jnp.exp(cs - logw)).astype(BF16)
            r_t = (r * jnp.exp(cs)).astype(BF16)
            k_inv, b_inv = k2 * g_inv, bb * g_inv
            k_end, nb_end = (k2 * g_end).astype(BF16), (-(bb * g_end)).astype(BF16)
            prep[b, ch] = (r, k2, v)
            for g in range(W // L):
                gl = slice(g * L, (g + 1) * L)
                grs.append(dict(
                    b=b, g=g, g_all=g_all[:, gl], v=v[:, gl],
                    lhs=jnp.concatenate([kk_t[:, gl], r_t[:, gl]], axis=0),
                    rhs=jnp.concatenate([block_diag(b_inv[:, gl]), block_diag(k_inv[:, gl])],
                                        axis=0),
                    end=jnp.concatenate([k_end[:, gl], nb_end[:, gl]], axis=0)))
        return grs

    def output_stage(ch, outs):
        cr = slice(ch * C, (ch + 1) * C)
        o = [jnp.concatenate([outs[b, g] for g in range(W // L)], axis=-1) for b in range(batch)]
        mean = [head_sums(t) * (1.0 / N) for t in o]
        yield
        d = [t - m for t, m in zip(o, mean)]
        var = [head_sums(t * t) * (1.0 / N) for t in d]
        bonus = [head_sums(prep[b, ch][0] * prep[b, ch][1] * rk_ref[...]) * prep[b, ch][2]
                 for b in range(batch)]
        yield
        for b in range(batch):
            t = d[b] * lax.rsqrt(var[b] + GN_EPS) * lnw_ref[...] + lnb_ref[...]
            y_ref[b, cr, :] = ((t + bonus[b]) * _silu(gate_ref[b, cr, :].astype(F32))
                               ).astype(y_ref.dtype)

    carried = {(b, g): state[b, g] for b in range(batch) for g in range(W // L)}
    pending = iter(())
    nxt = prepare(0)
    for ch in range(chunks):
        grs = nxt
        for gr in grs:
            m = _dot_nt(gr["lhs"], gr["rhs"])
            gr["a_b"] = jnp.where(tri, m[:, :L], 0.0)
            gr["a_k"] = jnp.where(tri, m[:, L:], 0.0).astype(BF16)
            gr["t"] = eye - jnp.where(levels[0], gr["a_b"][:C], 0.0)
        for gr in grs:
            gr["st"] = carried[gr["b"], gr["g"]]
            gr["x"] = (_dot(gr["a_k"], block_diag(gr["v"]))
                       + _dot_nt(gr["lhs"], block_diag(gr["st"])))
        if ch + 1 < chunks:
            nxt = prepare(ch + 1, anchor=grs[-1]["a_b"])
        for level in levels[1:]:
            for gr in grs:
                off = jnp.where(level, gr["a_b"][:C], 0.0)
                gr["inner"] = _dot(gr["t"].astype(BF16), block_diag(off)).astype(BF16)
            for gr in grs:
                gr["t"] = gr["t"] - _dot(gr["inner"], block_diag(gr["t"]))
            next(pending, None)
        for gr in grs:
            gr["uu"] = _dot(gr["t"].astype(BF16), block_diag(gr["x"][:C]))
        outs = {}
        for gr in grs:
            outs[gr["b"], gr["g"]] = gr["x"][C:] - _dot(gr["a_b"][C:].astype(BF16),
                                                       block_diag(gr["uu"]))
        for gr in grs:
            vu = jnp.concatenate([gr["v"], gr["uu"]], axis=0).astype(BF16)
            delta = jnp.where(same_head, _dot_tn(vu, gr["end"]), 0.0)
            carried[gr["b"], gr["g"]] = gr["st"] * gr["g_all"] + sum(
                delta[h * N:(h + 1) * N] for h in range(heads))
        for _ in pending:
            pass
        pending = output_stage(ch, outs)
    for (b, g), st in carried.items():
        state[b, g] = st
    for _ in pending:
        pass


def _rwkv_branch(feat, gate, mu, w0, w_w2, a0, w_a2, k_k, k_a, r_k, lnx_w, lnx_b):
    batch, seq, _ = feat.shape
    rows = RW_CHUNK * RW_STEP_CHUNKS
    full = lambda shape: pl.BlockSpec(shape, lambda i: (0,) * len(shape))
    vec = lambda t: t.reshape(1, -1)
    return pl.pallas_call(
        _rwkv_kernel,
        grid=(seq // rows,),
        in_specs=[
            pl.BlockSpec((batch, rows, RW_SHIFTED), lambda i: (0, i, 0)),
            pl.BlockSpec((batch, rows, RW_WIDTH), lambda i: (0, i, 0)),
            full((1, RW_SHIFTED)), full((1, RW_WIDTH)), full((RW_RANK, RW_WIDTH)),
            full((1, RW_WIDTH)), full((RW_RANK, RW_WIDTH)),
            full((1, RW_WIDTH)), full((1, RW_WIDTH)), full((1, RW_WIDTH)),
            full((1, RW_WIDTH)), full((1, RW_WIDTH)),
        ],
        out_specs=pl.BlockSpec((batch, rows, RW_WIDTH), lambda i: (0, i, 0)),
        out_shape=jax.ShapeDtypeStruct((batch, seq, RW_WIDTH), BF16),
        scratch_shapes=[
            pltpu.VMEM((batch, rows + 8, RW_SHIFTED), F32),
            pltpu.VMEM((batch, RW_WIDTH // RW_GROUP, RW_HEAD, RW_GROUP), F32),
        ],
        compiler_params=pltpu.CompilerParams(
            dimension_semantics=("arbitrary",),
            vmem_limit_bytes=VMEM_LIMIT),
        name="rwkv7_chunked",
    )(feat, gate, vec(mu), vec(w0), w_w2.astype(BF16), vec(a0), w_a2.astype(BF16),
      vec(k_k), vec(k_a), vec(r_k), vec(lnx_w), vec(lnx_b))


def _attn_kernel(lq1_ref, lk1_ref, lq2_ref, lk2_ref, sw_ref, qt_ref, k_ref, vt_ref,
                 g_ref, o_ref, s_buf, p_buf, acc_buf, *, lambda_init):
    T, HALF = ATT_TILE, ATT_TILE // 2
    tiles = k_ref.shape[1] // T

    comp = lax.broadcasted_iota(jnp.int32, (DA_VDIM, T), 0) < DA_QKDIM
    key = lax.broadcasted_iota(jnp.int32, (HALF, HALF), 0)
    query = lax.broadcasted_iota(jnp.int32, (HALF, HALF), 1)
    visible = (query // CHUNK) >= (key // CHUNK)
    lam = (jnp.exp(jnp.sum(lq1_ref[...] * lk1_ref[...], axis=-1, keepdims=True))
           - jnp.exp(jnp.sum(lq2_ref[...] * lk2_ref[...], axis=-1, keepdims=True))
           + lambda_init)

    for step in range(tiles // 2):
        pl.when(pl.program_id(2) == step)(functools.partial(
            _attn_tasks, [(i, j) for i in (tiles - 1 - step, step) for j in range(i + 1)],
            comp, visible, lam, sw_ref, qt_ref, k_ref, vt_ref, g_ref, o_ref,
            s_buf, p_buf, acc_buf, lambda_init))


def _attn_tasks(tasks, comp, visible, lam, sw_ref, qt_ref, k_ref, vt_ref, g_ref, o_ref,
                s_buf, p_buf, acc_buf, lambda_init):
    T, HALF = ATT_TILE, ATT_TILE // 2
    qt_parts = {}

    def qt_c(i):
        if i not in qt_parts:
            qt = qt_ref[0, 0, i]
            zero = jnp.zeros_like(qt)
            qt_parts[i] = (jnp.where(comp, qt, zero), jnp.where(comp, zero, qt))
        return qt_parts[i]

    def colmax(s):
        return jnp.max(s, axis=0, keepdims=True)

    def colsum(p):
        return jnp.sum(p, axis=0, keepdims=True)

    def scores(n):
        i, j = tasks[n]
        kb = k_ref[0, j * T:(j + 1) * T, :]
        blk_max = []
        for c in range(2):
            q = qt_c(i)[c]
            if j < i:
                s = _dot(kb, q)
                s_buf[n % 2, c] = s
                blk_max.append(colmax(s))
            else:
                s_r = _dot(kb, q[:, HALF:])
                s_b = jnp.where(visible, s_r[HALF:], -jnp.inf)
                s_l = jnp.where(visible, _dot(kb[:HALF], q[:, :HALF]), -jnp.inf)
                s_buf[n % 2, c, :HALF, HALF:] = s_r[:HALF]
                s_buf[n % 2, c, HALF:, HALF:] = s_b
                s_buf[n % 2, c, :HALF, :HALF] = s_l
                blk_max.append(jnp.concatenate(
                    [colmax(s_l), jnp.maximum(colmax(s_r[:HALF]), colmax(s_b))], axis=1))
        return blk_max

    def softmax(n, blk_maxes, stats):
        i, j = tasks[n]
        new, alphas = [], []
        for c in range(2):
            blk_max = blk_maxes[c]
            if j < i:
                s = s_buf[n % 2, c]
            else:
                s_l = s_buf[n % 2, c, :HALF, :HALF]
                s_t = s_buf[n % 2, c, :HALF, HALF:]
                s_b = s_buf[n % 2, c, HALF:, HALF:]
            if j == 0:
                m_new, alpha = blk_max, None
            else:
                m, l = stats[c]
                m_new = jnp.maximum(m, blk_max)
                alpha = jnp.exp2(m - m_new)
            if j < i:
                p = jnp.exp2(s - m_new)
                blk_sum = colsum(p)
                p_buf[n % 2, c] = p.astype(BF16)
            else:
                p_l = jnp.exp2(s_l - m_new[:, :HALF])
                p_t = jnp.exp2(s_t - m_new[:, HALF:])
                p_b = jnp.exp2(s_b - m_new[:, HALF:])
                blk_sum = jnp.concatenate([colsum(p_l), colsum(p_t) + colsum(p_b)], axis=1)
                p_buf[n % 2, c, :HALF, :HALF] = p_l.astype(BF16)
                p_buf[n % 2, c, :HALF, HALF:] = p_t.astype(BF16)
                p_buf[n % 2, c, HALF:, HALF:] = p_b.astype(BF16)
            new.append((m_new, blk_sum if j == 0 else alpha * l + blk_sum))
            alphas.append(alpha)
        return tuple(new), alphas

    def values(n, alphas):
        i, j = tasks[n]
        vtb = vt_ref[0, 0, j]
        for c in range(2):
            if j < i:
                pv = _dot(vtb, p_buf[n % 2, c])
            else:
                pv = jnp.concatenate(
                    [_dot(vtb[:, :HALF], p_buf[n % 2, c, :HALF, :HALF]),
                     _dot(vtb, p_buf[n % 2, c, :, HALF:])], axis=1)
            acc_buf[i % 2, c] = pv if j == 0 else alphas[c] * acc_buf[i % 2, c] + pv

    def finish(i, stats):
        (_, l0), (_, l1) = stats
        o = (acc_buf[i % 2, 0] * (1.0 / l0) - acc_buf[i % 2, 1] * (lam / l1)).T
        o = o * lax.rsqrt(jnp.mean(o * o, axis=-1, keepdims=True) + SUBLN_EPS)
        o = o * sw_ref[...] * (1.0 - lambda_init)
        rows = slice(i * T, (i + 1) * T)
        o_ref[0, rows, :] = (o * _silu(g_ref[0, rows, :].astype(F32))).astype(o_ref.dtype)

    blk_max = scores(0)
    stats, pending = None, None
    for n, (i, j) in enumerate(tasks):
        next_max = scores(n + 1) if n + 1 < len(tasks) else None
        if pending is not None:
            values(pending[0], pending[1])
            if pending[2] is not None:
                finish(tasks[pending[0]][0], pending[2])
        stats, alphas = softmax(n, blk_max, stats)
        pending = (n, alphas, stats if j == i else None)
        blk_max = next_max
    values(pending[0], pending[1])
    finish(tasks[-1][0], pending[2])


def _diff_attention(qt, k, vt, gate, lam_q1, lam_k1, lam_q2, lam_k2, subln_w, lambda_init):
    batch, seq, _ = k.shape
    T = ATT_TILE
    small = lambda n: pl.BlockSpec((1, n), lambda b, h, s: (0, 0))
    rows = pl.BlockSpec((1, seq, DA_VDIM), lambda b, h, s: (b, 0, h))
    cols = pl.BlockSpec((1, 1, seq // T, DA_VDIM, T), lambda b, h, s: (b, h, 0, 0, 0))
    return pl.pallas_call(
        functools.partial(_attn_kernel, lambda_init=lambda_init),
        grid=(batch, DA_HEADS, seq // T // 2),
        in_specs=[small(DA_QKDIM)] * 4 + [small(DA_VDIM), cols, rows, cols, rows],
        out_specs=rows,
        out_shape=jax.ShapeDtypeStruct((batch, seq, DA_WIDTH), BF16),
        scratch_shapes=[pltpu.VMEM((2, 2, T, T), F32),
                        pltpu.VMEM((2, 2, T, T), BF16),
                        pltpu.VMEM((2, 2, DA_VDIM, T), F32)],
        compiler_params=pltpu.CompilerParams(
            dimension_semantics=("parallel", "parallel", "arbitrary"),
            vmem_limit_bytes=VMEM_LIMIT),
        name="diff_attn",
    )(*(t.reshape(1, -1) for t in (lam_q1, lam_k1, lam_q2, lam_k2, subln_w)),
      qt, k, vt, gate)


def _out_proj_kernel(yr_ref, yd_ref, wr_ref, wd_ref, g_ref, gate_ref, x_ref, o_ref):
    y = _dot(yr_ref[0], wr_ref[...]) + _dot(yd_ref[0], wd_ref[...])
    y = y * lax.rsqrt(jnp.mean(y * y, axis=-1, keepdims=True) + RMS_EPS) * g_ref[...]
    o_ref[0] = x_ref[0] + gate_ref[0] * y


def _out_projection(y_rw, y_da, w_out_bf16, g_post, gate, x):
    batch, seq, _ = x.shape
    tm = OUT_ROWS
    half = lambda j: pl.BlockSpec((RW_WIDTH, D_MODEL), lambda b, i: (j, 0),
                                  pipeline_mode=pl.Buffered(1))
    row_spec = lambda width: pl.BlockSpec((1, tm, width), lambda b, i: (b, i, 0))
    return pl.pallas_call(
        _out_proj_kernel,
        grid=(batch, seq // tm),
        in_specs=[
            row_spec(RW_WIDTH), row_spec(DA_WIDTH), half(0), half(1),
            pl.BlockSpec((1, D_MODEL), lambda b, i: (0, 0)),
            pl.BlockSpec((1, 1, D_MODEL), lambda b, i: (b, 0, 0)),
            row_spec(D_MODEL),
        ],
        out_specs=row_spec(D_MODEL),
        out_shape=jax.ShapeDtypeStruct(x.shape, x.dtype),
        compiler_params=pltpu.CompilerParams(
            dimension_semantics=("parallel", "parallel"),
            vmem_limit_bytes=VMEM_LIMIT),
        name="out_proj",
    )(y_rw, y_da, w_out_bf16, w_out_bf16, g_post, gate, x)


def kernel(x, c, w_ada, b_ada, g_pre, g_post, w_in, w_out, rw_mu, rw_w0, rw_w_w2,
           rw_a0, rw_w_a2, rw_k_k, rw_k_a, rw_r_k, rw_lnx_w, rw_lnx_b,
           da_lam_q1, da_lam_k1, da_lam_q2, da_lam_k2, da_subln_w):
    batch = x.shape[0]
    depth = w_in.shape[0]
    for l in range(depth):
        lambda_init = 0.8 - 0.6 * math.exp(-0.3 * l)
        mod = _modulation(c, w_ada[l], b_ada[l]).reshape(batch, 3, 1, D_MODEL)
        shift, scale, gate = mod[:, 0], mod[:, 1], mod[:, 2]
        feat, rw_gate, dk, dg, dqt, dvt = _in_projection(
            x, shift, scale, g_pre[l].reshape(1, D_MODEL), w_in[l])
        y_rw = _rwkv_branch(feat, rw_gate, rw_mu[l], rw_w0[l], rw_w_w2[l], rw_a0[l],
                            rw_w_a2[l], rw_k_k[l], rw_k_a[l], rw_r_k[l],
                            rw_lnx_w[l], rw_lnx_b[l])
        y_da = _diff_attention(dqt, dk, dvt, dg, da_lam_q1[l], da_lam_k1[l],
                               da_lam_q2[l], da_lam_k2[l], da_subln_w[l], lambda_init)
        x = _out_projection(y_rw, y_da, w_out[l].astype(BF16),
                            g_post[l].reshape(1, D_MODEL), gate, x)
    return x
```

```python
import functools
import math

import jax
import jax.numpy as jnp
from jax import lax
from jax.experimental import pallas as pl
from jax.experimental.pallas import tpu as pltpu

D_MODEL = 1024
SEQ = 4096
CHUNK = 64
RW_WIDTH = 512
RW_HEAD = 64
RW_HEADS = RW_WIDTH // RW_HEAD
RW_RANK = 64
RW_SHIFTED = 3 * RW_WIDTH + 2 * RW_RANK
RW_COLS = RW_SHIFTED + RW_WIDTH
DA_WIDTH = 512
DA_HEADS = 4
DA_VDIM = DA_WIDTH // DA_HEADS
DA_QKDIM = DA_VDIM // 2
D_IN = RW_COLS + 4 * DA_WIDTH
RMS_EPS = 1e-6
GN_EPS = 64e-5
SUBLN_EPS = 1e-5

IN_ROWS = 512
OUT_ROWS = 512
RW_CHUNK = 64
RW_STEP_CHUNKS = 2
RW_GROUP = 4 * RW_HEAD
ATT_TILE = 512
ONES_ROWS = 16
VMEM_LIMIT = 48 * 1024 * 1024

F32 = jnp.float32
BF16 = jnp.bfloat16
HIGHEST = lax.Precision.HIGHEST


def _silu(t):
    return t * jax.nn.sigmoid(t)


def _dot(a, b, precision=None):
    return jnp.dot(a, b, preferred_element_type=F32, precision=precision)


def _dot_nt(a, b):
    return lax.dot_general(a, b, (((1,), (1,)), ((), ())), preferred_element_type=F32)


def _dot_tn(a, b):
    return lax.dot_general(a, b, (((0,), (0,)), ((), ())), preferred_element_type=F32)


def _split_dot(x, w_exact, terms, *, x_is_lhs):
    acc = None
    for _ in range(terms):
        piece = x.astype(BF16)
        part = _dot(piece, w_exact) if x_is_lhs else _dot(w_exact, piece)
        acc = part if acc is None else acc + part
        x = x - piece.astype(F32)
    return acc


def _mod_kernel(c_ref, w_ref, b_ref, o_ref):
    o_ref[...] = _dot(_silu(c_ref[...]), w_ref[...], HIGHEST) + b_ref[...]


def _modulation(c, w_ada, b_ada):
    batch = c.shape[0]
    return pl.pallas_call(
        _mod_kernel,
        grid=(3,),
        in_specs=[
            pl.BlockSpec((batch, D_MODEL), lambda j: (0, 0)),
            pl.BlockSpec((D_MODEL, D_MODEL), lambda j: (0, j)),
            pl.BlockSpec((1, D_MODEL), lambda j: (0, j)),
        ],
        out_specs=pl.BlockSpec((batch, D_MODEL), lambda j: (0, j)),
        out_shape=jax.ShapeDtypeStruct((batch, 3 * D_MODEL), F32),
        compiler_params=pltpu.CompilerParams(vmem_limit_bytes=VMEM_LIMIT),
        name="adaln_mod",
    )(c, w_ada, b_ada.reshape(1, 3 * D_MODEL))


_Q_COLS = (RW_COLS, RW_COLS + DA_WIDTH)
_V_COLS = (RW_COLS + 2 * DA_WIDTH, RW_COLS + 3 * DA_WIDTH)
_ROW_SEGMENTS = (
    (0, RW_SHIFTED),
    (RW_SHIFTED, RW_COLS),
    (RW_COLS + DA_WIDTH, RW_COLS + 2 * DA_WIDTH),
    (RW_COLS + 3 * DA_WIDTH, D_IN),
)
_ROW_WIDTHS = tuple(hi - lo for lo, hi in _ROW_SEGMENTS)
_ROW_DTYPES = (F32, BF16, BF16, BF16)
Q_SCALE = math.log2(math.e) / math.sqrt(DA_QKDIM)


def _in_proj_kernel(x_ref, shift_ref, scale_ref, g_ref, w_ref, wt_ref, *out_refs):
    x = x_ref[0]
    y = x * lax.rsqrt(jnp.mean(x * x, axis=-1, keepdims=True) + RMS_EPS) * g_ref[...]
    h = (y * (1.0 + scale_ref[0]) + shift_ref[0]).astype(BF16)
    lo = 0
    for width, o_ref in zip(_ROW_WIDTHS, out_refs):
        o_ref[0] = _dot(h, w_ref[:, lo:lo + width]).astype(o_ref.dtype)
        lo += width
    qt_ref, vt_ref = out_refs[len(_ROW_WIDTHS):]
    tm = h.shape[0]
    qt = _dot_nt(wt_ref[0:DA_WIDTH, :], h) * Q_SCALE
    vt = _dot_nt(wt_ref[DA_WIDTH:, :], h)
    qt_ref[0, :, 0] = qt.astype(BF16).reshape(DA_HEADS, DA_VDIM, tm)
    vt_ref[0, :, 0] = vt.astype(BF16).reshape(DA_HEADS, DA_VDIM, tm)


def _in_projection(x, shift, scale, g_pre, w_in):
    batch, seq, _ = x.shape
    tm = IN_ROWS
    w_rows =jnp.concatenate([w_in[:, lo:hi] for lo, hi in _ROW_SEGMENTS], axis=1).astype(BF16)
    w_t = jnp.concatenate([w_in[:, lo:hi] for lo, hi in (_Q_COLS, _V_COLS)], axis=1).T.astype(BF16)
    row_spec = lambda width: pl.BlockSpec((1, tm, width), lambda b, i: (b, i, 0))
    vec_spec = pl.BlockSpec((1, 1, D_MODEL), lambda b, i: (b, 0, 0))
    per_tile = ATT_TILE // tm
    t_spec = pl.BlockSpec((1, DA_HEADS, 1, DA_VDIM, tm),
                          lambda b, i: (b, 0, i // per_tile, 0, i % per_tile))
    t_shape = jax.ShapeDtypeStruct((batch, DA_HEADS, seq // ATT_TILE, DA_VDIM, ATT_TILE), BF16)
    return pl.pallas_call(
        _in_proj_kernel,
        grid=(batch, seq // tm),
        in_specs=[
            row_spec(D_MODEL), vec_spec, vec_spec,
            pl.BlockSpec((1, D_MODEL), lambda b, i: (0, 0)),
            pl.BlockSpec(w_rows.shape, lambda b, i: (0, 0), pipeline_mode=pl.Buffered(1)),
            pl.BlockSpec(w_t.shape, lambda b, i: (0, 0), pipeline_mode=pl.Buffered(1)),
        ],
        out_specs=[row_spec(w) for w in _ROW_WIDTHS] + [t_spec, t_spec],
        out_shape=[jax.ShapeDtypeStruct((batch, seq, w), dt)
                   for w, dt in zip(_ROW_WIDTHS, _ROW_DTYPES)] + [t_shape, t_shape],
        compiler_params=pltpu.CompilerParams(
            dimension_semantics=("parallel", "parallel"),
            vmem_limit_bytes=VMEM_LIMIT),
        name="in_proj",
    )(x, shift, scale, g_pre, w_rows, w_t)


def _rwkv_kernel(feat_ref, gate_ref, mu_ref, w0_ref, ww2_ref, a0_ref, wa2_ref,
                 kk_ref, ka_ref, rk_ref, lnw_ref, lnb_ref, y_ref, ubuf, state):
    C, W, N, L = RW_CHUNK, RW_WIDTH, RW_HEAD, RW_GROUP
    heads = L // N
    batch, rows = feat_ref.shape[0], feat_ref.shape[1]
    chunks = rows // C
    ci = pl.program_id(0)

    @pl.when(ci == 0)
    def _():
        ubuf[:, 0:8, :] = jnp.zeros((batch, 8, RW_SHIFTED), F32)
        state[...] = jnp.zeros(state.shape, F32)

    iota = lambda shape, axis: lax.broadcasted_iota(jnp.int32, shape, axis)
    row2, col2 = iota((2 * C, L), 0), iota((2 * C, L), 1) & (N - 1)
    tri = ((row2 & (C - 1)) > col2) | ((row2 >= C) & ((row2 & (C - 1)) == col2))
    r1, c1 = iota((C, L), 0), iota((C, L), 1) & (N - 1)
    eye = (r1 == c1).astype(F32)
    levels = []
    for bit in range(C.bit_length() - 1):
        levels.append(((r1 >> (bit + 1)) == (c1 >> (bit + 1)))
                      & (((r1 >> bit) & 1) == 1) & (((c1 >> bit) & 1) == 0))
    head_bits = N.bit_length() - 1
    same_head = (iota((L, L), 0) >> head_bits) == (iota((L, L), 1) >> head_bits)
    seg = same_head.astype(BF16)
    cum = (iota((C, C), 0) >= iota((C, C), 1)).astype(BF16)

    def block_diag(t):
        return jnp.where(same_head, jnp.concatenate([t] * heads, axis=0), 0.0).astype(BF16)

    def head_sums(t):
        return jnp.concatenate(
            [_dot(t[:, g * L:(g + 1) * L].astype(BF16), seg) for g in range(W // L)], axis=-1)

    roots = []
    for b in range(batch):
        u = feat_ref[b]
        ubuf[b, 8:8 + rows, :] = u
        prev = ubuf[b, 7:7 + rows, :]
        ubuf[b, 0:8, :] = u[rows - 8:rows, :]
        feat = u + (prev - u) * mu_ref[...]
        r, k, v = feat[:, 0:W], feat[:, W:2 * W], feat[:, 2 * W:3 * W]
        w_lo = feat[:, 3 * W:3 * W + RW_RANK]
        a_lo = feat[:, 3 * W + RW_RANK:]
        z = w0_ref[...] + _dot(jnp.tanh(w_lo).astype(BF16), ww2_ref[...])
        logw = -math.exp(-0.5) * jax.nn.sigmoid(z)
        a_pre = a0_ref[...] + _dot(a_lo.astype(BF16), wa2_ref[...])
        kk_raw = k * kk_ref[...]
        norm2 = head_sums(kk_raw * kk_raw)
        cs = [_split_dot(logw[ch * C:(ch + 1) * C], cum, 2, x_is_lhs=False)
              for ch in range(chunks)]
        roots.append((r, k, v, logw, a_pre, kk_raw, norm2, cs))

    prep = {}

    def prepare(ch, anchor=None):
        cr = slice(ch * C, (ch + 1) * C)
        zero = 0.0
        if anchor is not None:
            bits = pltpu.bitcast(anchor[0:1, :], jnp.uint32)
            half_word = jnp.uint32(16)
            bits = lax.shift_right_logical(lax.shift_right_logical(bits, half_word), half_word)
            zero = jnp.concatenate([pltpu.bitcast(bits, F32)] * (W // L), axis=1)
        grs = []
        for b in range(batch):
            r, k, v, logw, a_pre, kk_raw, norm2, cs_all = roots[b]
            r, k, v, logw, cs = r[cr], k[cr], v[cr], logw[cr], cs_all[ch] + zero
            a = jax.nn.sigmoid(a_pre[cr] + zero)
            kk = kk_raw[cr] * lax.rsqrt(jnp.maximum(norm2[cr], 1e-24))
            bb = a * kk
            k2 = k * (1.0 + (a - 1.0) * ka_ref[...])
            cs_last = cs[C - 1:C, :]
            g_all = jnp.exp(cs_last)
            g_inv = jnp.exp(-cs)
            g_end = jnp.exp(cs_last - cs)
            kk_t = (kk * jnp.exp(cs - logw)).astype(BF16)
            r_t = (r * jnp.exp(cs)).astype(BF16)
            k_inv, b_inv = k2 * g_inv, bb * g_inv
            k_end, nb_end = (k2 * g_end).astype(BF16), (-(bb * g_end)).astype(BF16)
            prep[b, ch] = (r, k2, v)
            for g in range(W // L):
                gl = slice(g * L, (g + 1) * L)
                grs.append(dict(
                    b=b, g=g, g_all=g_all[:, gl], v=v[:, gl],
                    lhs=jnp.concatenate([kk_t[:, gl], r_t[:, gl]], axis=0),
                    rhs=jnp.concatenate([block_diag(b_inv[:, gl]), block_diag(k_inv[:, gl])],
                                        axis=0),
                    end=jnp.concatenate([k_end[:, gl], nb_end[:, gl]], axis=0)))
        return grs

    def output_stage(ch, outs):
        cr = slice(ch * C, (ch + 1) * C)
        o = [jnp.concatenate([outs[b, g] for g in range(W // L)], axis=-1) for b in range(batch)]
        mean = [head_sums(t) * (1.0 / N) for t in o]
        yield
        d = [t - m for t, m in zip(o, mean)]
        var = [head_sums(t * t) * (1.0 / N) for t in d]
        bonus = [head_sums(prep[b, ch][0] * prep[b, ch][1] * rk_ref[...]) * prep[b, ch][2]
                 for b in range(batch)]
        yield
        for b in range(batch):
            t = d[b] * lax.rsqrt(var[b] + GN_EPS) * lnw_ref[...] + lnb_ref[...]
            y_ref[b, cr, :] = ((t + bonus[b]) * _silu(gate_ref[b, cr, :].astype(F32))
                               ).astype(y_ref.dtype)

    carried = {(b, g): state[b, g] for b in range(batch) for g in range(W // L)}
    pending = iter(())
    nxt = prepare(0)
    for ch in range(chunks):
        grs = nxt
        for gr in grs:
            m = _dot_nt(gr["lhs"], gr["rhs"])
            gr["a_b"] = jnp.where(tri, m[:, :L], 0.0)
            gr["a_k"] = jnp.where(tri, m[:, L:], 0.0).astype(BF16)
            gr["t"] = eye - jnp.where(levels[0], gr["a_b"][:C], 0.0)
        for gr in grs:
            gr["st"] = carried[gr["b"], gr["g"]]
            gr["x"] = (_dot(gr["a_k"], block_diag(gr["v"]))
                       + _dot_nt(gr["lhs"], block_diag(gr["st"])))
        if ch + 1 < chunks:
            nxt = prepare(ch + 1, anchor=grs[-1]["a_b"])
        for level in levels[1:]:
            for gr in grs:
                off = jnp.where(level, gr["a_b"][:C], 0.0)
                gr["inner"] = _dot(gr["t"].astype(BF16), block_diag(off)).astype(BF16)
            for gr in grs:
                gr["t"] = gr["t"] - _dot(gr["inner"], block_diag(gr["t"]))
            next(pending, None)
        for gr in grs:
            gr["uu"] = _dot(gr["t"].astype(BF16), block_diag(gr["x"][:C]))
        outs = {}
        for gr in grs:
            outs[gr["b"], gr["g"]] = gr["x"][C:] - _dot(gr["a_b"][C:].astype(BF16),
                                                       block_diag(gr["uu"]))
        for gr in grs:
            vu = jnp.concatenate([gr["v"], gr["uu"]], axis=0).astype(BF16)
            delta = jnp.where(same_head, _dot_tn(vu, gr["end"]), 0.0)
            carried[gr["b"], gr["g"]] = gr["st"] * gr["g_all"] + sum(
                delta[h * N:(h + 1) * N] for h in range(heads))
        for _ in pending:
            pass
        pending = output_stage(ch, outs)
    for (b, g), st in carried.items():
        state[b, g] = st
    for _ in pending:
        pass


def _rwkv_branch(feat, gate, mu, w0, w_w2, a0, w_a2, k_k, k_a, r_k, lnx_w, lnx_b):
    batch, seq, _ = feat.shape
    rows = RW_CHUNK * RW_STEP_CHUNKS
    full = lambda shape: pl.BlockSpec(shape, lambda i: (0,) * len(shape))
    vec = lambda t: t.reshape(1, -1)
    return pl.pallas_call(
        _rwkv_kernel,
        grid=(seq // rows,),
        in_specs=[
            pl.BlockSpec((batch, rows, RW_SHIFTED), lambda i: (0, i, 0)),
            pl.BlockSpec((batch, rows, RW_WIDTH), lambda i: (0, i, 0)),
            full((1, RW_SHIFTED)), full((1, RW_WIDTH)), full((RW_RANK, RW_WIDTH)),
            full((1, RW_WIDTH)), full((RW_RANK, RW_WIDTH)),
            full((1, RW_WIDTH)), full((1, RW_WIDTH)), full((1, RW_WIDTH)),
            full((1, RW_WIDTH)), full((1, RW_WIDTH)),
        ],
        out_specs=pl.BlockSpec((batch, rows, RW_WIDTH), lambda i: (0, i, 0)),
        out_shape=jax.ShapeDtypeStruct((batch, seq, RW_WIDTH), BF16),
        scratch_shapes=[
            pltpu.VMEM((batch, rows + 8, RW_SHIFTED), F32),
            pltpu.VMEM((batch, RW_WIDTH // RW_GROUP, RW_HEAD, RW_GROUP), F32),
        ],
        compiler_params=pltpu.CompilerParams(
            dimension_semantics=("arbitrary",),
            vmem_limit_bytes=VMEM_LIMIT),
        name="rwkv7_chunked",
    )(feat, gate, vec(mu), vec(w0), w_w2.astype(BF16), vec(a0), w_a2.astype(BF16),
      vec(k_k), vec(k_a), vec(r_k), vec(lnx_w), vec(lnx_b))


def _attn_kernel(lq1_ref, lk1_ref, lq2_ref, lk2_ref, sw_ref, qt_ref, k_ref, vt_ref,
                 g_ref, o_ref, s_buf, p_buf, acc_buf, *, lambda_init):
    T, HALF = ATT_TILE, ATT_TILE // 2
    tiles = k_ref.shape[1] // T

    comp = lax.broadcasted_iota(jnp.int32, (DA_VDIM, T), 0) < DA_QKDIM
    key = lax.broadcasted_iota(jnp.int32, (HALF, HALF), 0)
    query = lax.broadcasted_iota(jnp.int32, (HALF, HALF), 1)
    visible = (query // CHUNK) >= (key // CHUNK)
    lam = (jnp.exp(jnp.sum(lq1_ref[...] * lk1_ref[...], axis=-1, keepdims=True))
           - jnp.exp(jnp.sum(lq2_ref[...] * lk2_ref[...], axis=-1, keepdims=True))
           + lambda_init)

    for step in range(tiles // 2):
        pl.when(pl.program_id(2) == step)(functools.partial(
            _attn_tasks, [(i, j) for i in (tiles - 1 - step, step) for j in range(i + 1)],
            comp, visible, lam, sw_ref, qt_ref, k_ref, vt_ref, g_ref, o_ref,
            s_buf, p_buf, acc_buf, lambda_init))


def _attn_tasks(tasks, comp, visible, lam, sw_ref, qt_ref, k_ref, vt_ref, g_ref, o_ref,
                s_buf, p_buf, acc_buf, lambda_init):
    T, HALF = ATT_TILE, ATT_TILE // 2
    qt_parts = {}

    def qt_c(i):
        if i not in qt_parts:
            qt = qt_ref[0, 0, i]
            zero = jnp.zeros_like(qt)
            qt_parts[i] = (jnp.where(comp, qt, zero), jnp.where(comp, zero, qt))
        return qt_parts[i]

    def colmax(s):
        return jnp.max(s, axis=0, keepdims=True)

    def scores(n):
        i, j = tasks[n]
        kb = k_ref[0, j * T:(j + 1) * T, :]
        blk_max = []
        for c in range(2):
            q = qt_c(i)[c]
            if j < i:
                s = _dot(kb, q)
                s_buf[n % 2, c] = s
                blk_max.append(colmax(s))
            else:
                s_r = _dot(kb, q[:, HALF:])
                s_b = jnp.where(visible, s_r[HALF:], -jnp.inf)
                s_l = jnp.where(visible, _dot(kb[:HALF], q[:, :HALF]), -jnp.inf)
                s_buf[n % 2, c, :HALF, HALF:] = s_r[:HALF]
                s_buf[n % 2, c, HALF:, HALF:] = s_b
                s_buf[n % 2, c, :HALF, :HALF] = s_l
                blk_max.append(jnp.concatenate(
                    [colmax(s_l), jnp.maximum(colmax(s_r[:HALF]), colmax(s_b))], axis=1))
        return blk_max

    def softmax(n, blk_maxes, maxes):
        i, j = tasks[n]
        new, alphas = [], []
        for c in range(2):
            if j == 0:
                m_new, alpha = blk_maxes[c], None
            else:
                m_new = jnp.maximum(maxes[c], blk_maxes[c])
                alpha = jnp.exp2(maxes[c] - m_new)
            prob = lambda s, m: jnp.exp2((s - m).astype(BF16))
            if j < i:
                p_buf[n % 2, c] = prob(s_buf[n % 2, c], m_new)
            else:
                left, right = m_new[:, :HALF], m_new[:, HALF:]
                p_buf[n % 2, c, :HALF, :HALF] = prob(s_buf[n % 2, c, :HALF, :HALF], left)
                p_buf[n % 2, c, :HALF, HALF:] = prob(s_buf[n % 2, c, :HALF, HALF:], right)
                p_buf[n % 2, c, HALF:, HALF:] = prob(s_buf[n % 2, c, HALF:, HALF:], right)
            new.append(m_new)
            alphas.append(alpha)
        return new, alphas

    ones_rows = (lax.broadcasted_iota(jnp.int32, (ONES_ROWS, T), 0) == 0).astype(BF16)

    def values(n, alphas):
        i, j = tasks[n]
        vtb = jnp.concatenate([vt_ref[0, 0, j], ones_rows], axis=0)
        for c in range(2):
            if j < i:
                pv = _dot(vtb, p_buf[n % 2, c])
            else:
                pv = jnp.concatenate(
                    [_dot(vtb[:, :HALF], p_buf[n % 2, c, :HALF, :HALF]),
                     _dot(vtb, p_buf[n % 2, c, :, HALF:])], axis=1)
            acc_buf[i % 2, c] = pv if j == 0 else alphas[c] * acc_buf[i % 2, c] + pv

    def finish(i):
        num0, num1 = acc_buf[i % 2, 0, :DA_VDIM], acc_buf[i % 2, 1, :DA_VDIM]
        l0 = acc_buf[i % 2, 0, DA_VDIM:DA_VDIM + 1]
        l1 = acc_buf[i % 2, 1, DA_VDIM:DA_VDIM + 1]
        o = (num0 * (1.0 / l0) - num1 * (lam / l1)).T
        o = o * lax.rsqrt(jnp.mean(o * o, axis=-1, keepdims=True) + SUBLN_EPS)
        o = o * sw_ref[...] * (1.0 - lambda_init)
        rows = slice(i * T, (i + 1) * T)
        o_ref[0, rows, :] = (o * _silu(g_ref[0, rows, :].astype(F32))).astype(o_ref.dtype)

    blk_max = scores(0)
    maxes, pending = None, None
    for n, (i, j) in enumerate(tasks):
        next_max = scores(n + 1) if n + 1 < len(tasks) else None
        if pending is not None:
            values(pending[0], pending[1])
            if pending[2]:
                finish(tasks[pending[0]][0])
        maxes, alphas = softmax(n, blk_max, maxes)
        pending = (n, alphas, j == i)
        blk_max = next_max
    values(pending[0], pending[1])
    finish(tasks[-1][0])


def _diff_attention(qt, k, vt, gate, lam_q1, lam_k1, lam_q2, lam_k2, subln_w, lambda_init):
    batch, seq, _ = k.shape
    T = ATT_TILE
    small = lambda n: pl.BlockSpec((1, n), lambda b, h, s: (0, 0))
    rows = pl.BlockSpec((1, seq, DA_VDIM), lambda b, h, s: (b, 0, h))
    cols = pl.BlockSpec((1, 1, seq // T, DA_VDIM, T), lambda b, h, s: (b, h, 0, 0, 0))
    return pl.pallas_call(
        functools.partial(_attn_kernel, lambda_init=lambda_init),
        grid=(batch, DA_HEADS, seq // T // 2),
        in_specs=[small(DA_QKDIM)] * 4 + [small(DA_VDIM), cols, rows, cols, rows],
        out_specs=rows,
        out_shape=jax.ShapeDtypeStruct((batch, seq, DA_WIDTH), BF16),
        scratch_shapes=[pltpu.VMEM((2, 2, T, T), F32),
                        pltpu.VMEM((2, 2, T, T), BF16),
                        pltpu.VMEM((2, 2, DA_VDIM + ONES_ROWS, T), F32)],
        compiler_params=pltpu.CompilerParams(
            dimension_semantics=("parallel", "parallel", "arbitrary"),
            vmem_limit_bytes=VMEM_LIMIT),
        name="diff_attn",
    )(*(t.reshape(1, -1) for t in (lam_q1, lam_k1, lam_q2, lam_k2, subln_w)),
      qt, k, vt, gate)


def _out_proj_kernel(yr_ref, yd_ref, wr_ref, wd_ref, g_ref, gate_ref, x_ref, o_ref):
    y = _dot(yr_ref[0], wr_ref[...]) + _dot(yd_ref[0], wd_ref[...])
    y = y * lax.rsqrt(jnp.mean(y * y, axis=-1, keepdims=True) + RMS_EPS) * g_ref[...]
    o_ref[0] = x_ref[0] + gate_ref[0] * y


def _out_projection(y_rw, y_da, w_out_bf16, g_post, gate, x):
    batch, seq, _ = x.shape
    tm = OUT_ROWS
    half = lambda j: pl.BlockSpec((RW_WIDTH, D_MODEL), lambda b, i: (j, 0),
                                  pipeline_mode=pl.Buffered(1))
    row_spec = lambda width: pl.BlockSpec((1, tm, width), lambda b, i: (b, i, 0))
    return pl.pallas_call(
        _out_proj_kernel,
        grid=(batch, seq // tm),
        in_specs=[
            row_spec(RW_WIDTH), row_spec(DA_WIDTH), half(0), half(1),
            pl.BlockSpec((1, D_MODEL), lambda b, i: (0, 0)),
            pl.BlockSpec((1, 1, D_MODEL), lambda b, i: (b, 0, 0)),
            row_spec(D_MODEL),
        ],
        out_specs=row_spec(D_MODEL),
        out_shape=jax.ShapeDtypeStruct(x.shape, x.dtype),
        compiler_params=pltpu.CompilerParams(
            dimension_semantics=("parallel", "parallel"),
            vmem_limit_bytes=VMEM_LIMIT),
        name="out_proj",
    )(y_rw, y_da, w_out_bf16, w_out_bf16, g_post, gate, x)


def kernel(x, c, w_ada, b_ada, g_pre, g_post, w_in, w_out, rw_mu, rw_w0, rw_w_w2,
           rw_a0, rw_w_a2, rw_k_k, rw_k_a, rw_r_k, rw_lnx_w, rw_lnx_b,
           da_lam_q1, da_lam_k1, da_lam_q2, da_lam_k2, da_subln_w):
    batch = x.shape[0]
    depth = w_in.shape[0]
    for l in range(depth):
        lambda_init = 0.8 - 0.6 * math.exp(-0.3 * l)
        mod = _modulation(c, w_ada[l], b_ada[l]).reshape(batch, 3, 1, D_MODEL)
        shift, scale, gate = mod[:, 0], mod[:, 1], mod[:, 2]
        feat, rw_gate, dk, dg, dqt, dvt = _in_projection(
            x, shift, scale, g_pre[l].reshape(1, D_MODEL), w_in[l])
        y_rw = _rwkv_branch(feat, rw_gate, rw_mu[l], rw_w0[l], rw_w_w2[l], rw_a0[l],
                            rw_w_a2[l], rw_k_k[l], rw_k_a[l], rw_r_k[l],
                            rw_lnx_w[l], rw_lnx_b[l])
        y_da = _diff_attention(dqt, dk, dvt, dg, da_lam_q1[l], da_lam_k1[l],
                               da_lam_q2[l], da_lam_k2[l], da_subln_w[l], lambda_init)
        x = _out_projection(y_rw, y_da, w_out[l].astype(BF16),
                            g_post[l].reshape(1, D_MODEL), gate, x)
    return x
```

```python
import functools
import math

import jax
import jax.numpy as jnp
from jax import lax
from jax.experimental import pallas as pl
from jax.experimental.pallas import tpu as pltpu

D_MODEL = 1024
SEQ = 4096
CHUNK = 64
RW_WIDTH = 512
RW_HEAD = 64
RW_HEADS = RW_WIDTH // RW_HEAD
RW_RANK = 64
RW_SHIFTED = 3 * RW_WIDTH + 2 * RW_RANK
RW_COLS = RW_SHIFTED + RW_WIDTH
DA_WIDTH = 512
DA_HEADS = 4
DA_VDIM = DA_WIDTH // DA_HEADS
DA_QKDIM = DA_VDIM // 2
D_IN = RW_COLS + 4 * DA_WIDTH
RMS_EPS = 1e-6
GN_EPS = 64e-5
SUBLN_EPS = 1e-5

IN_ROWS = 512
OUT_ROWS = 512
RW_CHUNK = 64
RW_STEP_CHUNKS = 2
RW_GROUP = 4 * RW_HEAD
ATT_TILE = 512
ONES_ROWS = 16
ATT_STEPS = 2
VMEM_LIMIT = 48 * 1024 * 1024

F32 = jnp.float32
BF16 = jnp.bfloat16
HIGHEST = lax.Precision.HIGHEST


def _silu(t):
    return t * jax.nn.sigmoid(t)


def _dot(a, b, precision=None):
    return jnp.dot(a, b, preferred_element_type=F32, precision=precision)


def _dot_nt(a, b):
    return lax.dot_general(a, b, (((1,), (1,)), ((), ())), preferred_element_type=F32)


def _dot_tn(a, b):
    return lax.dot_general(a, b, (((0,), (0,)), ((), ())), preferred_element_type=F32)


def _split_dot(x, w_exact, terms, *, x_is_lhs):
    acc = None
    for _ in range(terms):
        piece = x.astype(BF16)
        part = _dot(piece, w_exact) if x_is_lhs else _dot(w_exact, piece)
        acc = part if acc is None else acc + part
        x = x - piece.astype(F32)
    return acc


def _mod_kernel(c_ref, w_ref, b_ref, o_ref):
    o_ref[...] = _dot(_silu(c_ref[...]), w_ref[...], HIGHEST) + b_ref[...]


def _modulation(c, w_ada, b_ada):
    batch = c.shape[0]
    return pl.pallas_call(
        _mod_kernel,
        grid=(3,),
        in_specs=[
            pl.BlockSpec((batch, D_MODEL), lambda j: (0, 0)),
            pl.BlockSpec((D_MODEL, D_MODEL), lambda j: (0, j)),
            pl.BlockSpec((1, D_MODEL), lambda j: (0, j)),
        ],
        out_specs=pl.BlockSpec((batch, D_MODEL), lambda j: (0, j)),
        out_shape=jax.ShapeDtypeStruct((batch, 3 * D_MODEL), F32),
        compiler_params=pltpu.CompilerParams(vmem_limit_bytes=VMEM_LIMIT),
        name="adaln_mod",
    )(c, w_ada, b_ada.reshape(1, 3 * D_MODEL))


_Q_COLS = (RW_COLS, RW_COLS + DA_WIDTH)
_V_COLS = (RW_COLS + 2 * DA_WIDTH, RW_COLS + 3 * DA_WIDTH)
_ROW_SEGMENTS = (
    (0, RW_SHIFTED),
    (RW_SHIFTED, RW_COLS),
    (RW_COLS + DA_WIDTH, RW_COLS + 2 * DA_WIDTH),
    (RW_COLS + 3 * DA_WIDTH, D_IN),
)
_ROW_WIDTHS = tuple(hi - lo for lo, hi in _ROW_SEGMENTS)
_ROW_DTYPES = (F32, BF16, BF16, BF16)
Q_SCALE = math.log2(math.e) / math.sqrt(DA_QKDIM)


def _in_proj_kernel(x_ref, shift_ref, scale_ref, g_ref, w_ref, *rest):
    out_refs, wt_ref = rest[:-1], rest[-1]

    @pl.when((pl.program_id(0) == 0) & (pl.program_id(1) == 0))
    def _():
        wt_ref[0:DA_WIDTH, :] = w_ref[:, _Q_COLS[0]:_Q_COLS[1]].T
        wt_ref[DA_WIDTH:, :] = w_ref[:, _V_COLS[0]:_V_COLS[1]].T

    x = x_ref[0]
    y = x * lax.rsqrt(jnp.mean(x * x, axis=-1, keepdims=True) + RMS_EPS) * g_ref[...]
    h = (y * (1.0 + scale_ref[0]) + shift_ref[0]).astype(BF16)
    for (lo, hi), o_ref in zip(_ROW_SEGMENTS, out_refs):
        o_ref[0] = _dot(h, w_ref[:, lo:hi]).astype(o_ref.dtype)
    qt_ref, vt_ref = out_refs[len(_ROW_SEGMENTS):]
    tm = h.shape[0]
    qt = _dot_nt(wt_ref[0:DA_WIDTH, :], h) * Q_SCALE
    vt = _dot_nt(wt_ref[DA_WIDTH:, :], h)
    qt_ref[0, :, 0] = qt.astype(BF16).reshape(DA_HEADS, DA_VDIM, tm)
    vt_ref[0, :, 0] = vt.astype(BF16).reshape(DA_HEADS, DA_VDIM, tm)


def _in_projection(x, shift, scale, g_pre, w_in_bf16):
    batch, seq, _ = x.shape
    tm = IN_ROWS
    row_spec = lambda width: pl.BlockSpec((1, tm, width), lambda b, i: (b, i, 0))
    vec_spec = pl.BlockSpec((1, 1, D_MODEL), lambda b, i: (b, 0, 0))
    per_tile = ATT_TILE // tm
    t_spec = pl.BlockSpec((1, DA_HEADS, 1, DA_VDIM, tm),
                          lambda b, i: (b, 0, i // per_tile, 0, i % per_tile))
    t_shape = jax.ShapeDtypeStruct((batch, DA_HEADS, seq // ATT_TILE, DA_VDIM, ATT_TILE), BF16)
    return pl.pallas_call(
        _in_proj_kernel,
        grid=(batch, seq // tm),
        in_specs=[
            row_spec(D_MODEL), vec_spec, vec_spec,
            pl.BlockSpec((1, D_MODEL), lambda b, i: (0, 0)),
            pl.BlockSpec((D_MODEL, D_IN), lambda b, i: (0, 0), pipeline_mode=pl.Buffered(1)),
        ],
        out_specs=[row_spec(w) for w in _ROW_WIDTHS] + [t_spec, t_spec],
        out_shape=[jax.ShapeDtypeStruct((batch, seq, w), dt)
                   for w, dt in zip(_ROW_WIDTHS, _ROW_DTYPES)] + [t_shape, t_shape],
        scratch_shapes=[pltpu.VMEM((2 * DA_WIDTH, D_MODEL), BF16)],
        compiler_params=pltpu.CompilerParams(
            dimension_semantics=("arbitrary", "arbitrary"),
            vmem_limit_bytes=VMEM_LIMIT),
        name="in_proj",
    )(x, shift, scale, g_pre, w_in_bf16)


def _rwkv_kernel(feat_ref, gate_ref, mu_ref, w0_ref, ww2_ref, a0_ref, wa2_ref,
                 kk_ref, ka_ref, rk_ref, lnw_ref, lnb_ref, y_ref, ubuf, state):
    C, W, N, L = RW_CHUNK, RW_WIDTH, RW_HEAD, RW_GROUP
    heads = L // N
    batch, rows = feat_ref.shape[0], feat_ref.shape[1]
    chunks = rows // C
    ci = pl.program_id(0)

    @pl.when(ci == 0)
    def _():
        ubuf[:, 0:8, :] = jnp.zeros((batch, 8, RW_SHIFTED), F32)
        state[...] = jnp.zeros(state.shape, F32)

    iota = lambda shape, axis: lax.broadcasted_iota(jnp.int32, shape, axis)
    row2, col2 = iota((2 * C, L), 0), iota((2 * C, L), 1) & (N - 1)
    tri = ((row2 & (C - 1)) > col2) | ((row2 >= C) & ((row2 & (C - 1)) == col2))
    r1, c1 = iota((C, L), 0), iota((C, L), 1) & (N - 1)
    eye = (r1 == c1).astype(F32)
    levels = []
    for bit in range(C.bit_length() - 1):
        levels.append(((r1 >> (bit + 1)) == (c1 >> (bit + 1)))
                      & (((r1 >> bit) & 1) == 1) & (((c1 >> bit) & 1) == 0))
    head_bits = N.bit_length() - 1
    same_head = (iota((L, L), 0) >> head_bits) == (iota((L, L), 1) >> head_bits)
    seg = same_head.astype(BF16)
    cum = (iota((C, C), 0) >= iota((C, C), 1)).astype(BF16)

    def block_diag(t):
        return jnp.where(same_head, jnp.concatenate([t] * heads, axis=0), 0.0).astype(BF16)

    def head_sums(t):
        return jnp.concatenate(
            [_dot(t[:, g * L:(g + 1) * L].astype(BF16), seg) for g in range(W // L)], axis=-1)

    roots = []
    for b in range(batch):
        u = feat_ref[b]
        ubuf[b, 8:8 + rows, :] = u
        prev = ubuf[b, 7:7 + rows, :]
        ubuf[b, 0:8, :] = u[rows - 8:rows, :]
        feat = u + (prev - u) * mu_ref[...]
        r, k, v = feat[:, 0:W], feat[:, W:2 * W], feat[:, 2 * W:3 * W]
        w_lo = feat[:, 3 * W:3 * W + RW_RANK]
        a_lo = feat[:, 3 * W + RW_RANK:]
        z = w0_ref[...] + _dot(jnp.tanh(w_lo).astype(BF16), ww2_ref[...])
        logw = -math.exp(-0.5) * jax.nn.sigmoid(z)
        a_pre = a0_ref[...] + _dot(a_lo.astype(BF16), wa2_ref[...])
        kk_raw = k * kk_ref[...]
        norm2 = head_sums(kk_raw * kk_raw)
        cs = [_split_dot(logw[ch * C:(ch + 1) * C], cum, 2, x_is_lhs=False)
              for ch in range(chunks)]
        roots.append((r, k, v, logw, a_pre, kk_raw, norm2, cs))

    prep = {}

    def prepare(ch, anchor=None):
        cr = slice(ch * C, (ch + 1) * C)
        zero = 0.0
        if anchor is not None:
            bits = pltpu.bitcast(anchor[0:1, :], jnp.uint32)
            half_word = jnp.uint32(16)
            bits = lax.shift_right_logical(lax.shift_right_logical(bits, half_word), half_word)
            zero = jnp.concatenate([pltpu.bitcast(bits, F32)] * (W // L), axis=1)
        grs = []
        for b in range(batch):
            r, k, v, logw, a_pre, kk_raw, norm2, cs_all = roots[b]
            r, k, v, logw, cs = r[cr], k[cr], v[cr], logw[cr], cs_all[ch] + zero
            a = jax.nn.sigmoid(a_pre[cr] + zero)
            kk = kk_raw[cr] * lax.rsqrt(jnp.maximum(norm2[cr], 1e-24))
            bb = a * kk
            k2 = k * (1.0 + (a - 1.0) * ka_ref[...])
            cs_last = cs[C - 1:C, :]
            g_all = jnp.exp(cs_last)
            g_inv = jnp.exp(-cs)
            g_end = jnp.exp(cs_last - cs)
            kk_t = (kk * jnp.exp(cs - logw)).astype(BF16)
            r_t = (r * jnp.exp(cs)).astype(BF16)
            k_inv, b_inv = k2 * g_inv, bb * g_inv
            k_end, nb_end = (k2 * g_end).astype(BF16), (-(bb * g_end)).astype(BF16)
            prep[b, ch] = (r, k2, v)
            for g in range(W // L):
                gl = slice(g * L, (g + 1) * L)
                grs.append(dict(
                    b=b, g=g, g_all=g_all[:, gl], v=v[:, gl],
                    lhs=jnp.concatenate([kk_t[:, gl], r_t[:, gl]], axis=0),
                    rhs=jnp.concatenate([block_diag(b_inv[:, gl]), block_diag(k_inv[:, gl])],
                                        axis=0),
                    end=jnp.concatenate([k_end[:, gl], nb_end[:, gl]], axis=0)))
        return grs

    def output_stage(ch, outs):
        cr = slice(ch * C, (ch + 1) * C)
        o = [jnp.concatenate([outs[b, g] for g in range(W // L)], axis=-1) for b in range(batch)]
        mean = [head_sums(t) * (1.0 / N) for t in o]
        yield
        d = [t - m for t, m in zip(o, mean)]
        var = [head_sums(t * t) * (1.0 / N) for t in d]
        bonus = [head_sums(prep[b, ch][0] * prep[b, ch][1] * rk_ref[...]) * prep[b, ch][2]
                 for b in range(batch)]
        yield
        for b in range(batch):
            t = d[b] * lax.rsqrt(var[b] + GN_EPS) * lnw_ref[...] + lnb_ref[...]
            y_ref[b, cr, :] = ((t + bonus[b]) * _silu(gate_ref[b, cr, :].astype(F32))
                               ).astype(y_ref.dtype)

    carried = {(b, g): state[b, g] for b in range(batch) for g in range(W // L)}
    pending = iter(())
    nxt = prepare(0)
    for ch in range(chunks):
        grs = nxt
        for gr in grs:
            m = _dot_nt(gr["lhs"], gr["rhs"])
            gr["a_b"] = jnp.where(tri, m[:, :L], 0.0)
            gr["a_k"] = jnp.where(tri, m[:, L:], 0.0).astype(BF16)
            gr["t"] = eye - jnp.where(levels[0], gr["a_b"][:C], 0.0)
        for gr in grs:
            gr["st"] = carried[gr["b"], gr["g"]]
            gr["x"] = (_dot(gr["a_k"], block_diag(gr["v"]))
                       + _dot_nt(gr["lhs"], block_diag(gr["st"])))
        if ch + 1 < chunks:
            nxt = prepare(ch + 1, anchor=grs[-1]["a_b"])
        for level in levels[1:]:
            for gr in grs:
                off = jnp.where(level, gr["a_b"][:C], 0.0)
                gr["inner"] = _dot(gr["t"].astype(BF16), block_diag(off)).astype(BF16)
            for gr in grs:
                gr["t"] = gr["t"] - _dot(gr["inner"], block_diag(gr["t"]))
            next(pending, None)
        for gr in grs:
            gr["uu"] = _dot(gr["t"].astype(BF16), block_diag(gr["x"][:C]))
        outs = {}
        for gr in grs:
            outs[gr["b"], gr["g"]] = gr["x"][C:] - _dot(gr["a_b"][C:].astype(BF16),
                                                       block_diag(gr["uu"]))
        for gr in grs:
            vu = jnp.concatenate([gr["v"], gr["uu"]], axis=0).astype(BF16)
            delta = jnp.where(same_head, _dot_tn(vu, gr["end"]), 0.0)
            carried[gr["b"], gr["g"]] = gr["st"] * gr["g_all"] + sum(
                delta[h * N:(h + 1) * N] for h in range(heads))
        for _ in pending:
            pass
        pending = output_stage(ch, outs)
    for (b, g), st in carried.items():
        state[b, g] = st
    for _ in pending:
        pass


def _rwkv_branch(feat, gate, mu, w0, w_w2, a0, w_a2, k_k, k_a, r_k, lnx_w, lnx_b):
    batch, seq, _ = feat.shape
    rows = RW_CHUNK * RW_STEP_CHUNKS
    full = lambda shape: pl.BlockSpec(shape, lambda i: (0,) * len(shape))
    vec = lambda t: t.reshape(1, -1)
    return pl.pallas_call(
        _rwkv_kernel,
        grid=(seq // rows,),
        in_specs=[
            pl.BlockSpec((batch, rows, RW_SHIFTED), lambda i: (0, i, 0)),
            pl.BlockSpec((batch, rows, RW_WIDTH), lambda i: (0, i, 0)),
            full((1, RW_SHIFTED)), full((1, RW_WIDTH)), full((RW_RANK, RW_WIDTH)),
            full((1, RW_WIDTH)), full((RW_RANK, RW_WIDTH)),
            full((1, RW_WIDTH)), full((1, RW_WIDTH)), full((1, RW_WIDTH)),
            full((1, RW_WIDTH)), full((1, RW_WIDTH)),
        ],
        out_specs=pl.BlockSpec((batch, rows, RW_WIDTH), lambda i: (0, i, 0)),
        out_shape=jax.ShapeDtypeStruct((batch, seq, RW_WIDTH), BF16),
        scratch_shapes=[
            pltpu.VMEM((batch, rows + 8, RW_SHIFTED), F32),
            pltpu.VMEM((batch, RW_WIDTH // RW_GROUP, RW_HEAD, RW_GROUP), F32),
        ],
        compiler_params=pltpu.CompilerParams(
            dimension_semantics=("arbitrary",),
            vmem_limit_bytes=VMEM_LIMIT),
        name="rwkv7_chunked",
    )(feat, gate, vec(mu), vec(w0), w_w2.astype(BF16), vec(a0), w_a2.astype(BF16),
      vec(k_k), vec(k_a), vec(r_k), vec(lnx_w), vec(lnx_b))


def _attn_kernel(lq1_ref, lk1_ref, lq2_ref, lk2_ref, sw_ref, qt_ref, k_ref, vt_ref,
                 g_ref, o_ref, s_buf, p_buf, acc_buf, *, lambda_init):
    T, HALF = ATT_TILE, ATT_TILE // 2
    tiles = k_ref.shape[1] // T

    comp = lax.broadcasted_iota(jnp.int32, (DA_VDIM, T), 0) < DA_QKDIM
    key = lax.broadcasted_iota(jnp.int32, (HALF, HALF), 0)
    query = lax.broadcasted_iota(jnp.int32, (HALF, HALF), 1)
    visible = (query // CHUNK) >= (key // CHUNK)
    lam = (jnp.exp(jnp.sum(lq1_ref[...] * lk1_ref[...], axis=-1, keepdims=True))
           - jnp.exp(jnp.sum(lq2_ref[...] * lk2_ref[...], axis=-1, keepdims=True))
           + lambda_init)

    for step in range(ATT_STEPS):
        order = [i for k in range(step, tiles // 2, ATT_STEPS) for i in (tiles - 1 - k, k)]
        pl.when(pl.program_id(2) == step)(functools.partial(
            _attn_tasks, [(i, j) for i in order for j in range(i + 1)],
            comp, visible, lam, sw_ref, qt_ref, k_ref, vt_ref, g_ref, o_ref,
            s_buf, p_buf, acc_buf, lambda_init))


def _attn_tasks(tasks, comp, visible, lam, sw_ref, qt_ref, k_ref, vt_ref, g_ref, o_ref,
                s_buf, p_buf, acc_buf, lambda_init):
    T, HALF = ATT_TILE, ATT_TILE // 2
    qt_parts = {}

    def qt_c(i):
        if i not in qt_parts:
            qt = qt_ref[0, 0, i]
            zero = jnp.zeros_like(qt)
            qt_parts[i] = (jnp.where(comp, qt, zero), jnp.where(comp, zero, qt))
        return qt_parts[i]

    def colmax(s):
        return jnp.max(s, axis=0, keepdims=True)

    def scores(n):
        i, j = tasks[n]
        kb = k_ref[0, j * T:(j + 1) * T, :]
        blk_max = []
        for c in range(2):
            q = qt_c(i)[c]
            if j < i:
                s = _dot(kb, q)
                s_buf[n % 2, c] = s
                blk_max.append(colmax(s))
            else:
                s_r = _dot(kb, q[:, HALF:])
                s_b = jnp.where(visible, s_r[HALF:], -jnp.inf)
                s_l = jnp.where(visible, _dot(kb[:HALF], q[:, :HALF]), -jnp.inf)
                s_buf[n % 2, c, :HALF, HALF:] = s_r[:HALF]
                s_buf[n % 2, c, HALF:, HALF:] = s_b
                s_buf[n % 2, c, :HALF, :HALF] = s_l
                blk_max.append(jnp.concatenate(
                    [colmax(s_l), jnp.maximum(colmax(s_r[:HALF]), colmax(s_b))], axis=1))
        return blk_max

    def softmax(n, blk_maxes, maxes):
        i, j = tasks[n]
        new, alphas = [], []
        for c in range(2):
            if j == 0:
                m_new, alpha = blk_maxes[c], None
            else:
                m_new = jnp.maximum(maxes[c], blk_maxes[c])
                alpha = jnp.exp2(maxes[c] - m_new)
            prob = lambda s, m: jnp.exp2((s - m).astype(BF16))
            if j < i:
                p_buf[n % 2, c] = prob(s_buf[n % 2, c], m_new)
            else:
                left, right = m_new[:, :HALF], m_new[:, HALF:]
                p_buf[n % 2, c, :HALF, :HALF] = prob(s_buf[n % 2, c, :HALF, :HALF], left)
                p_buf[n % 2, c, :HALF, HALF:] = prob(s_buf[n % 2, c, :HALF, HALF:], right)
                p_buf[n % 2, c, HALF:, HALF:] = prob(s_buf[n % 2, c, HALF:, HALF:], right)
            new.append(m_new)
            alphas.append(alpha)
        return new, alphas

    ones_rows = (lax.broadcasted_iota(jnp.int32, (ONES_ROWS, T), 0) == 0).astype(BF16)

    def values(n, alphas):
        i, j = tasks[n]
        vtb = jnp.concatenate([vt_ref[0, 0, j], ones_rows], axis=0)
        for c in range(2):
            if j < i:
                pv = _dot(vtb, p_buf[n % 2, c])
            else:
                pv = jnp.concatenate(
                    [_dot(vtb[:, :HALF], p_buf[n % 2, c, :HALF, :HALF]),
                     _dot(vtb, p_buf[n % 2, c, :, HALF:])], axis=1)
            acc_buf[i % 2, c] = pv if j == 0 else alphas[c] * acc_buf[i % 2, c] + pv

    def finish(i):
        num0, num1 = acc_buf[i % 2, 0, :DA_VDIM], acc_buf[i % 2, 1, :DA_VDIM]
        l0 = acc_buf[i % 2, 0, DA_VDIM:DA_VDIM + 1]
        l1 = acc_buf[i % 2, 1, DA_VDIM:DA_VDIM + 1]
        o = (num0 * (1.0 / l0) - num1 * (lam / l1)).T
        o = o * lax.rsqrt(jnp.mean(o * o, axis=-1, keepdims=True) + SUBLN_EPS)
        o = o * sw_ref[...] * (1.0 - lambda_init)
        rows = slice(i * T, (i + 1) * T)
        o_ref[0, rows, :] = (o * _silu(g_ref[0, rows, :].astype(F32))).astype(o_ref.dtype)

    blk_max = scores(0)
    maxes, pending = None, None
    for n, (i, j) in enumerate(tasks):
        next_max = scores(n + 1) if n + 1 < len(tasks) else None
        if pending is not None:
            values(pending[0], pending[1])
            if pending[2]:
                finish(tasks[pending[0]][0])
        maxes, alphas = softmax(n, blk_max, maxes)
        pending = (n, alphas, j == i)
        blk_max = next_max
    values(pending[0], pending[1])
    finish(tasks[-1][0])


def _diff_attention(qt, k, vt, gate, lam_q1, lam_k1, lam_q2, lam_k2, subln_w, lambda_init):
    batch, seq, _ = k.shape
    T = ATT_TILE
    small = lambda n: pl.BlockSpec((1, n), lambda b, h, s: (0, 0))
    rows = pl.BlockSpec((1, seq, DA_VDIM), lambda b, h, s: (b, 0, h))
    cols = pl.BlockSpec((1, 1, seq // T, DA_VDIM, T), lambda b, h, s: (b, h, 0, 0, 0))
    return pl.pallas_call(
        functools.partial(_attn_kernel, lambda_init=lambda_init),
        grid=(batch, DA_HEADS, ATT_STEPS),
        in_specs=[small(DA_QKDIM)] * 4 + [small(DA_VDIM), cols, rows, cols, rows],
        out_specs=rows,
        out_shape=jax.ShapeDtypeStruct((batch, seq, DA_WIDTH), BF16),
        scratch_shapes=[pltpu.VMEM((2, 2, T, T), F32),
                        pltpu.VMEM((2, 2, T, T), BF16),
                        pltpu.VMEM((2, 2, DA_VDIM + ONES_ROWS, T), F32)],
        compiler_params=pltpu.CompilerParams(
            dimension_semantics=("parallel", "parallel", "arbitrary"),
            vmem_limit_bytes=VMEM_LIMIT),
        name="diff_attn",
    )(*(t.reshape(1, -1) for t in (lam_q1, lam_k1, lam_q2, lam_k2, subln_w)),
      qt, k, vt, gate)


def _out_proj_kernel(yr_ref, yd_ref, wr_ref, wd_ref, g_ref, gate_ref, x_ref, o_ref):
    y = _dot(yr_ref[0], wr_ref[...]) + _dot(yd_ref[0], wd_ref[...])
    y = y * lax.rsqrt(jnp.mean(y * y, axis=-1, keepdims=True) + RMS_EPS) * g_ref[...]
    o_ref[0] = x_ref[0] + gate_ref[0] * y


def _out_projection(y_rw, y_da, w_out_bf16, g_post, gate, x):
    batch, seq, _ = x.shape
    tm = OUT_ROWS
    half = lambda j: pl.BlockSpec((RW_WIDTH, D_MODEL), lambda b, i: (j, 0),
                                  pipeline_mode=pl.Buffered(1))
    row_spec = lambda width: pl.BlockSpec((1, tm, width), lambda b, i: (b, i, 0))
    return pl.pallas_call(
        _out_proj_kernel,
        grid=(batch, seq // tm),
        in_specs=[
            row_spec(RW_WIDTH), row_spec(DA_WIDTH), half(0), half(1),
            pl.BlockSpec((1, D_MODEL), lambda b, i: (0, 0)),
            pl.BlockSpec((1, 1, D_MODEL), lambda b, i: (b, 0, 0)),
            row_spec(D_MODEL),
        ],
        out_specs=row_spec(D_MODEL),
        out_shape=jax.ShapeDtypeStruct(x.shape, x.dtype),
        compiler_params=pltpu.CompilerParams(
            dimension_semantics=("parallel", "parallel"),
            vmem_limit_bytes=VMEM_LIMIT),
        name="out_proj",
    )(y_rw, y_da, w_out_bf16, w_out_bf16, g_post, gate, x)


def kernel(x, c, w_ada, b_ada, g_pre, g_post, w_in, w_out, rw_mu, rw_w0, rw_w_w2,
           rw_a0, rw_w_a2, rw_k_k, rw_k_a, rw_r_k, rw_lnx_w, rw_lnx_b,
           da_lam_q1, da_lam_k1, da_lam_q2, da_lam_k2, da_subln_w):
    batch = x.shape[0]
    depth = w_in.shape[0]
    for l in range(depth):
        lambda_init = 0.8 - 0.6 * math.exp(-0.3 * l)
        mod = _modulation(c, w_ada[l], b_ada[l]).reshape(batch, 3, 1, D_MODEL)
        shift, scale, gate = mod[:, 0], mod[:, 1], mod[:, 2]
        feat, rw_gate, dk, dg, dqt, dvt = _in_projection(
            x, shift, scale, g_pre[l].reshape(1, D_MODEL), w_in[l].astype(BF16))
        y_rw = _rwkv_branch(feat, rw_gate, rw_mu[l], rw_w0[l], rw_w_w2[l], rw_a0[l],
                            rw_w_a2[l], rw_k_k[l], rw_k_a[l], rw_r_k[l],
                            rw_lnx_w[l], rw_lnx_b[l])
        y_da = _diff_attention(dqt, dk, dvt, dg, da_lam_q1[l], da_lam_k1[l],
                               da_lam_q2[l], da_lam_k2[l], da_subln_w[l], lambda_init)
        x = _out_projection(y_rw, y_da, w_out[l].astype(BF16),
                            g_post[l].reshape(1, D_MODEL), gate, x)
    return x
```

```python
import functools
import math

import jax
import jax.numpy as jnp
from jax import lax
from jax.experimental import pallas as pl
from jax.experimental.pallas import tpu as pltpu

D_MODEL = 1024
SEQ = 4096
CHUNK = 64
RW_WIDTH = 512
RW_HEAD = 64
RW_HEADS = RW_WIDTH // RW_HEAD
RW_RANK = 64
RW_SHIFTED = 3 * RW_WIDTH + 2 * RW_RANK
RW_COLS = RW_SHIFTED + RW_WIDTH
DA_WIDTH = 512
DA_HEADS = 4
DA_VDIM = DA_WIDTH // DA_HEADS
DA_QKDIM = DA_VDIM // 2
D_IN = RW_COLS + 4 * DA_WIDTH
RMS_EPS = 1e-6
GN_EPS = 64e-5
SUBLN_EPS = 1e-5

IN_ROWS = 512
OUT_ROWS = 512
RW_CHUNK = 64
RW_STEP_CHUNKS = 2
RW_GROUP = 2 * RW_HEAD
ATT_TILE = 512
ONES_ROWS = 16
ATT_STEPS = 2
VMEM_LIMIT = 48 * 1024 * 1024

F32 = jnp.float32
BF16 = jnp.bfloat16
HIGHEST = lax.Precision.HIGHEST


def _silu(t):
    return t * jax.nn.sigmoid(t)


def _dot(a, b, precision=None):
    return jnp.dot(a, b, preferred_element_type=F32, precision=precision)


def _dot_nt(a, b):
    return lax.dot_general(a, b, (((1,), (1,)), ((), ())), preferred_element_type=F32)


def _dot_tn(a, b):
    return lax.dot_general(a, b, (((0,), (0,)), ((), ())), preferred_element_type=F32)


def _split_dot(x, w_exact, terms, *, x_is_lhs):
    acc = None
    for _ in range(terms):
        piece = x.astype(BF16)
        part = _dot(piece, w_exact) if x_is_lhs else _dot(w_exact, piece)
        acc = part if acc is None else acc + part
        x = x - piece.astype(F32)
    return acc


def _mod_kernel(c_ref, w_ref, b_ref, o_ref):
    o_ref[...] = _dot(_silu(c_ref[...]), w_ref[...], HIGHEST) + b_ref[...]


def _modulation(c, w_ada, b_ada):
    batch = c.shape[0]
    return pl.pallas_call(
        _mod_kernel,
        grid=(3,),
        in_specs=[
            pl.BlockSpec((batch, D_MODEL), lambda j: (0, 0)),
            pl.BlockSpec((D_MODEL, D_MODEL), lambda j: (0, j)),
            pl.BlockSpec((1, D_MODEL), lambda j: (0, j)),
        ],
        out_specs=pl.BlockSpec((batch, D_MODEL), lambda j: (0, j)),
        out_shape=jax.ShapeDtypeStruct((batch, 3 * D_MODEL), F32),
        compiler_params=pltpu.CompilerParams(vmem_limit_bytes=VMEM_LIMIT),
        name="adaln_mod",
    )(c, w_ada, b_ada.reshape(1, 3 * D_MODEL))


_Q_COLS = (RW_COLS, RW_COLS + DA_WIDTH)
_V_COLS = (RW_COLS + 2 * DA_WIDTH, RW_COLS + 3 * DA_WIDTH)
_ROW_SEGMENTS = (
    (0, RW_SHIFTED),
    (RW_SHIFTED, RW_COLS),
    (RW_COLS + DA_WIDTH, RW_COLS + 2 * DA_WIDTH),
    (RW_COLS + 3 * DA_WIDTH, D_IN),
)
_ROW_WIDTHS = tuple(hi - lo for lo, hi in _ROW_SEGMENTS)
_ROW_DTYPES = (F32, BF16, BF16, BF16)
Q_SCALE = math.log2(math.e) / math.sqrt(DA_QKDIM)


def _in_proj_kernel(x_ref, shift_ref, scale_ref, g_ref, w_ref, *rest):
    out_refs, wt_ref = rest[:-1], rest[-1]

    @pl.when((pl.program_id(0) == 0) & (pl.program_id(1) == 0))
    def _():
        wt_ref[0:DA_WIDTH, :] = w_ref[:, _Q_COLS[0]:_Q_COLS[1]].T
        wt_ref[DA_WIDTH:, :] = w_ref[:, _V_COLS[0]:_V_COLS[1]].T

    x = x_ref[0]
    y = x * lax.rsqrt(jnp.mean(x * x, axis=-1, keepdims=True) + RMS_EPS) * g_ref[...]
    h = (y * (1.0 + scale_ref[0]) + shift_ref[0]).astype(BF16)
    for (lo, hi), o_ref in zip(_ROW_SEGMENTS, out_refs):
        o_ref[0] = _dot(h, w_ref[:, lo:hi]).astype(o_ref.dtype)
    qt_ref, vt_ref = out_refs[len(_ROW_SEGMENTS):]
    tm = h.shape[0]
    qt = _dot_nt(wt_ref[0:DA_WIDTH, :], h) * Q_SCALE
    vt = _dot_nt(wt_ref[DA_WIDTH:, :], h)
    qt_ref[0, :, 0] = qt.astype(BF16).reshape(DA_HEADS, DA_VDIM, tm)
    vt_ref[0, :, 0] = vt.astype(BF16).reshape(DA_HEADS, DA_VDIM, tm)


def _in_projection(x, shift, scale, g_pre, w_in_bf16):
    batch, seq, _ = x.shape
    tm = IN_ROWS
    row_spec = lambda width: pl.BlockSpec((1, tm, width), lambda b, i: (b, i, 0))
    vec_spec = pl.BlockSpec((1, 1, D_MODEL), lambda b, i: (b, 0, 0))
    per_tile = ATT_TILE // tm
    t_spec = pl.BlockSpec((1, DA_HEADS, 1, DA_VDIM, tm),
                          lambda b, i: (b, 0, i // per_tile, 0, i % per_tile))
    t_shape = jax.ShapeDtypeStruct((batch, DA_HEADS, seq // ATT_TILE, DA_VDIM, ATT_TILE), BF16)
    return pl.pallas_call(
        _in_proj_kernel,
        grid=(batch, seq // tm),
        in_specs=[
            row_spec(D_MODEL), vec_spec, vec_spec,
            pl.BlockSpec((1, D_MODEL), lambda b, i: (0, 0)),
            pl.BlockSpec((D_MODEL, D_IN), lambda b, i: (0, 0), pipeline_mode=pl.Buffered(1)),
        ],
        out_specs=[row_spec(w) for w in _ROW_WIDTHS] + [t_spec, t_spec],
        out_shape=[jax.ShapeDtypeStruct((batch, seq, w), dt)
                   for w, dt in zip(_ROW_WIDTHS, _ROW_DTYPES)] + [t_shape, t_shape],
        scratch_shapes=[pltpu.VMEM((2 * DA_WIDTH, D_MODEL), BF16)],
        compiler_params=pltpu.CompilerParams(
            dimension_semantics=("arbitrary", "arbitrary"),
            vmem_limit_bytes=VMEM_LIMIT),
        name="in_proj",
    )(x, shift, scale, g_pre, w_in_bf16)


def _rwkv_kernel(feat_ref, gate_ref, mu_ref, w0_ref, ww2_ref, a0_ref, wa2_ref,
                 kk_ref, ka_ref, rk_ref, lnw_ref, lnb_ref, y_ref, ubuf, state):
    C, W, N, L = RW_CHUNK, RW_WIDTH, RW_HEAD, RW_GROUP
    heads = L // N
    batch, rows = feat_ref.shape[0], feat_ref.shape[1]
    chunks = rows // C
    ci = pl.program_id(0)

    @pl.when(ci == 0)
    def _():
        ubuf[:, 0:8, :] = jnp.zeros((batch, 8, RW_SHIFTED), F32)
        state[...] = jnp.zeros(state.shape, F32)

    iota = lambda shape, axis: lax.broadcasted_iota(jnp.int32, shape, axis)
    row2, col2 = iota((2 * C, L), 0), iota((2 * C, L), 1) & (N - 1)
    tri = ((row2 & (C - 1)) > col2) | ((row2 >= C) & ((row2 & (C - 1)) == col2))
    r1, c1 = iota((C, L), 0), iota((C, L), 1) & (N - 1)
    eye = (r1 == c1).astype(F32)
    levels = []
    for bit in range(C.bit_length() - 1):
        levels.append(((r1 >> (bit + 1)) == (c1 >> (bit + 1)))
                      & (((r1 >> bit) & 1) == 1) & (((c1 >> bit) & 1) == 0))
    head_bits = N.bit_length() - 1
    same_head = (iota((L, L), 0) >> head_bits) == (iota((L, L), 1) >> head_bits)
    seg = same_head.astype(BF16)
    cum = (iota((C, C), 0) >= iota((C, C), 1)).astype(BF16)

    def block_diag(t):
        return jnp.where(same_head, jnp.concatenate([t] * heads, axis=0), 0.0).astype(BF16)

    def head_sums(t):
        return jnp.concatenate(
            [_dot(t[:, g * L:(g + 1) * L].astype(BF16), seg) for g in range(W // L)], axis=-1)

    roots = []
    for b in range(batch):
        u = feat_ref[b]
        ubuf[b, 8:8 + rows, :] = u
        prev = ubuf[b, 7:7 + rows, :]
        ubuf[b, 0:8, :] = u[rows - 8:rows, :]
        feat = u + (prev - u) * mu_ref[...]
        r, k, v = feat[:, 0:W], feat[:, W:2 * W], feat[:, 2 * W:3 * W]
        w_lo = feat[:, 3 * W:3 * W + RW_RANK]
        a_lo = feat[:, 3 * W + RW_RANK:]
        z = w0_ref[...] + _dot(jnp.tanh(w_lo).astype(BF16), ww2_ref[...])
        logw = -math.exp(-0.5) * jax.nn.sigmoid(z)
        a_pre = a0_ref[...] + _dot(a_lo.astype(BF16), wa2_ref[...])
        kk_raw = k * kk_ref[...]
        norm2 = head_sums(kk_raw * kk_raw)
        cs = [_split_dot(logw[ch * C:(ch + 1) * C], cum, 2, x_is_lhs=False)
              for ch in range(chunks)]
        roots.append((r, k, v, logw, a_pre, kk_raw, norm2, cs))

    prep = {}

    def prepare(ch, anchor=None):
        cr = slice(ch * C, (ch + 1) * C)
        zero = 0.0
        if anchor is not None:
            bits = pltpu.bitcast(anchor[0:1, :], jnp.uint32)
            half_word = jnp.uint32(16)
            bits = lax.shift_right_logical(lax.shift_right_logical(bits, half_word), half_word)
            zero = jnp.concatenate([pltpu.bitcast(bits, F32)] * (W // L), axis=1)
        grs = []
        for b in range(batch):
            r, k, v, logw, a_pre, kk_raw, norm2, cs_all = roots[b]
            r, k, v, logw, cs = r[cr], k[cr], v[cr], logw[cr], cs_all[ch] + zero
            a = jax.nn.sigmoid(a_pre[cr] + zero)
            kk = kk_raw[cr] * lax.rsqrt(jnp.maximum(norm2[cr], 1e-24))
            bb = a * kk
            k2 = k * (1.0 + (a - 1.0) * ka_ref[...])
            cs_last = cs[C - 1:C, :]
            g_all = jnp.exp(cs_last)
            g_inv = jnp.exp(-cs)
            g_end = jnp.exp(cs_last - cs)
            kk_t = (kk * jnp.exp(cs - logw)).astype(BF16)
            r_t = (r * jnp.exp(cs)).astype(BF16)
            k_inv, b_inv = k2 * g_inv, bb * g_inv
            k_end, nb_end = (k2 * g_end).astype(BF16), (-(bb * g_end)).astype(BF16)
            prep[b, ch] = (r, k2, v)
            for g in range(W // L):
                gl = slice(g * L, (g + 1) * L)
                grs.append(dict(
                    b=b, g=g, g_all=g_all[:, gl], v=v[:, gl],
                    lhs=jnp.concatenate([kk_t[:, gl], r_t[:, gl]], axis=0),
                    rhs=jnp.concatenate([block_diag(b_inv[:, gl]), block_diag(k_inv[:, gl])],
                                        axis=0),
                    end=jnp.concatenate([k_end[:, gl], nb_end[:, gl]], axis=0)))
        return grs

    def output_stage(ch, outs):
        cr = slice(ch * C, (ch + 1) * C)
        o = [jnp.concatenate([outs[b, g] for g in range(W // L)], axis=-1) for b in range(batch)]
        mean = [head_sums(t) * (1.0 / N) for t in o]
        yield
        d = [t - m for t, m in zip(o, mean)]
        var = [head_sums(t * t) * (1.0 / N) for t in d]
        bonus = [head_sums(prep[b, ch][0] * prep[b, ch][1] * rk_ref[...]) * prep[b, ch][2]
                 for b in range(batch)]
        yield
        for b in range(batch):
            t = d[b] * lax.rsqrt(var[b] + GN_EPS) * lnw_ref[...] + lnb_ref[...]
            y_ref[b, cr, :] = ((t + bonus[b]) * _silu(gate_ref[b, cr, :].astype(F32))
                               ).astype(y_ref.dtype)

    carried = {(b, g): state[b, g] for b in range(batch) for g in range(W // L)}
    pending = iter(())
    nxt = prepare(0)
    for ch in range(chunks):
        grs = nxt
        for gr in grs:
            m = _dot_nt(gr["lhs"], gr["rhs"])
            gr["a_b"] = jnp.where(tri, m[:, :L], 0.0)
            gr["a_k"] = jnp.where(tri, m[:, L:], 0.0).astype(BF16)
            gr["t"] = eye - jnp.where(levels[0], gr["a_b"][:C], 0.0)
        for gr in grs:
            gr["st"] = carried[gr["b"], gr["g"]]
            gr["x"] = (_dot(gr["a_k"], block_diag(gr["v"]))
                       + _dot_nt(gr["lhs"], block_diag(gr["st"])))
        if ch + 1 < chunks:
            nxt = prepare(ch + 1, anchor=grs[-1]["a_b"])
        for level in levels[1:]:
            for gr in grs:
                off = jnp.where(level, gr["a_b"][:C], 0.0)
                gr["inner"] = _dot(gr["t"].astype(BF16), block_diag(off)).astype(BF16)
            for gr in grs:
                gr["t"] = gr["t"] - _dot(gr["inner"], block_diag(gr["t"]))
            next(pending, None)
        for gr in grs:
            gr["uu"] = _dot(gr["t"].astype(BF16), block_diag(gr["x"][:C]))
        outs = {}
        for gr in grs:
            outs[gr["b"], gr["g"]] = gr["x"][C:] - _dot(gr["a_b"][C:].astype(BF16),
                                                       block_diag(gr["uu"]))
        for gr in grs:
            vu = jnp.concatenate([gr["v"], gr["uu"]], axis=0).astype(BF16)
            delta = jnp.where(same_head, _dot_tn(vu, gr["end"]), 0.0)
            carried[gr["b"], gr["g"]] = gr["st"] * gr["g_all"] + sum(
                delta[h * N:(h + 1) * N] for h in range(heads))
        for _ in pending:
            pass
        pending = output_stage(ch, outs)
    for (b, g), st in carried.items():
        state[b, g] = st
    for _ in pending:
        pass


def _rwkv_branch(feat, gate, mu, w0, w_w2, a0, w_a2, k_k, k_a, r_k, lnx_w, lnx_b):
    batch, seq, _ = feat.shape
    rows = RW_CHUNK * RW_STEP_CHUNKS
    full = lambda shape: pl.BlockSpec(shape, lambda i: (0,) * len(shape))
    vec = lambda t: t.reshape(1, -1)
    return pl.pallas_call(
        _rwkv_kernel,
        grid=(seq // rows,),
        in_specs=[
            pl.BlockSpec((batch, rows, RW_SHIFTED), lambda i: (0, i, 0)),
            pl.BlockSpec((batch, rows, RW_WIDTH), lambda i: (0, i, 0)),
            full((1, RW_SHIFTED)), full((1, RW_WIDTH)), full((RW_RANK, RW_WIDTH)),
            full((1, RW_WIDTH)), full((RW_RANK, RW_WIDTH)),
            full((1, RW_WIDTH)), full((1, RW_WIDTH)), full((1, RW_WIDTH)),
            full((1, RW_WIDTH)), full((1, RW_WIDTH)),
        ],
        out_specs=pl.BlockSpec((batch, rows, RW_WIDTH), lambda i: (0, i, 0)),
        out_shape=jax.ShapeDtypeStruct((batch, seq, RW_WIDTH), BF16),
        scratch_shapes=[
            pltpu.VMEM((batch, rows + 8, RW_SHIFTED), F32),
            pltpu.VMEM((batch, RW_WIDTH // RW_GROUP, RW_HEAD, RW_GROUP), F32),
        ],
        compiler_params=pltpu.CompilerParams(
            dimension_semantics=("arbitrary",),
            vmem_limit_bytes=VMEM_LIMIT),
        name="rwkv7_chunked",
    )(feat, gate, vec(mu), vec(w0), w_w2.astype(BF16), vec(a0), w_a2.astype(BF16),
      vec(k_k), vec(k_a), vec(r_k), vec(lnx_w), vec(lnx_b))


def _attn_kernel(lq1_ref, lk1_ref, lq2_ref, lk2_ref, sw_ref, qt_ref, k_ref, vt_ref,
                 g_ref, o_ref, s_buf, p_buf, acc_buf, *, lambda_init):
    T, HALF = ATT_TILE, ATT_TILE // 2
    tiles = k_ref.shape[1] // T

    comp = lax.broadcasted_iota(jnp.int32, (DA_VDIM, T), 0) < DA_QKDIM
    key = lax.broadcasted_iota(jnp.int32, (HALF, HALF), 0)
    query = lax.broadcasted_iota(jnp.int32, (HALF, HALF), 1)
    visible = (query // CHUNK) >= (key // CHUNK)
    lam = (jnp.exp(jnp.sum(lq1_ref[...] * lk1_ref[...], axis=-1, keepdims=True))
           - jnp.exp(jnp.sum(lq2_ref[...] * lk2_ref[...], axis=-1, keepdims=True))
           + lambda_init)

    for step in range(ATT_STEPS):
        order = [i for k in range(step, tiles // 2, ATT_STEPS) for i in (tiles - 1 - k, k)]
        pl.when(pl.program_id(2) == step)(functools.partial(
            _attn_tasks, [(i, j) for i in order for j in range(i + 1)],
            comp, visible, lam, sw_ref, qt_ref, k_ref, vt_ref, g_ref, o_ref,
            s_buf, p_buf, acc_buf, lambda_init))


def _attn_tasks(tasks, comp, visible, lam, sw_ref, qt_ref, k_ref, vt_ref, g_ref, o_ref,
                s_buf, p_buf, acc_buf, lambda_init):
    T, HALF = ATT_TILE, ATT_TILE // 2
    qt_parts = {}

    def qt_c(i):
        if i not in qt_parts:
            qt = qt_ref[0, 0, i]
            zero = jnp.zeros_like(qt)
            qt_parts[i] = (jnp.where(comp, qt, zero), jnp.where(comp, zero, qt))
        return qt_parts[i]

    def colmax(s):
        return jnp.max(s, axis=0, keepdims=True)

    def scores(n):
        i, j = tasks[n]
        kb = k_ref[0, j * T:(j + 1) * T, :]
        blk_max = []
        for c in range(2):
            q = qt_c(i)[c]
            if j < i:
                s = _dot(kb, q)
                s_buf[n % 2, c] = s
                blk_max.append(colmax(s))
            else:
                s_r = _dot(kb, q[:, HALF:])
                s_b = jnp.where(visible, s_r[HALF:], -jnp.inf)
                s_l = jnp.where(visible, _dot(kb[:HALF], q[:, :HALF]), -jnp.inf)
                s_buf[n % 2, c, :HALF, HALF:] = s_r[:HALF]
                s_buf[n % 2, c, HALF:, HALF:] = s_b
                s_buf[n % 2, c, :HALF, :HALF] = s_l
                blk_max.append(jnp.concatenate(
                    [colmax(s_l), jnp.maximum(colmax(s_r[:HALF]), colmax(s_b))], axis=1))
        return blk_max

    def softmax(n, blk_maxes, maxes):
        i, j = tasks[n]
        new, alphas = [], []
        for c in range(2):
            if j == 0:
                m_new, alpha = blk_maxes[c], None
            else:
                m_new = jnp.maximum(maxes[c], blk_maxes[c])
                alpha = jnp.exp2(maxes[c] - m_new)
            prob = lambda s, m: jnp.exp2((s - m).astype(BF16))
            if j < i:
                p_buf[n % 2, c] = prob(s_buf[n % 2, c], m_new)
            else:
                left, right = m_new[:, :HALF], m_new[:, HALF:]
                p_buf[n % 2, c, :HALF, :HALF] = prob(s_buf[n % 2, c, :HALF, :HALF], left)
                p_buf[n % 2, c, :HALF, HALF:] = prob(s_buf[n % 2, c, :HALF, HALF:], right)
                p_buf[n % 2, c, HALF:, HALF:] = prob(s_buf[n % 2, c, HALF:, HALF:], right)
            new.append(m_new)
            alphas.append(alpha)
        return new, alphas

    ones_rows = (lax.broadcasted_iota(jnp.int32, (ONES_ROWS, T), 0) == 0).astype(BF16)

    def values(n, alphas):
        i, j = tasks[n]
        vtb = jnp.concatenate([vt_ref[0, 0, j], ones_rows], axis=0)
        for c in range(2):
            if j < i:
                pv = _dot(vtb, p_buf[n % 2, c])
            else:
                pv = jnp.concatenate(
                    [_dot(vtb[:, :HALF], p_buf[n % 2, c, :HALF, :HALF]),
                     _dot(vtb, p_buf[n % 2, c, :, HALF:])], axis=1)
            acc_buf[i % 2, c] = pv if j == 0 else alphas[c] * acc_buf[i % 2, c] + pv

    def finish(i):
        num0, num1 = acc_buf[i % 2, 0, :DA_VDIM], acc_buf[i % 2, 1, :DA_VDIM]
        l0 = acc_buf[i % 2, 0, DA_VDIM:DA_VDIM + 1]
        l1 = acc_buf[i % 2, 1, DA_VDIM:DA_VDIM + 1]
        o = (num0 * (1.0 / l0) - num1 * (lam / l1)).T
        o = o * lax.rsqrt(jnp.mean(o * o, axis=-1, keepdims=True) + SUBLN_EPS)
        o = o * sw_ref[...] * (1.0 - lambda_init)
        rows = slice(i * T, (i + 1) * T)
        o_ref[0, rows, :] = (o * _silu(g_ref[0, rows, :].astype(F32))).astype(o_ref.dtype)

    blk_max = scores(0)
    maxes, pending = None, None
    for n, (i, j) in enumerate(tasks):
        next_max = scores(n + 1) if n + 1 < len(tasks) else None
        if pending is not None:
            values(pending[0], pending[1])
            if pending[2]:
                finish(tasks[pending[0]][0])
        maxes, alphas = softmax(n, blk_max, maxes)
        pending = (n, alphas, j == i)
        blk_max = next_max
    values(pending[0], pending[1])
    finish(tasks[-1][0])


def _diff_attention(qt, k, vt, gate, lam_q1, lam_k1, lam_q2, lam_k2, subln_w, lambda_init):
    batch, seq, _ = k.shape
    T = ATT_TILE
    small = lambda n: pl.BlockSpec((1, n), lambda b, h, s: (0, 0))
    rows = pl.BlockSpec((1, seq, DA_VDIM), lambda b, h, s: (b, 0, h))
    cols = pl.BlockSpec((1, 1, seq // T, DA_VDIM, T), lambda b, h, s: (b, h, 0, 0, 0))
    return pl.pallas_call(
        functools.partial(_attn_kernel, lambda_init=lambda_init),
        grid=(batch, DA_HEADS, ATT_STEPS),
        in_specs=[small(DA_QKDIM)] * 4 + [small(DA_VDIM), cols, rows, cols, rows],
        out_specs=rows,
        out_shape=jax.ShapeDtypeStruct((batch, seq, DA_WIDTH), BF16),
        scratch_shapes=[pltpu.VMEM((2, 2, T, T), F32),
                        pltpu.VMEM((2, 2, T, T), BF16),
                        pltpu.VMEM((2, 2, DA_VDIM + ONES_ROWS, T), F32)],
        compiler_params=pltpu.CompilerParams(
            dimension_semantics=("parallel", "parallel", "arbitrary"),
            vmem_limit_bytes=VMEM_LIMIT),
        name="diff_attn",
    )(*(t.reshape(1, -1) for t in (lam_q1, lam_k1, lam_q2, lam_k2, subln_w)),
      qt, k, vt, gate)


def _out_proj_kernel(yr_ref, yd_ref, wr_ref, wd_ref, g_ref, gate_ref, x_ref, o_ref):
    y = _dot(yr_ref[0], wr_ref[...]) + _dot(yd_ref[0], wd_ref[...])
    y = y * lax.rsqrt(jnp.mean(y * y, axis=-1, keepdims=True) + RMS_EPS) * g_ref[...]
    o_ref[0] = x_ref[0] + gate_ref[0] * y


def _out_projection(y_rw, y_da, w_out_bf16, g_post, gate, x):
    batch, seq, _ = x.shape
    tm = OUT_ROWS
    half = lambda j: pl.BlockSpec((RW_WIDTH, D_MODEL), lambda b, i: (j, 0),
                                  pipeline_mode=pl.Buffered(1))
    row_spec = lambda width: pl.BlockSpec((1, tm, width), lambda b, i: (b, i, 0))
    return pl.pallas_call(
        _out_proj_kernel,
        grid=(batch, seq // tm),
        in_specs=[
            row_spec(RW_WIDTH), row_spec(DA_WIDTH), half(0), half(1),
            pl.BlockSpec((1, D_MODEL), lambda b, i: (0, 0)),
            pl.BlockSpec((1, 1, D_MODEL), lambda b, i: (b, 0, 0)),
            row_spec(D_MODEL),
        ],
        out_specs=row_spec(D_MODEL),
        out_shape=jax.ShapeDtypeStruct(x.shape, x.dtype),
        compiler_params=pltpu.CompilerParams(
            dimension_semantics=("parallel", "parallel"),
            vmem_limit_bytes=VMEM_LIMIT),
        name="out_proj",
    )(y_rw, y_da, w_out_bf16, w_out_bf16, g_post, gate, x)


def kernel(x, c, w_ada, b_ada, g_pre, g_post, w_in, w_out, rw_mu, rw_w0, rw_w_w2,
           rw_a0, rw_w_a2, rw_k_k, rw_k_a, rw_r_k, rw_lnx_w, rw_lnx_b,
           da_lam_q1, da_lam_k1, da_lam_q2, da_lam_k2, da_subln_w):
    batch = x.shape[0]
    depth = w_in.shape[0]
    for l in range(depth):
        lambda_init = 0.8 - 0.6 * math.exp(-0.3 * l)
        mod = _modulation(c, w_ada[l], b_ada[l]).reshape(batch, 3, 1, D_MODEL)
        shift, scale, gate = mod[:, 0], mod[:, 1], mod[:, 2]
        feat, rw_gate, dk, dg, dqt, dvt = _in_projection(
            x, shift, scale, g_pre[l].reshape(1, D_MODEL), w_in[l].astype(BF16))
        y_rw = _rwkv_branch(feat, rw_gate, rw_mu[l], rw_w0[l], rw_w_w2[l], rw_a0[l],
                            rw_w_a2[l], rw_k_k[l], rw_k_a[l], rw_r_k[l],
                            rw_lnx_w[l], rw_lnx_b[l])
        y_da = _diff_attention(dqt, dk, dvt, dg, da_lam_q1[l], da_lam_k1[l],
                               da_lam_q2[l], da_lam_k2[l], da_subln_w[l], lambda_init)
        x = _out_projection(y_rw, y_da, w_out[l].astype(BF16),
                            g_post[l].reshape(1, D_MODEL), gate, x)
    return x
```

```python
import functools
import math

import jax
import jax.numpy as jnp
from jax import lax
from jax.experimental import pallas as pl
from jax.experimental.pallas import tpu as pltpu

D_MODEL = 1024
SEQ = 4096
CHUNK = 64
RW_WIDTH = 512
RW_HEAD = 64
RW_HEADS = RW_WIDTH // RW_HEAD
RW_RANK = 64
RW_SHIFTED = 3 * RW_WIDTH + 2 * RW_RANK
RW_COLS = RW_SHIFTED + RW_WIDTH
DA_WIDTH = 512
DA_HEADS = 4
DA_VDIM = DA_WIDTH // DA_HEADS
DA_QKDIM = DA_VDIM // 2
D_IN = RW_COLS + 4 * DA_WIDTH
RMS_EPS = 1e-6
GN_EPS = 64e-5
SUBLN_EPS = 1e-5

IN_ROWS = 512
OUT_ROWS = 1024
RW_CHUNK = 64
RW_STEP_CHUNKS = 2
RW_GROUP = 2 * RW_HEAD
ATT_TILE = 512
ONES_ROWS = 16
ATT_STEPS = 2
VMEM_LIMIT = 48 * 1024 * 1024

F32 = jnp.float32
BF16 = jnp.bfloat16
HIGHEST = lax.Precision.HIGHEST


def _silu(t):
    return t * jax.nn.sigmoid(t)


def _dot(a, b, precision=None):
    return jnp.dot(a, b, preferred_element_type=F32, precision=precision)


def _dot_nt(a, b):
    return lax.dot_general(a, b, (((1,), (1,)), ((), ())), preferred_element_type=F32)


def _dot_tn(a, b):
    return lax.dot_general(a, b, (((0,), (0,)), ((), ())), preferred_element_type=F32)


def _split_dot(x, w_exact, terms, *, x_is_lhs):
    acc = None
    for _ in range(terms):
        piece = x.astype(BF16)
        part = _dot(piece, w_exact) if x_is_lhs else _dot(w_exact, piece)
        acc = part if acc is None else acc + part
        x = x - piece.astype(F32)
    return acc


def _mod_kernel(c_ref, w_ref, b_ref, o_ref):
    o_ref[...] = _dot(_silu(c_ref[...]), w_ref[...], HIGHEST) + b_ref[...]


def _modulation(c, w_ada, b_ada):
    batch = c.shape[0]
    return pl.pallas_call(
        _mod_kernel,
        grid=(3,),
        in_specs=[
            pl.BlockSpec((batch, D_MODEL), lambda j: (0, 0)),
            pl.BlockSpec((D_MODEL, D_MODEL), lambda j: (0, j)),
            pl.BlockSpec((1, D_MODEL), lambda j: (0, j)),
        ],
        out_specs=pl.BlockSpec((batch, D_MODEL), lambda j: (0, j)),
        out_shape=jax.ShapeDtypeStruct((batch, 3 * D_MODEL), F32),
        compiler_params=pltpu.CompilerParams(vmem_limit_bytes=VMEM_LIMIT),
        name="adaln_mod",
    )(c, w_ada, b_ada.reshape(1, 3 * D_MODEL))


_Q_COLS = (RW_COLS, RW_COLS + DA_WIDTH)
_V_COLS = (RW_COLS + 2 * DA_WIDTH, RW_COLS + 3 * DA_WIDTH)
_ROW_SEGMENTS = (
    (0, RW_SHIFTED),
    (RW_SHIFTED, RW_COLS),
    (RW_COLS + DA_WIDTH, RW_COLS + 2 * DA_WIDTH),
    (RW_COLS + 3 * DA_WIDTH, D_IN),
)
_ROW_WIDTHS = tuple(hi - lo for lo, hi in _ROW_SEGMENTS)
_ROW_DTYPES = (F32, BF16, BF16, BF16)
Q_SCALE = math.log2(math.e) / math.sqrt(DA_QKDIM)


def _in_proj_kernel(x_ref, shift_ref, scale_ref, g_ref, w_ref, *rest):
    out_refs, wt_ref = rest[:-1], rest[-1]

    @pl.when((pl.program_id(0) == 0) & (pl.program_id(1) == 0))
    def _():
        wt_ref[0:DA_WIDTH, :] = w_ref[:, _Q_COLS[0]:_Q_COLS[1]].T
        wt_ref[DA_WIDTH:, :] = w_ref[:, _V_COLS[0]:_V_COLS[1]].T

    x = x_ref[0]
    y = x * lax.rsqrt(jnp.mean(x * x, axis=-1, keepdims=True) + RMS_EPS) * g_ref[...]
    h = (y * (1.0 + scale_ref[0]) + shift_ref[0]).astype(BF16)
    for (lo, hi), o_ref in zip(_ROW_SEGMENTS, out_refs):
        o_ref[0] = _dot(h, w_ref[:, lo:hi]).astype(o_ref.dtype)
    qt_ref, vt_ref = out_refs[len(_ROW_SEGMENTS):]
    tm = h.shape[0]
    qt = _dot_nt(wt_ref[0:DA_WIDTH, :], h) * Q_SCALE
    vt = _dot_nt(wt_ref[DA_WIDTH:, :], h)
    qt_ref[0, :, 0] = qt.astype(BF16).reshape(DA_HEADS, DA_VDIM, tm)
    vt_ref[0, :, 0] = vt.astype(BF16).reshape(DA_HEADS, DA_VDIM, tm)


def _in_projection(x, shift, scale, g_pre, w_in_bf16):
    batch, seq, _ = x.shape
    tm = IN_ROWS
    row_spec = lambda width: pl.BlockSpec((1, tm, width), lambda b, i: (b, i, 0))
    vec_spec = pl.BlockSpec((1, 1, D_MODEL), lambda b, i: (b, 0, 0))
    per_tile = ATT_TILE // tm
    t_spec = pl.BlockSpec((1, DA_HEADS, 1, DA_VDIM, tm),
                          lambda b, i: (b, 0, i // per_tile, 0, i % per_tile))
    t_shape = jax.ShapeDtypeStruct((batch, DA_HEADS, seq // ATT_TILE, DA_VDIM, ATT_TILE), BF16)
    return pl.pallas_call(
        _in_proj_kernel,
        grid=(batch, seq // tm),
        in_specs=[
            row_spec(D_MODEL), vec_spec, vec_spec,
            pl.BlockSpec((1, D_MODEL), lambda b, i: (0, 0)),
            pl.BlockSpec((D_MODEL, D_IN), lambda b, i: (0, 0), pipeline_mode=pl.Buffered(1)),
        ],
        out_specs=[row_spec(w) for w in _ROW_WIDTHS] + [t_spec, t_spec],
        out_shape=[jax.ShapeDtypeStruct((batch, seq, w), dt)
                   for w, dt in zip(_ROW_WIDTHS, _ROW_DTYPES)] + [t_shape, t_shape],
        scratch_shapes=[pltpu.VMEM((2 * DA_WIDTH, D_MODEL), BF16)],
        compiler_params=pltpu.CompilerParams(
            dimension_semantics=("arbitrary", "arbitrary"),
            vmem_limit_bytes=VMEM_LIMIT),
        name="in_proj",
    )(x, shift, scale, g_pre, w_in_bf16)


def _rwkv_kernel(feat_ref, gate_ref, mu_ref, w0_ref, ww2_ref, a0_ref, wa2_ref,
                 kk_ref, ka_ref, rk_ref, lnw_ref, lnb_ref, y_ref, ubuf, state):
    C, W, N, L = RW_CHUNK, RW_WIDTH, RW_HEAD, RW_GROUP
    heads = L // N
    batch, rows = feat_ref.shape[0], feat_ref.shape[1]
    chunks = rows // C
    ci = pl.program_id(0)

    @pl.when(ci == 0)
    def _():
        ubuf[:, 0:8, :] = jnp.zeros((batch, 8, RW_SHIFTED), F32)
        state[...] = jnp.zeros(state.shape, F32)

    iota = lambda shape, axis: lax.broadcasted_iota(jnp.int32, shape, axis)
    row2, col2 = iota((2 * C, L), 0), iota((2 * C, L), 1) & (N - 1)
    tri = ((row2 & (C - 1)) > col2) | ((row2 >= C) & ((row2 & (C - 1)) == col2))
    r1, c1 = iota((C, L), 0), iota((C, L), 1) & (N - 1)
    eye = (r1 == c1).astype(F32)
    levels = []
    for bit in range(C.bit_length() - 1):
        levels.append(((r1 >> (bit + 1)) == (c1 >> (bit + 1)))
                      & (((r1 >> bit) & 1) == 1) & (((c1 >> bit) & 1) == 0))
    head_bits = N.bit_length() - 1
    same_head = (iota((L, L), 0) >> head_bits) == (iota((L, L), 1) >> head_bits)
    seg = same_head.astype(BF16)
    cum = (iota((C, C), 0) >= iota((C, C), 1)).astype(BF16)

    def block_diag(t, keep=same_head):
        return jnp.where(keep, jnp.concatenate([t] * heads, axis=0), 0.0).astype(BF16)

    level_keep = [same_head & jnp.concatenate([level] * heads, axis=0) for level in levels]

    def head_sums(t):
        return jnp.concatenate(
            [_dot(t[:, g * L:(g + 1) * L].astype(BF16), seg) for g in range(W // L)], axis=-1)

    roots = []
    for b in range(batch):
        u = feat_ref[b]
        ubuf[b, 8:8 + rows, :] = u
        prev = ubuf[b, 7:7 + rows, :]
        ubuf[b, 0:8, :] = u[rows - 8:rows, :]
        feat = u + (prev - u) * mu_ref[...]
        r, k, v = feat[:, 0:W], feat[:, W:2 * W], feat[:, 2 * W:3 * W]
        w_lo = feat[:, 3 * W:3 * W + RW_RANK]
        a_lo = feat[:, 3 * W + RW_RANK:]
        z = w0_ref[...] + _dot(jnp.tanh(w_lo).astype(BF16), ww2_ref[...])
        logw = -math.exp(-0.5) * jax.nn.sigmoid(z)
        a_pre = a0_ref[...] + _dot(a_lo.astype(BF16), wa2_ref[...])
        kk_raw = k * kk_ref[...]
        norm2 = head_sums(kk_raw * kk_raw)
        cs = [_split_dot(logw[ch * C:(ch + 1) * C], cum, 2, x_is_lhs=False)
              for ch in range(chunks)]
        roots.append((r, k, v, logw, a_pre, kk_raw, norm2, cs))

    prep = {}

    def prepare(ch, anchor=None):
        cr = slice(ch * C, (ch + 1) * C)
        zero = 0.0
        if anchor is not None:
            bits = pltpu.bitcast(anchor[0:1, :], jnp.uint32)
            half_word = jnp.uint32(16)
            bits = lax.shift_right_logical(lax.shift_right_logical(bits, half_word), half_word)
            zero = jnp.concatenate([pltpu.bitcast(bits, F32)] * (W // L), axis=1)
        grs = []
        for b in range(batch):
            r, k, v, logw, a_pre, kk_raw, norm2, cs_all = roots[b]
            r, k, v, logw, cs = r[cr], k[cr], v[cr], logw[cr], cs_all[ch] + zero
            a = jax.nn.sigmoid(a_pre[cr] + zero)
            kk = kk_raw[cr] * lax.rsqrt(jnp.maximum(norm2[cr], 1e-24))
            bb = a * kk
            k2 = k * (1.0 + (a - 1.0) * ka_ref[...])
            cs_last = cs[C - 1:C, :]
            g_all = jnp.exp(cs_last)
            g_inv = jnp.exp(-cs)
            g_end = jnp.exp(cs_last - cs)
            kk_t = (kk * jnp.exp(cs - logw)).astype(BF16)
            r_t = (r * jnp.exp(cs)).astype(BF16)
            k_inv, b_inv = k2 * g_inv, bb * g_inv
            k_end, nb_end = (k2 * g_end).astype(BF16), (-(bb * g_end)).astype(BF16)
            prep[b, ch] = (r, k2, v)
            for g in range(W // L):
                gl = slice(g * L, (g + 1) * L)
                grs.append(dict(
                    b=b, g=g, g_all=g_all[:, gl], v=v[:, gl],
                    lhs=jnp.concatenate([kk_t[:, gl], r_t[:, gl]], axis=0),
                    rhs=jnp.concatenate([block_diag(b_inv[:, gl]), block_diag(k_inv[:, gl])],
                                        axis=0),
                    end=jnp.concatenate([k_end[:, gl], nb_end[:, gl]], axis=0)))
        return grs

    def output_stage(ch, outs):
        cr = slice(ch * C, (ch + 1) * C)
        o = [jnp.concatenate([outs[b, g] for g in range(W // L)], axis=-1) for b in range(batch)]
        mean = [head_sums(t) * (1.0 / N) for t in o]
        yield
        d = [t - m for t, m in zip(o, mean)]
        var = [head_sums(t * t) * (1.0 / N) for t in d]
        bonus = [head_sums(prep[b, ch][0] * prep[b, ch][1] * rk_ref[...]) * prep[b, ch][2]
                 for b in range(batch)]
        yield
        for b in range(batch):
            t = d[b] * lax.rsqrt(var[b] + GN_EPS) * lnw_ref[...] + lnb_ref[...]
            y_ref[b, cr, :] = ((t + bonus[b]) * _silu(gate_ref[b, cr, :].astype(F32))
                               ).astype(y_ref.dtype)

    carried = {(b, g): state[b, g] for b in range(batch) for g in range(W // L)}
    pending = iter(())
    nxt = prepare(0)
    for ch in range(chunks):
        grs = nxt
        for gr in grs:
            m = _dot_nt(gr["lhs"], gr["rhs"])
            gr["a_b"] = jnp.where(tri, m[:, :L], 0.0)
            gr["a_k"] = jnp.where(tri, m[:, L:], 0.0).astype(BF16)
            gr["t"] = eye - jnp.where(levels[0], gr["a_b"][:C], 0.0)
        for gr in grs:
            gr["st"] = carried[gr["b"], gr["g"]]
            gr["x"] = (_dot(gr["a_k"], block_diag(gr["v"]))
                       + _dot_nt(gr["lhs"], block_diag(gr["st"])))
        if ch + 1 < chunks:
            nxt = prepare(ch + 1, anchor=grs[-1]["a_b"])
        for keep in level_keep[1:]:
            for gr in grs:
                gr["inner"] = _dot(gr["t"].astype(BF16),
                                   block_diag(gr["a_b"][:C], keep)).astype(BF16)
            for gr in grs:
                gr["t"] = gr["t"] - _dot(gr["inner"], block_diag(gr["t"]))
            next(pending, None)
        for gr in grs:
            gr["uu"] = _dot(gr["t"].astype(BF16), block_diag(gr["x"][:C]))
        outs = {}
        for gr in grs:
            outs[gr["b"], gr["g"]] = gr["x"][C:] - _dot(gr["a_b"][C:].astype(BF16),
                                                       block_diag(gr["uu"]))
        for gr in grs:
            vu = jnp.concatenate([gr["v"], gr["uu"]], axis=0).astype(BF16)
            delta = jnp.where(same_head, _dot_tn(vu, gr["end"]), 0.0)
            carried[gr["b"], gr["g"]] = gr["st"] * gr["g_all"] + sum(
                delta[h * N:(h + 1) * N] for h in range(heads))
        for _ in pending:
            pass
        pending = output_stage(ch, outs)
    for (b, g), st in carried.items():
        state[b, g] = st
    for _ in pending:
        pass


def _rwkv_branch(feat, gate, mu, w0, w_w2, a0, w_a2, k_k, k_a, r_k, lnx_w, lnx_b):
    batch, seq, _ = feat.shape
    rows = RW_CHUNK * RW_STEP_CHUNKS
    full = lambda shape: pl.BlockSpec(shape, lambda i: (0,) * len(shape))
    vec = lambda t: t.reshape(1, -1)
    return pl.pallas_call(
        _rwkv_kernel,
        grid=(seq // rows,),
        in_specs=[
            pl.BlockSpec((batch, rows, RW_SHIFTED), lambda i: (0, i, 0)),
            pl.BlockSpec((batch, rows, RW_WIDTH), lambda i: (0, i, 0)),
            full((1, RW_SHIFTED)), full((1, RW_WIDTH)), full((RW_RANK, RW_WIDTH)),
            full((1, RW_WIDTH)), full((RW_RANK, RW_WIDTH)),
            full((1, RW_WIDTH)), full((1, RW_WIDTH)), full((1, RW_WIDTH)),
            full((1, RW_WIDTH)), full((1, RW_WIDTH)),
        ],
        out_specs=pl.BlockSpec((batch, rows, RW_WIDTH), lambda i: (0, i, 0)),
        out_shape=jax.ShapeDtypeStruct((batch, seq, RW_WIDTH), BF16),
        scratch_shapes=[
            pltpu.VMEM((batch, rows + 8, RW_SHIFTED), F32),
            pltpu.VMEM((batch, RW_WIDTH // RW_GROUP, RW_HEAD, RW_GROUP), F32),
        ],
        compiler_params=pltpu.CompilerParams(
            dimension_semantics=("arbitrary",),
            vmem_limit_bytes=VMEM_LIMIT),
        name="rwkv7_chunked",
    )(feat, gate, vec(mu), vec(w0), w_w2.astype(BF16), vec(a0), w_a2.astype(BF16),
      vec(k_k), vec(k_a), vec(r_k), vec(lnx_w), vec(lnx_b))


def _attn_kernel(lq1_ref, lk1_ref, lq2_ref, lk2_ref, sw_ref, qt_ref, k_ref, vt_ref,
                 g_ref, o_ref, s_buf, p_buf, acc_buf, *, lambda_init):
    T, HALF = ATT_TILE, ATT_TILE // 2
    tiles = k_ref.shape[1] // T

    comp = lax.broadcasted_iota(jnp.int32, (DA_VDIM, T), 0) < DA_QKDIM
    key = lax.broadcasted_iota(jnp.int32, (HALF, HALF), 0)
    query = lax.broadcasted_iota(jnp.int32, (HALF, HALF), 1)
    visible = (query // CHUNK) >= (key // CHUNK)
    lam = (jnp.exp(jnp.sum(lq1_ref[...] * lk1_ref[...], axis=-1, keepdims=True))
           - jnp.exp(jnp.sum(lq2_ref[...] * lk2_ref[...], axis=-1, keepdims=True))
           + lambda_init)

    for step in range(ATT_STEPS):
        order = [i for k in range(step, tiles // 2, ATT_STEPS) for i in (tiles - 1 - k, k)]
        pl.when(pl.program_id(2) == step)(functools.partial(
            _attn_tasks, [(i, j) for i in order for j in range(i + 1)],
            comp, visible, lam, sw_ref, qt_ref, k_ref, vt_ref, g_ref, o_ref,
            s_buf, p_buf, acc_buf, lambda_init))


def _attn_tasks(tasks, comp, visible, lam, sw_ref, qt_ref, k_ref, vt_ref, g_ref, o_ref,
                s_buf, p_buf, acc_buf, lambda_init):
    T, HALF = ATT_TILE, ATT_TILE // 2
    qt_parts = {}

    def qt_c(i):
        if i not in qt_parts:
            qt = qt_ref[0, 0, i]
            zero = jnp.zeros_like(qt)
            qt_parts[i] = (jnp.where(comp, qt, zero), jnp.where(comp, zero, qt))
        return qt_parts[i]

    def colmax(s):
        return jnp.max(s, axis=0, keepdims=True)

    def scores(n):
        i, j = tasks[n]
        kb = k_ref[0, j * T:(j + 1) * T, :]
        blk_max = []
        for c in range(2):
            q = qt_c(i)[c]
            if j < i:
                s = _dot(kb, q)
                s_buf[n % 2, c] = s
                blk_max.append(colmax(s))
            else:
                s_r = _dot(kb, q[:, HALF:])
                s_b = jnp.where(visible, s_r[HALF:], -jnp.inf)
                s_l = jnp.where(visible, _dot(kb[:HALF], q[:, :HALF]), -jnp.inf)
                s_buf[n % 2, c, :HALF, HALF:] = s_r[:HALF]
                s_buf[n % 2, c, HALF:, HALF:] = s_b
                s_buf[n % 2, c, :HALF, :HALF] = s_l
                blk_max.append(jnp.concatenate(
                    [colmax(s_l), jnp.maximum(colmax(s_r[:HALF]), colmax(s_b))], axis=1))
        return blk_max

    def softmax(n, blk_maxes, maxes):
        i, j = tasks[n]
        new, alphas = [], []
        for c in range(2):
            if j == 0:
                m_new, alpha = blk_maxes[c], None
            else:
                m_new = jnp.maximum(maxes[c], blk_maxes[c])
                alpha = jnp.exp2(maxes[c] - m_new)
            prob = lambda s, m: jnp.exp2((s - m).astype(BF16))
            if j < i:
                p_buf[n % 2, c] = prob(s_buf[n % 2, c], m_new)
            else:
                left, right = m_new[:, :HALF], m_new[:, HALF:]
                p_buf[n % 2, c, :HALF, :HALF] = prob(s_buf[n % 2, c, :HALF, :HALF], left)
                p_buf[n % 2, c, :HALF, HALF:] = prob(s_buf[n % 2, c, :HALF, HALF:], right)
                p_buf[n % 2, c, HALF:, HALF:] = prob(s_buf[n % 2, c, HALF:, HALF:], right)
            new.append(m_new)
            alphas.append(alpha)
        return new, alphas

    ones_rows = (lax.broadcasted_iota(jnp.int32, (ONES_ROWS, T), 0) == 0).astype(BF16)

    def values(n, alphas):
        i, j = tasks[n]
        vtb = jnp.concatenate([vt_ref[0, 0, j], ones_rows], axis=0)
        for c in range(2):
            if j < i:
                pv = _dot(vtb, p_buf[n % 2, c])
            else:
                pv = jnp.concatenate(
                    [_dot(vtb[:, :HALF], p_buf[n % 2, c, :HALF, :HALF]),
                     _dot(vtb, p_buf[n % 2, c, :, HALF:])], axis=1)
            acc_buf[i % 2, c] = pv if j == 0 else alphas[c] * acc_buf[i % 2, c] + pv

    def finish(i):
        num0, num1 = acc_buf[i % 2, 0, :DA_VDIM], acc_buf[i % 2, 1, :DA_VDIM]
        l0 = acc_buf[i % 2, 0, DA_VDIM:DA_VDIM + 1]
        l1 = acc_buf[i % 2, 1, DA_VDIM:DA_VDIM + 1]
        o = (num0 * (1.0 / l0) - num1 * (lam / l1)).T
        o = o * lax.rsqrt(jnp.mean(o * o, axis=-1, keepdims=True) + SUBLN_EPS)
        o = o * sw_ref[...] * (1.0 - lambda_init)
        rows = slice(i * T, (i + 1) * T)
        o_ref[0, rows, :] = (o * _silu(g_ref[0, rows, :].astype(F32))).astype(o_ref.dtype)

    blk_max = scores(0)
    maxes, pending = None, None
    for n, (i, j) in enumerate(tasks):
        next_max = scores(n + 1) if n + 1 < len(tasks) else None
        if pending is not None:
            values(pending[0], pending[1])
            if pending[2]:
                finish(tasks[pending[0]][0])
        maxes, alphas = softmax(n, blk_max, maxes)
        pending = (n, alphas, j == i)
        blk_max = next_max
    values(pending[0], pending[1])
    finish(tasks[-1][0])


def _diff_attention(qt, k, vt, gate, lam_q1, lam_k1, lam_q2, lam_k2, subln_w, lambda_init):
    batch, seq, _ = k.shape
    T = ATT_TILE
    small = lambda n: pl.BlockSpec((1, n), lambda b, h, s: (0, 0))
    rows = pl.BlockSpec((1, seq, DA_VDIM), lambda b, h, s: (b, 0, h))
    cols = pl.BlockSpec((1, 1, seq // T, DA_VDIM, T), lambda b, h, s: (b, h, 0, 0, 0))
    return pl.pallas_call(
        functools.partial(_attn_kernel, lambda_init=lambda_init),
        grid=(batch, DA_HEADS, ATT_STEPS),
        in_specs=[small(DA_QKDIM)] * 4 + [small(DA_VDIM), cols, rows, cols, rows],
        out_specs=rows,
        out_shape=jax.ShapeDtypeStruct((batch, seq, DA_WIDTH), BF16),
        scratch_shapes=[pltpu.VMEM((2, 2, T, T), F32),
                        pltpu.VMEM((2, 2, T, T), BF16),
                        pltpu.VMEM((2, 2, DA_VDIM + ONES_ROWS, T), F32)],
        compiler_params=pltpu.CompilerParams(
            dimension_semantics=("parallel", "parallel", "arbitrary"),
            vmem_limit_bytes=VMEM_LIMIT),
        name="diff_attn",
    )(*(t.reshape(1, -1) for t in (lam_q1, lam_k1, lam_q2, lam_k2, subln_w)),
      qt, k, vt, gate)


def _out_proj_kernel(yr_ref, yd_ref, wr_ref, wd_ref, g_ref, gate_ref, x_ref, o_ref):
    y = _dot(yr_ref[0], wr_ref[...]) + _dot(yd_ref[0], wd_ref[...])
    y = y * lax.rsqrt(jnp.mean(y * y, axis=-1, keepdims=True) + RMS_EPS) * g_ref[...]
    o_ref[0] = x_ref[0] + gate_ref[0] * y


def _out_projection(y_rw, y_da, w_out_bf16, g_post, gate, x):
    batch, seq, _ = x.shape
    tm = OUT_ROWS
    half = lambda j: pl.BlockSpec((RW_WIDTH, D_MODEL), lambda b, i: (j, 0),
                                  pipeline_mode=pl.Buffered(1))
    row_spec = lambda width: pl.BlockSpec((1, tm, width), lambda b, i: (b, i, 0))
    return pl.pallas_call(
        _out_proj_kernel,
        grid=(batch, seq // tm),
        in_specs=[
            row_spec(RW_WIDTH), row_spec(DA_WIDTH), half(0), half(1),
            pl.BlockSpec((1, D_MODEL), lambda b, i: (0, 0)),
            pl.BlockSpec((1, 1, D_MODEL), lambda b, i: (b, 0, 0)),
            row_spec(D_MODEL),
        ],
        out_specs=row_spec(D_MODEL),
        out_shape=jax.ShapeDtypeStruct(x.shape, x.dtype),
        compiler_params=pltpu.CompilerParams(
            dimension_semantics=("parallel", "parallel"),
            vmem_limit_bytes=VMEM_LIMIT),
        name="out_proj",
    )(y_rw, y_da, w_out_bf16, w_out_bf16, g_post, gate, x)


def kernel(x, c, w_ada, b_ada, g_pre, g_post, w_in, w_out, rw_mu, rw_w0, rw_w_w2,
           rw_a0, rw_w_a2, rw_k_k, rw_k_a, rw_r_k, rw_lnx_w, rw_lnx_b,
           da_lam_q1, da_lam_k1, da_lam_q2, da_lam_k2, da_subln_w):
    batch = x.shape[0]
    depth = w_in.shape[0]
    for l in range(depth):
        lambda_init = 0.8 - 0.6 * math.exp(-0.3 * l)
        mod = _modulation(c, w_ada[l], b_ada[l]).reshape(batch, 3, 1, D_MODEL)
        shift, scale, gate = mod[:, 0], mod[:, 1], mod[:, 2]
        feat, rw_gate, dk, dg, dqt, dvt = _in_projection(
            x, shift, scale, g_pre[l].reshape(1, D_MODEL), w_in[l].astype(BF16))
        y_rw = _rwkv_branch(feat, rw_gate, rw_mu[l], rw_w0[l], rw_w_w2[l], rw_a0[l],
                            rw_w_a2[l], rw_k_k[l], rw_k_a[l], rw_r_k[l],
                            rw_lnx_w[l], rw_lnx_b[l])
        y_da = _diff_attention(dqt, dk, dvt, dg, da_lam_q1[l], da_lam_k1[l],
                               da_lam_q2[l], da_lam_k2[l], da_subln_w[l], lambda_init)
        x = _out_projection(y_rw, y_da, w_out[l].astype(BF16),
                            g_post[l].reshape(1, D_MODEL), gate, x)
    return x
```

```python
import functools
import math

import jax
import jax.numpy as jnp
from jax import lax
from jax.experimental import pallas as pl
from jax.experimental.pallas import tpu as pltpu

D_MODEL = 1024
SEQ = 4096
CHUNK = 64
RW_WIDTH = 512
RW_HEAD = 64
RW_HEADS = RW_WIDTH // RW_HEAD
RW_RANK = 64
RW_SHIFTED = 3 * RW_WIDTH + 2 * RW_RANK
RW_COLS = RW_SHIFTED + RW_WIDTH
DA_WIDTH = 512
DA_HEADS = 4
DA_VDIM = DA_WIDTH // DA_HEADS
DA_QKDIM = DA_VDIM // 2
D_IN = RW_COLS + 4 * DA_WIDTH
RMS_EPS = 1e-6
GN_EPS = 64e-5
SUBLN_EPS = 1e-5

IN_ROWS = 512
OUT_ROWS = 1024
RW_CHUNK = 64
RW_STEP_CHUNKS = 2
RW_GROUP = 2 * RW_HEAD
ATT_TILE = 512
ONES_ROWS = 16
ATT_STEPS = 2
VMEM_LIMIT = 48 * 1024 * 1024

F32 = jnp.float32
BF16 = jnp.bfloat16
HIGHEST = lax.Precision.HIGHEST


def _silu(t):
    return t * jax.nn.sigmoid(t)


def _dot(a, b, precision=None):
    return jnp.dot(a, b, preferred_element_type=F32, precision=precision)


def _dot_nt(a, b):
    return lax.dot_general(a, b, (((1,), (1,)), ((), ())), preferred_element_type=F32)


def _dot_tn(a, b):
    return lax.dot_general(a, b, (((0,), (0,)), ((), ())), preferred_element_type=F32)


def _split_dot(x, w_exact, terms, *, x_is_lhs):
    acc = None
    for _ in range(terms):
        piece = x.astype(BF16)
        part = _dot(piece, w_exact) if x_is_lhs else _dot(w_exact, piece)
        acc = part if acc is None else acc + part
        x = x - piece.astype(F32)
    return acc


def _mod_kernel(c_ref, w_ref, b_ref, o_ref):
    o_ref[...] = _dot(_silu(c_ref[...]), w_ref[...], HIGHEST) + b_ref[...]


def _modulation(c, w_ada, b_ada):
    batch = c.shape[0]
    return pl.pallas_call(
        _mod_kernel,
        grid=(3,),
        in_specs=[
            pl.BlockSpec((batch, D_MODEL), lambda j: (0, 0)),
            pl.BlockSpec((D_MODEL, D_MODEL), lambda j: (0, j)),
            pl.BlockSpec((1, D_MODEL), lambda j: (0, j)),
        ],
        out_specs=pl.BlockSpec((batch, D_MODEL), lambda j: (0, j)),
        out_shape=jax.ShapeDtypeStruct((batch, 3 * D_MODEL), F32),
        compiler_params=pltpu.CompilerParams(vmem_limit_bytes=VMEM_LIMIT),
        name="adaln_mod",
    )(c, w_ada, b_ada.reshape(1, 3 * D_MODEL))


_Q_COLS = (RW_COLS, RW_COLS + DA_WIDTH)
_V_COLS = (RW_COLS + 2 * DA_WIDTH, RW_COLS + 3 * DA_WIDTH)
_ROW_SEGMENTS = (
    (0, RW_SHIFTED),
    (RW_SHIFTED, RW_COLS),
    (RW_COLS + DA_WIDTH, RW_COLS + 2 * DA_WIDTH),
    (RW_COLS + 3 * DA_WIDTH, D_IN),
)
_ROW_WIDTHS = tuple(hi - lo for lo, hi in _ROW_SEGMENTS)
_ROW_DTYPES = (F32, BF16, BF16, BF16)
Q_SCALE = math.log2(math.e) / math.sqrt(DA_QKDIM)


def _in_proj_kernel(x_ref, shift_ref, scale_ref, g_ref, mu_ref, w_ref, *rest):
    out_refs, wt_ref, ubuf = rest[:-2], rest[-2], rest[-1]
    tm = x_ref.shape[1]

    @pl.when((pl.program_id(0) == 0) & (pl.program_id(1) == 0))
    def _():
        wt_ref[0:DA_WIDTH, :] = w_ref[:, _Q_COLS[0]:_Q_COLS[1]].T
        wt_ref[DA_WIDTH:, :] = w_ref[:, _V_COLS[0]:_V_COLS[1]].T

    @pl.when(pl.program_id(1) == 0)
    def _():
        ubuf[0:8, :] = jnp.zeros((8, RW_SHIFTED), F32)

    x = x_ref[0]
    y = x * lax.rsqrt(jnp.mean(x * x, axis=-1, keepdims=True) + RMS_EPS) * g_ref[...]
    h = (y * (1.0 + scale_ref[0]) + shift_ref[0]).astype(BF16)
    for (lo, hi), o_ref in zip(_ROW_SEGMENTS, out_refs):
        u = _dot(h, w_ref[:, lo:hi])
        if lo == 0:
            ubuf[8:8 + tm, :] = u
            prev = ubuf[7:7 + tm, :]
            ubuf[0:8, :] = u[tm - 8:tm, :]
            u = u + (prev - u) * mu_ref[...]
        o_ref[0] = u.astype(o_ref.dtype)
    qt_ref, vt_ref = out_refs[len(_ROW_SEGMENTS):]
    qt = _dot_nt(wt_ref[0:DA_WIDTH, :], h) * Q_SCALE
    vt = _dot_nt(wt_ref[DA_WIDTH:, :], h)
    qt_ref[0, :, 0] = qt.astype(BF16).reshape(DA_HEADS, DA_VDIM, tm)
    vt_ref[0, :, 0] = vt.astype(BF16).reshape(DA_HEADS, DA_VDIM, tm)


def _in_projection(x, shift, scale, g_pre, mu, w_in_bf16):
    batch, seq, _ = x.shape
    tm = IN_ROWS
    row_spec = lambda width: pl.BlockSpec((1, tm, width), lambda b, i: (b, i, 0))
    vec_spec = pl.BlockSpec((1, 1, D_MODEL), lambda b, i: (b, 0, 0))
    per_tile = ATT_TILE // tm
    t_spec = pl.BlockSpec((1, DA_HEADS, 1, DA_VDIM, tm),
                          lambda b, i: (b, 0, i // per_tile, 0, i % per_tile))
    t_shape = jax.ShapeDtypeStruct((batch, DA_HEADS, seq // ATT_TILE, DA_VDIM, ATT_TILE), BF16)
    return pl.pallas_call(
        _in_proj_kernel,
        grid=(batch, seq // tm),
        in_specs=[
            row_spec(D_MODEL), vec_spec, vec_spec,
            pl.BlockSpec((1, D_MODEL), lambda b, i: (0, 0)),
            pl.BlockSpec((1, RW_SHIFTED), lambda b, i: (0, 0)),
            pl.BlockSpec((D_MODEL, D_IN), lambda b, i: (0, 0), pipeline_mode=pl.Buffered(1)),
        ],
        out_specs=[row_spec(w) for w in _ROW_WIDTHS] + [t_spec, t_spec],
        out_shape=[jax.ShapeDtypeStruct((batch, seq, w), dt)
                   for w, dt in zip(_ROW_WIDTHS, _ROW_DTYPES)] + [t_shape, t_shape],
        scratch_shapes=[pltpu.VMEM((2 * DA_WIDTH, D_MODEL), BF16),
                        pltpu.VMEM((tm + 8, RW_SHIFTED), F32)],
        compiler_params=pltpu.CompilerParams(
            dimension_semantics=("arbitrary", "arbitrary"),
            vmem_limit_bytes=VMEM_LIMIT),
        name="in_proj",
    )(x, shift, scale, g_pre, mu, w_in_bf16)


def _rwkv_kernel(feat_ref, gate_ref, w0_ref, ww2_ref, a0_ref, wa2_ref,
                 kk_ref, ka_ref, rk_ref, lnw_ref, lnb_ref, y_ref, state):
    C, W, N, L = RW_CHUNK, RW_WIDTH, RW_HEAD, RW_GROUP
    heads = L // N
    batch, rows = feat_ref.shape[0], feat_ref.shape[1]
    chunks = rows // C
    ci = pl.program_id(0)

    @pl.when(ci == 0)
    def _():
        state[...] = jnp.zeros(state.shape, F32)

    iota = lambda shape, axis: lax.broadcasted_iota(jnp.int32, shape, axis)
    row2, col2 = iota((2 * C, L), 0), iota((2 * C, L), 1) & (N - 1)
    tri = ((row2 & (C - 1)) > col2) | ((row2 >= C) & ((row2 & (C - 1)) == col2))
    r1, c1 = iota((C, L), 0), iota((C, L), 1) & (N - 1)
    eye = (r1 == c1).astype(F32)
    levels = []
    for bit in range(C.bit_length() - 1):
        levels.append(((r1 >> (bit + 1)) == (c1 >> (bit + 1)))
                      & (((r1 >> bit) & 1) == 1) & (((c1 >> bit) & 1) == 0))
    head_bits = N.bit_length() - 1
    same_head = (iota((L, L), 0) >> head_bits) == (iota((L, L), 1) >> head_bits)
    seg = same_head.astype(BF16)
    cum = (iota((C, C), 0) >= iota((C, C), 1)).astype(BF16)

    def block_diag(t, keep=same_head):
        return jnp.where(keep, jnp.concatenate([t] * heads, axis=0), 0.0).astype(BF16)

    level_keep = [same_head & jnp.concatenate([level] * heads, axis=0) for level in levels]

    def head_sums(t):
        return jnp.concatenate(
            [_dot(t[:, g * L:(g + 1) * L].astype(BF16), seg) for g in range(W // L)], axis=-1)

    roots = []
    for b in range(batch):
        feat = feat_ref[b]
        r, k, v = feat[:, 0:W], feat[:, W:2 * W], feat[:, 2 * W:3 * W]
        w_lo = feat[:, 3 * W:3 * W + RW_RANK]
        a_lo = feat[:, 3 * W + RW_RANK:]
        z = w0_ref[...] + _dot(jnp.tanh(w_lo).astype(BF16), ww2_ref[...])
        logw = -math.exp(-0.5) * jax.nn.sigmoid(z)
        a_pre = a0_ref[...] + _dot(a_lo.astype(BF16), wa2_ref[...])
        kk_raw = k * kk_ref[...]
        norm2 = head_sums(kk_raw * kk_raw)
        cs = [_split_dot(logw[ch * C:(ch + 1) * C], cum, 2, x_is_lhs=False)
              for ch in range(chunks)]
        roots.append((r, k, v, logw, a_pre, kk_raw, norm2, cs))

    prep = {}

    def prepare(ch, anchor=None):
        cr = slice(ch * C, (ch + 1) * C)
        zero = 0.0
        if anchor is not None:
            bits = pltpu.bitcast(anchor[0:1, :], jnp.uint32)
            half_word = jnp.uint32(16)
            bits = lax.shift_right_logical(lax.shift_right_logical(bits, half_word), half_word)
            zero = jnp.concatenate([pltpu.bitcast(bits, F32)] * (W // L), axis=1)
        grs = []
        for b in range(batch):
            r, k, v, logw, a_pre, kk_raw, norm2, cs_all = roots[b]
            r, k, v, logw, cs = r[cr], k[cr], v[cr], logw[cr], cs_all[ch] + zero
            a = jax.nn.sigmoid(a_pre[cr] + zero)
            kk = kk_raw[cr] * lax.rsqrt(jnp.maximum(norm2[cr], 1e-24))
            bb = a * kk
            k2 = k * (1.0 + (a - 1.0) * ka_ref[...])
            cs_last = cs[C - 1:C, :]
            g_all = jnp.exp(cs_last)
            g_inv = jnp.exp(-cs)
            g_end = jnp.exp(cs_last - cs)
            kk_t = (kk * jnp.exp(cs - logw)).astype(BF16)
            r_t = (r * jnp.exp(cs)).astype(BF16)
            k_inv, b_inv = k2 * g_inv, bb * g_inv
            k_end, nb_end = (k2 * g_end).astype(BF16), (-(bb * g_end)).astype(BF16)
            prep[b, ch] = (r, k2, v)
            for g in range(W // L):
                gl = slice(g * L, (g + 1) * L)
                grs.append(dict(
                    b=b, g=g, g_all=g_all[:, gl], v=v[:, gl],
                    lhs=jnp.concatenate([kk_t[:, gl], r_t[:, gl]], axis=0),
                    rhs=jnp.concatenate([block_diag(b_inv[:, gl]), block_diag(k_inv[:, gl])],
                                        axis=0),
                    end=jnp.concatenate([k_end[:, gl], nb_end[:, gl]], axis=0)))
        return grs

    def output_stage(ch, outs):
        cr = slice(ch * C, (ch + 1) * C)
        o = [jnp.concatenate([outs[b, g] for g in range(W // L)], axis=-1) for b in range(batch)]
        mean = [head_sums(t) * (1.0 / N) for t in o]
        yield
        d = [t - m for t, m in zip(o, mean)]
        var = [head_sums(t * t) * (1.0 / N) for t in d]
        bonus = [head_sums(prep[b, ch][0] * prep[b, ch][1] * rk_ref[...]) * prep[b, ch][2]
                 for b in range(batch)]
        yield
        for b in range(batch):
            t = d[b] * lax.rsqrt(var[b] + GN_EPS) * lnw_ref[...] + lnb_ref[...]
            y_ref[b, cr, :] = ((t + bonus[b]) * _silu(gate_ref[b, cr, :].astype(F32))
                               ).astype(y_ref.dtype)

    carried = {(b, g): state[b, g] for b in range(batch) for g in range(W // L)}
    pending = iter(())
    nxt = prepare(0)
    for ch in range(chunks):
        grs = nxt
        for gr in grs:
            m = _dot_nt(gr["lhs"], gr["rhs"])
            gr["a_b"] = jnp.where(tri, m[:, :L], 0.0)
            gr["a_k"] = jnp.where(tri, m[:, L:], 0.0).astype(BF16)
            gr["t"] = eye - jnp.where(levels[0], gr["a_b"][:C], 0.0)
        for gr in grs:
            gr["st"] = carried[gr["b"], gr["g"]]
            gr["x"] = (_dot(gr["a_k"], block_diag(gr["v"]))
                       + _dot_nt(gr["lhs"], block_diag(gr["st"])))
        if ch + 1 < chunks:
            nxt = prepare(ch + 1, anchor=grs[-1]["a_b"])
        for keep in level_keep[1:]:
            for gr in grs:
                gr["inner"] = _dot(gr["t"].astype(BF16),
                                   block_diag(gr["a_b"][:C], keep)).astype(BF16)
            for gr in grs:
                gr["t"] = gr["t"] - _dot(gr["inner"], block_diag(gr["t"]))
            next(pending, None)
        for gr in grs:
            gr["uu"] = _dot(gr["t"].astype(BF16), block_diag(gr["x"][:C]))
        outs = {}
        for gr in grs:
            outs[gr["b"], gr["g"]] = gr["x"][C:] - _dot(gr["a_b"][C:].astype(BF16),
                                                       block_diag(gr["uu"]))
        for gr in grs:
            vu = jnp.concatenate([gr["v"], gr["uu"]], axis=0).astype(BF16)
            delta = jnp.where(same_head, _dot_tn(vu, gr["end"]), 0.0)
            carried[gr["b"], gr["g"]] = gr["st"] * gr["g_all"] + sum(
                delta[h * N:(h + 1) * N] for h in range(heads))
        for _ in pending:
            pass
        pending = output_stage(ch, outs)
    for (b, g), st in carried.items():
        state[b, g] = st
    for _ in pending:
        pass


def _rwkv_branch(feat, gate, w0, w_w2, a0, w_a2, k_k, k_a, r_k, lnx_w, lnx_b):
    batch, seq, _ = feat.shape
    rows = RW_CHUNK * RW_STEP_CHUNKS
    full = lambda shape: pl.BlockSpec(shape, lambda i: (0,) * len(shape))
    vec = lambda t: t.reshape(1, -1)
    return pl.pallas_call(
        _rwkv_kernel,
        grid=(seq // rows,),
        in_specs=[
            pl.BlockSpec((batch, rows, RW_SHIFTED), lambda i: (0, i, 0)),
            pl.BlockSpec((batch, rows, RW_WIDTH), lambda i: (0, i, 0)),
            full((1, RW_WIDTH)), full((RW_RANK, RW_WIDTH)),
            full((1, RW_WIDTH)), full((RW_RANK, RW_WIDTH)),
            full((1, RW_WIDTH)), full((1, RW_WIDTH)), full((1, RW_WIDTH)),
            full((1, RW_WIDTH)), full((1, RW_WIDTH)),
        ],
        out_specs=pl.BlockSpec((batch, rows, RW_WIDTH), lambda i: (0, i, 0)),
        out_shape=jax.ShapeDtypeStruct((batch, seq, RW_WIDTH), BF16),
        scratch_shapes=[
            pltpu.VMEM((batch, RW_WIDTH // RW_GROUP, RW_HEAD, RW_GROUP), F32),
        ],
        compiler_params=pltpu.CompilerParams(
            dimension_semantics=("arbitrary",),
            vmem_limit_bytes=VMEM_LIMIT),
        name="rwkv7_chunked",
    )(feat, gate, vec(w0), w_w2.astype(BF16), vec(a0), w_a2.astype(BF16),
      vec(k_k), vec(k_a), vec(r_k), vec(lnx_w), vec(lnx_b))


def _attn_kernel(lq1_ref, lk1_ref, lq2_ref, lk2_ref, sw_ref, qt_ref, k_ref, vt_ref,
                 g_ref, o_ref, s_buf, p_buf, acc_buf, *, lambda_init):
    T, HALF = ATT_TILE, ATT_TILE // 2
    tiles = k_ref.shape[1] // T

    comp = lax.broadcasted_iota(jnp.int32, (DA_VDIM, T), 0) < DA_QKDIM
    key = lax.broadcasted_iota(jnp.int32, (HALF, HALF), 0)
    query = lax.broadcasted_iota(jnp.int32, (HALF, HALF), 1)
    visible = (query // CHUNK) >= (key // CHUNK)
    lam = (jnp.exp(jnp.sum(lq1_ref[...] * lk1_ref[...], axis=-1, keepdims=True))
           - jnp.exp(jnp.sum(lq2_ref[...] * lk2_ref[...], axis=-1, keepdims=True))
           + lambda_init)

    for step in range(ATT_STEPS):
        order = [i for k in range(step, tiles // 2, ATT_STEPS) for i in (tiles - 1 - k, k)]
        pl.when(pl.program_id(2) == step)(functools.partial(
            _attn_tasks, [(i, j) for i in order for j in range(i + 1)],
            comp, visible, lam, sw_ref, qt_ref, k_ref, vt_ref, g_ref, o_ref,
            s_buf, p_buf, acc_buf, lambda_init))


def _attn_tasks(tasks, comp, visible, lam, sw_ref, qt_ref, k_ref, vt_ref, g_ref, o_ref,
                s_buf, p_buf, acc_buf, lambda_init):
    T, HALF = ATT_TILE, ATT_TILE // 2
    qt_parts = {}

    def qt_c(i):
        if i not in qt_parts:
            qt = qt_ref[0, 0, i]
            zero = jnp.zeros_like(qt)
            qt_parts[i] = (jnp.where(comp, qt, zero), jnp.where(comp, zero, qt))
        return qt_parts[i]

    def colmax(s):
        return jnp.max(s, axis=0, keepdims=True)

    def scores(n):
        i, j = tasks[n]
        kb = k_ref[0, j * T:(j + 1) * T, :]
        blk_max = []
        for c in range(2):
            q = qt_c(i)[c]
            if j < i:
                s = _dot(kb, q)
                s_buf[n % 2, c] = s
                blk_max.append(colmax(s))
            else:
                s_r = _dot(kb, q[:, HALF:])
                s_b = jnp.where(visible, s_r[HALF:], -jnp.inf)
                s_l = jnp.where(visible, _dot(kb[:HALF], q[:, :HALF]), -jnp.inf)
                s_buf[n % 2, c, :HALF, HALF:] = s_r[:HALF]
                s_buf[n % 2, c, HALF:, HALF:] = s_b
                s_buf[n % 2, c, :HALF, :HALF] = s_l
                blk_max.append(jnp.concatenate(
                    [colmax(s_l), jnp.maximum(colmax(s_r[:HALF]), colmax(s_b))], axis=1))
        return blk_max

    def softmax(n, blk_maxes, maxes):
        i, j = tasks[n]
        new, alphas = [], []
        for c in range(2):
            if j == 0:
                m_new, alpha = blk_maxes[c], None
            else:
                m_new = jnp.maximum(maxes[c], blk_maxes[c])
                alpha = jnp.exp2(maxes[c] - m_new)
            prob = lambda s, m: jnp.exp2((s - m).astype(BF16))
            if j < i:
                p_buf[n % 2, c] = prob(s_buf[n % 2, c], m_new)
            else:
                left, right = m_new[:, :HALF], m_new[:, HALF:]
                p_buf[n % 2, c, :HALF, :HALF] = prob(s_buf[n % 2, c, :HALF, :HALF], left)
                p_buf[n % 2, c, :HALF, HALF:] = prob(s_buf[n % 2, c, :HALF, HALF:], right)
                p_buf[n % 2, c, HALF:, HALF:] = prob(s_buf[n % 2, c, HALF:, HALF:], right)
            new.append(m_new)
            alphas.append(alpha)
        return new, alphas

    ones_rows = (lax.broadcasted_iota(jnp.int32, (ONES_ROWS, T), 0) == 0).astype(BF16)

    def values(n, alphas):
        i, j = tasks[n]
        vtb = jnp.concatenate([vt_ref[0, 0, j], ones_rows], axis=0)
        for c in range(2):
            if j < i:
                pv = _dot(vtb, p_buf[n % 2, c])
            else:
                pv = jnp.concatenate(
                    [_dot(vtb[:, :HALF], p_buf[n % 2, c, :HALF, :HALF]),
                     _dot(vtb, p_buf[n % 2, c, :, HALF:])], axis=1)
            acc_buf[i % 2, c] = pv if j == 0 else alphas[c] * acc_buf[i % 2, c] + pv

    def finish(i):
        num0, num1 = acc_buf[i % 2, 0, :DA_VDIM], acc_buf[i % 2, 1, :DA_VDIM]
        l0 = acc_buf[i % 2, 0, DA_VDIM:DA_VDIM + 1]
        l1 = acc_buf[i % 2, 1, DA_VDIM:DA_VDIM + 1]
        o = (num0 * (1.0 / l0) - num1 * (lam / l1)).T
        o = o * lax.rsqrt(jnp.mean(o * o, axis=-1, keepdims=True) + SUBLN_EPS)
        o = o * sw_ref[...] * (1.0 - lambda_init)
        rows = slice(i * T, (i + 1) * T)
        o_ref[0, rows, :] = (o * _silu(g_ref[0, rows, :].astype(F32))).astype(o_ref.dtype)

    blk_max = scores(0)
    maxes, pending = None, None
    for n, (i, j) in enumerate(tasks):
        next_max = scores(n + 1) if n + 1 < len(tasks) else None
        if pending is not None:
            values(pending[0], pending[1])
            if pending[2]:
                finish(tasks[pending[0]][0])
        maxes, alphas = softmax(n, blk_max, maxes)
        pending = (n, alphas, j == i)
        blk_max = next_max
    values(pending[0], pending[1])
    finish(tasks[-1][0])


def _diff_attention(qt, k, vt, gate, lam_q1, lam_k1, lam_q2, lam_k2, subln_w, lambda_init):
    batch, seq, _ = k.shape
    T = ATT_TILE
    small = lambda n: pl.BlockSpec((1, n), lambda b, h, s: (0, 0))
    rows = pl.BlockSpec((1, seq, DA_VDIM), lambda b, h, s: (b, 0, h))
    cols = pl.BlockSpec((1, 1, seq // T, DA_VDIM, T), lambda b, h, s: (b, h, 0, 0, 0))
    return pl.pallas_call(
        functools.partial(_attn_kernel, lambda_init=lambda_init),
        grid=(batch, DA_HEADS, ATT_STEPS),
        in_specs=[small(DA_QKDIM)] * 4 + [small(DA_VDIM), cols, rows, cols, rows],
        out_specs=rows,
        out_shape=jax.ShapeDtypeStruct((batch, seq, DA_WIDTH), BF16),
        scratch_shapes=[pltpu.VMEM((2, 2, T, T), F32),
                        pltpu.VMEM((2, 2, T, T), BF16),
                        pltpu.VMEM((2, 2, DA_VDIM + ONES_ROWS, T), F32)],
        compiler_params=pltpu.CompilerParams(
            dimension_semantics=("parallel", "parallel", "arbitrary"),
            vmem_limit_bytes=VMEM_LIMIT),
        name="diff_attn",
    )(*(t.reshape(1, -1) for t in (lam_q1, lam_k1, lam_q2, lam_k2, subln_w)),
      qt, k, vt, gate)


def _out_proj_kernel(yr_ref, yd_ref, wr_ref, wd_ref, g_ref, gate_ref, x_ref, o_ref):
    y = _dot(yr_ref[0], wr_ref[...]) + _dot(yd_ref[0], wd_ref[...])
    y = y * lax.rsqrt(jnp.mean(y * y, axis=-1, keepdims=True) + RMS_EPS) * g_ref[...]
    o_ref[0] = x_ref[0] + gate_ref[0] * y


def _out_projection(y_rw, y_da, w_out_bf16, g_post, gate, x):
    batch, seq, _ = x.shape
    tm = OUT_ROWS
    half = lambda j: pl.BlockSpec((RW_WIDTH, D_MODEL), lambda b, i: (j, 0),
                                  pipeline_mode=pl.Buffered(1))
    row_spec = lambda width: pl.BlockSpec((1, tm, width), lambda b, i: (b, i, 0))
    return pl.pallas_call(
        _out_proj_kernel,
        grid=(batch, seq // tm),
        in_specs=[
            row_spec(RW_WIDTH), row_spec(DA_WIDTH), half(0), half(1),
            pl.BlockSpec((1, D_MODEL), lambda b, i: (0, 0)),
            pl.BlockSpec((1, 1, D_MODEL), lambda b, i: (b, 0, 0)),
            row_spec(D_MODEL),
        ],
        out_specs=row_spec(D_MODEL),
        out_shape=jax.ShapeDtypeStruct(x.shape, x.dtype),
        compiler_params=pltpu.CompilerParams(
            dimension_semantics=("parallel", "parallel"),
            vmem_limit_bytes=VMEM_LIMIT),
        name="out_proj",
    )(y_rw, y_da, w_out_bf16, w_out_bf16, g_post, gate, x)


def kernel(x, c, w_ada, b_ada, g_pre, g_post, w_in, w_out, rw_mu, rw_w0, rw_w_w2,
           rw_a0, rw_w_a2, rw_k_k, rw_k_a, rw_r_k, rw_lnx_w, rw_lnx_b,
           da_lam_q1, da_lam_k1, da_lam_q2, da_lam_k2, da_subln_w):
    batch = x.shape[0]
    depth = w_in.shape[0]
    for l in range(depth):
        lambda_init = 0.8 - 0.6 * math.exp(-0.3 * l)
        mod = _modulation(c, w_ada[l], b_ada[l]).reshape(batch, 3, 1, D_MODEL)
        shift, scale, gate = mod[:, 0], mod[:, 1], mod[:, 2]
        feat, rw_gate, dk, dg, dqt, dvt = _in_projection(
            x, shift, scale, g_pre[l].reshape(1, D_MODEL), rw_mu[l].reshape(1, RW_SHIFTED),
            w_in[l].astype(BF16))
        y_rw = _rwkv_branch(feat, rw_gate, rw_w0[l], rw_w_w2[l], rw_a0[l],
                            rw_w_a2[l], rw_k_k[l], rw_k_a[l], rw_r_k[l],
                            rw_lnx_w[l], rw_lnx_b[l])
        y_da = _diff_attention(dqt, dk, dvt, dg, da_lam_q1[l], da_lam_k1[l],
                               da_lam_q2[l], da_lam_k2[l], da_subln_w[l], lambda_init)
        x = _out_projection(y_rw, y_da, w_out[l].astype(BF16),
                            g_post[l].reshape(1, D_MODEL), gate, x)
    return x
```

```python
import functools
import math

import jax
import jax.numpy as jnp
from jax import lax
from jax.experimental import pallas as pl
from jax.experimental.pallas import tpu as pltpu

D_MODEL = 1024
SEQ = 4096
CHUNK = 64
RW_WIDTH = 512
RW_HEAD = 64
RW_HEADS = RW_WIDTH // RW_HEAD
RW_RANK = 64
RW_SHIFTED = 3 * RW_WIDTH + 2 * RW_RANK
RW_COLS = RW_SHIFTED + RW_WIDTH
DA_WIDTH = 512
DA_HEADS = 4
DA_VDIM = DA_WIDTH // DA_HEADS
DA_QKDIM = DA_VDIM // 2
D_IN = RW_COLS + 4 * DA_WIDTH
RMS_EPS = 1e-6
GN_EPS = 64e-5
SUBLN_EPS = 1e-5

IN_ROWS = 512
IN_SUBTILES = 2
OUT_ROWS = 1024
RW_CHUNK = 64
RW_STEP_CHUNKS = 2
RW_GROUP = 2 * RW_HEAD
ATT_TILE = 512
ONES_ROWS = 16
ATT_STEPS = 2
VMEM_LIMIT = 48 * 1024 * 1024

F32 = jnp.float32
BF16 = jnp.bfloat16
HIGHEST = lax.Precision.HIGHEST


def _silu(t):
    return t * jax.nn.sigmoid(t)


def _dot(a, b, precision=None):
    return jnp.dot(a, b, preferred_element_type=F32, precision=precision)


def _dot_nt(a, b):
    return lax.dot_general(a, b, (((1,), (1,)), ((), ())), preferred_element_type=F32)


def _dot_tn(a, b):
    return lax.dot_general(a, b, (((0,), (0,)), ((), ())), preferred_element_type=F32)


def _split_dot(x, w_exact, terms, *, x_is_lhs):
    acc = None
    for _ in range(terms):
        piece = x.astype(BF16)
        part = _dot(piece, w_exact) if x_is_lhs else _dot(w_exact, piece)
        acc = part if acc is None else acc + part
        x = x - piece.astype(F32)
    return acc


def _mod_kernel(c_ref, w_ref, b_ref, o_ref):
    o_ref[...] = _dot(_silu(c_ref[...]), w_ref[...], HIGHEST) + b_ref[...]


def _modulation(c, w_ada, b_ada):
    batch = c.shape[0]
    return pl.pallas_call(
        _mod_kernel,
        grid=(3,),
        in_specs=[
            pl.BlockSpec((batch, D_MODEL), lambda j: (0, 0)),
            pl.BlockSpec((D_MODEL, D_MODEL), lambda j: (0, j)),
            pl.BlockSpec((1, D_MODEL), lambda j: (0, j)),
        ],
        out_specs=pl.BlockSpec((batch, D_MODEL), lambda j: (0, j)),
        out_shape=jax.ShapeDtypeStruct((batch, 3 * D_MODEL), F32),
        compiler_params=pltpu.CompilerParams(vmem_limit_bytes=VMEM_LIMIT),
        name="adaln_mod",
    )(c, w_ada, b_ada.reshape(1, 3 * D_MODEL))


_Q_COLS = (RW_COLS, RW_COLS + DA_WIDTH)
_V_COLS = (RW_COLS + 2 * DA_WIDTH, RW_COLS + 3 * DA_WIDTH)
_ROW_SEGMENTS = (
    (0, RW_SHIFTED),
    (RW_SHIFTED, RW_COLS),
    (RW_COLS + DA_WIDTH, RW_COLS + 2 * DA_WIDTH),
    (RW_COLS + 3 * DA_WIDTH, D_IN),
)
_ROW_WIDTHS = tuple(hi - lo for lo, hi in _ROW_SEGMENTS)
_ROW_DTYPES = (F32, BF16, BF16, BF16)
Q_SCALE = math.log2(math.e) / math.sqrt(DA_QKDIM)


def _in_proj_kernel(x_ref, shift_ref, scale_ref, g_ref, mu_ref, w_ref, *rest):
    out_refs, wt_ref, ubuf = rest[:-2], rest[-2], rest[-1]
    tm = x_ref.shape[1]

    @pl.when((pl.program_id(0) == 0) & (pl.program_id(1) == 0))
    def _():
        wt_ref[0:DA_WIDTH, :] = w_ref[:, _Q_COLS[0]:_Q_COLS[1]].T
        wt_ref[DA_WIDTH:, :] = w_ref[:, _V_COLS[0]:_V_COLS[1]].T

    @pl.when(pl.program_id(1) == 0)
    def _():
        ubuf[0:8, :] = jnp.zeros((8, RW_SHIFTED), F32)

    qt_ref, vt_ref = out_refs[len(_ROW_SEGMENTS):]
    sub = tm // IN_SUBTILES
    for rows in (slice(t * sub, (t + 1) * sub) for t in range(IN_SUBTILES)):
        x = x_ref[0, rows, :]
        y = x * lax.rsqrt(jnp.mean(x * x, axis=-1, keepdims=True) + RMS_EPS) * g_ref[...]
        h = (y * (1.0 + scale_ref[0]) + shift_ref[0]).astype(BF16)
        for (lo, hi), o_ref in zip(_ROW_SEGMENTS, out_refs):
            u = _dot(h, w_ref[:, lo:hi])
            if lo == 0:
                ubuf[8 + rows.start:8 + rows.stop, :] = u
                prev = ubuf[7 + rows.start:7 + rows.stop, :]
                u = u + (prev - u) * mu_ref[...]
            o_ref[0, rows, :] = u.astype(o_ref.dtype)
        qt = _dot_nt(wt_ref[0:DA_WIDTH, :], h) * Q_SCALE
        vt = _dot_nt(wt_ref[DA_WIDTH:, :], h)
        qt_ref[0, :, 0, :, rows] = qt.astype(BF16).reshape(DA_HEADS, DA_VDIM, sub)
        vt_ref[0, :, 0, :, rows] = vt.astype(BF16).reshape(DA_HEADS, DA_VDIM, sub)
    ubuf[0:8, :] = ubuf[tm:tm + 8, :]


def _in_projection(x, shift, scale, g_pre, mu, w_in_bf16):
    batch, seq, _ = x.shape
    tm = IN_ROWS
    row_spec = lambda width: pl.BlockSpec((1, tm, width), lambda b, i: (b, i, 0))
    vec_spec = pl.BlockSpec((1, 1, D_MODEL), lambda b, i: (b, 0, 0))
    per_tile = ATT_TILE // tm
    t_spec = pl.BlockSpec((1, DA_HEADS, 1, DA_VDIM, tm),
                          lambda b, i: (b, 0, i // per_tile, 0, i % per_tile))
    t_shape = jax.ShapeDtypeStruct((batch, DA_HEADS, seq // ATT_TILE, DA_VDIM, ATT_TILE), BF16)
    return pl.pallas_call(
        _in_proj_kernel,
        grid=(batch, seq // tm),
        in_specs=[
            row_spec(D_MODEL), vec_spec, vec_spec,
            pl.BlockSpec((1, D_MODEL), lambda b, i: (0, 0)),
            pl.BlockSpec((1, RW_SHIFTED), lambda b, i: (0, 0)),
            pl.BlockSpec((D_MODEL, D_IN), lambda b, i: (0, 0), pipeline_mode=pl.Buffered(1)),
        ],
        out_specs=[row_spec(w) for w in _ROW_WIDTHS] + [t_spec, t_spec],
        out_shape=[jax.ShapeDtypeStruct((batch, seq, w), dt)
                   for w, dt in zip(_ROW_WIDTHS, _ROW_DTYPES)] + [t_shape, t_shape],
        scratch_shapes=[pltpu.VMEM((2 * DA_WIDTH, D_MODEL), BF16),
                        pltpu.VMEM((tm + 8, RW_SHIFTED), F32)],
        compiler_params=pltpu.CompilerParams(
            dimension_semantics=("arbitrary", "arbitrary"),
            vmem_limit_bytes=VMEM_LIMIT),
        name="in_proj",
    )(x, shift, scale, g_pre, mu, w_in_bf16)


def _rwkv_kernel(feat_ref, gate_ref, w0_ref, ww2_ref, a0_ref, wa2_ref,
                 kk_ref, ka_ref, rk_ref, lnw_ref, lnb_ref, y_ref, state):
    C, W, N, L = RW_CHUNK, RW_WIDTH, RW_HEAD, RW_GROUP
    heads = L // N
    batch, rows = feat_ref.shape[0], feat_ref.shape[1]
    chunks = rows // C
    ci = pl.program_id(0)

    @pl.when(ci == 0)
    def _():
        state[...] = jnp.zeros(state.shape, F32)

    iota = lambda shape, axis: lax.broadcasted_iota(jnp.int32, shape, axis)
    row2, col2 = iota((2 * C, L), 0), iota((2 * C, L), 1) & (N - 1)
    tri = ((row2 & (C - 1)) > col2) | ((row2 >= C) & ((row2 & (C - 1)) == col2))
    r1, c1 = iota((C, L), 0), iota((C, L), 1) & (N - 1)
    eye = (r1 == c1).astype(F32)
    levels = []
    for bit in range(C.bit_length() - 1):
        levels.append(((r1 >> (bit + 1)) == (c1 >> (bit + 1)))
                      & (((r1 >> bit) & 1) == 1) & (((c1 >> bit) & 1) == 0))
    head_bits = N.bit_length() - 1
    same_head = (iota((L, L), 0) >> head_bits) == (iota((L, L), 1) >> head_bits)
    seg = same_head.astype(BF16)
    cum = (iota((C, C), 0) >= iota((C, C), 1)).astype(BF16)

    def block_diag(t, keep=same_head):
        return jnp.where(keep, jnp.concatenate([t] * heads, axis=0), 0.0).astype(BF16)

    level_keep = [same_head & jnp.concatenate([level] * heads, axis=0) for level in levels]

    def head_sums(t):
        return jnp.concatenate(
            [_dot(t[:, g * L:(g + 1) * L].astype(BF16), seg) for g in range(W // L)], axis=-1)

    roots = []
    for b in range(batch):
        feat = feat_ref[b]
        r, k, v = feat[:, 0:W], feat[:, W:2 * W], feat[:, 2 * W:3 * W]
        w_lo = feat[:, 3 * W:3 * W + RW_RANK]
        a_lo = feat[:, 3 * W + RW_RANK:]
        z = w0_ref[...] + _dot(jnp.tanh(w_lo).astype(BF16), ww2_ref[...])
        logw = -math.exp(-0.5) * jax.nn.sigmoid(z)
        a_pre = a0_ref[...] + _dot(a_lo.astype(BF16), wa2_ref[...])
        kk_raw = k * kk_ref[...]
        norm2 = head_sums(kk_raw * kk_raw)
        cs = [_split_dot(logw[ch * C:(ch + 1) * C], cum, 2, x_is_lhs=False)
              for ch in range(chunks)]
        roots.append((r, k, v, logw, a_pre, kk_raw, norm2, cs))

    prep = {}

    def prepare(ch, anchor=None):
        cr = slice(ch * C, (ch + 1) * C)
        zero = 0.0
        if anchor is not None:
            bits = pltpu.bitcast(anchor[0:1, :], jnp.uint32)
            half_word = jnp.uint32(16)
            bits = lax.shift_right_logical(lax.shift_right_logical(bits, half_word), half_word)
            zero = jnp.concatenate([pltpu.bitcast(bits, F32)] * (W // L), axis=1)
        grs = []
        for b in range(batch):
            r, k, v, logw, a_pre, kk_raw, norm2, cs_all = roots[b]
            r, k, v, logw, cs = r[cr], k[cr], v[cr], logw[cr], cs_all[ch] + zero
            a = jax.nn.sigmoid(a_pre[cr] + zero)
            kk = kk_raw[cr] * lax.rsqrt(jnp.maximum(norm2[cr], 1e-24))
            bb = a * kk
            k2 = k * (1.0 + (a - 1.0) * ka_ref[...])
            cs_last = cs[C - 1:C, :]
            g_all = jnp.exp(cs_last)
            g_inv = jnp.exp(-cs)
            g_end = jnp.exp(cs_last - cs)
            kk_t = (kk * jnp.exp(cs - logw)).astype(BF16)
            r_t = (r * jnp.exp(cs)).astype(BF16)
            k_inv, b_inv = k2 * g_inv, bb * g_inv
            k_end, nb_end = (k2 * g_end).astype(BF16), (-(bb * g_end)).astype(BF16)
            prep[b, ch] = (r, k2, v)
            for g in range(W // L):
                gl = slice(g * L, (g + 1) * L)
                grs.append(dict(
                    b=b, g=g, g_all=g_all[:, gl], v=v[:, gl],
                    lhs=jnp.concatenate([kk_t[:, gl], r_t[:, gl]], axis=0),
                    rhs=jnp.concatenate([block_diag(b_inv[:, gl]), block_diag(k_inv[:, gl])],
                                        axis=0),
                    end=jnp.concatenate([k_end[:, gl], nb_end[:, gl]], axis=0)))
        return grs

    def output_stage(ch, outs):
        cr = slice(ch * C, (ch + 1) * C)
        o = [jnp.concatenate([outs[b, g] for g in range(W // L)], axis=-1) for b in range(batch)]
        mean = [head_sums(t) * (1.0 / N) for t in o]
        yield
        d = [t - m for t, m in zip(o, mean)]
        var = [head_sums(t * t) * (1.0 / N) for t in d]
        bonus = [head_sums(prep[b, ch][0] * prep[b, ch][1] * rk_ref[...]) * prep[b, ch][2]
                 for b in range(batch)]
        yield
        for b in range(batch):
            t = d[b] * lax.rsqrt(var[b] + GN_EPS) * lnw_ref[...] + lnb_ref[...]
            y_ref[b, cr, :] = ((t + bonus[b]) * _silu(gate_ref[b, cr, :].astype(F32))
                               ).astype(y_ref.dtype)

    carried = {(b, g): state[b, g] for b in range(batch) for g in range(W // L)}
    pending = iter(())
    nxt = prepare(0)
    for ch in range(chunks):
        grs = nxt
        for gr in grs:
            m = _dot_nt(gr["lhs"], gr["rhs"])
            gr["a_b"] = jnp.where(tri, m[:, :L], 0.0)
            gr["a_k"] = jnp.where(tri, m[:, L:], 0.0).astype(BF16)
            gr["t"] = eye - jnp.where(levels[0], gr["a_b"][:C], 0.0)
        for gr in grs:
            gr["st"] = carried[gr["b"], gr["g"]]
            gr["x"] = (_dot(gr["a_k"], block_diag(gr["v"]))
                       + _dot_nt(gr["lhs"], block_diag(gr["st"])))
        if ch + 1 < chunks:
            nxt = prepare(ch + 1, anchor=grs[-1]["a_b"])
        for keep in level_keep[1:]:
            for gr in grs:
                gr["inner"] = _dot(gr["t"].astype(BF16),
                                   block_diag(gr["a_b"][:C], keep)).astype(BF16)
            for gr in grs:
                gr["t"] = gr["t"] - _dot(gr["inner"], block_diag(gr["t"]))
            next(pending, None)
        for gr in grs:
            gr["uu"] = _dot(gr["t"].astype(BF16), block_diag(gr["x"][:C]))
        outs = {}
        for gr in grs:
            outs[gr["b"], gr["g"]] = gr["x"][C:] - _dot(gr["a_b"][C:].astype(BF16),
                                                       block_diag(gr["uu"]))
        for gr in grs:
            vu = jnp.concatenate([gr["v"], gr["uu"]], axis=0).astype(BF16)
            delta = jnp.where(same_head, _dot_tn(vu, gr["end"]), 0.0)
            carried[gr["b"], gr["g"]] = gr["st"] * gr["g_all"] + sum(
                delta[h * N:(h + 1) * N] for h in range(heads))
        for _ in pending:
            pass
        pending = output_stage(ch, outs)
    for (b, g), st in carried.items():
        state[b, g] = st
    for _ in pending:
        pass


def _rwkv_branch(feat, gate, w0, w_w2, a0, w_a2, k_k, k_a, r_k, lnx_w, lnx_b):
    batch, seq, _ = feat.shape
    rows = RW_CHUNK * RW_STEP_CHUNKS
    full = lambda shape: pl.BlockSpec(shape, lambda i: (0,) * len(shape))
    vec = lambda t: t.reshape(1, -1)
    return pl.pallas_call(
        _rwkv_kernel,
        grid=(seq // rows,),
        in_specs=[
            pl.BlockSpec((batch, rows, RW_SHIFTED), lambda i: (0, i, 0)),
            pl.BlockSpec((batch, rows, RW_WIDTH), lambda i: (0, i, 0)),
            full((1, RW_WIDTH)), full((RW_RANK, RW_WIDTH)),
            full((1, RW_WIDTH)), full((RW_RANK, RW_WIDTH)),
            full((1, RW_WIDTH)), full((1, RW_WIDTH)), full((1, RW_WIDTH)),
            full((1, RW_WIDTH)), full((1, RW_WIDTH)),
        ],
        out_specs=pl.BlockSpec((batch, rows, RW_WIDTH), lambda i: (0, i, 0)),
        out_shape=jax.ShapeDtypeStruct((batch, seq, RW_WIDTH), BF16),
        scratch_shapes=[
            pltpu.VMEM((batch, RW_WIDTH // RW_GROUP, RW_HEAD, RW_GROUP), F32),
        ],
        compiler_params=pltpu.CompilerParams(
            dimension_semantics=("arbitrary",),
            vmem_limit_bytes=VMEM_LIMIT),
        name="rwkv7_chunked",
    )(feat, gate, vec(w0), w_w2.astype(BF16), vec(a0), w_a2.astype(BF16),
      vec(k_k), vec(k_a), vec(r_k), vec(lnx_w), vec(lnx_b))


def _attn_kernel(lq1_ref, lk1_ref, lq2_ref, lk2_ref, sw_ref, qt_ref, k_ref, vt_ref,
                 g_ref, o_ref, s_buf, p_buf, acc_buf, *, lambda_init):
    T, HALF = ATT_TILE, ATT_TILE // 2
    tiles = k_ref.shape[1] // T

    comp = lax.broadcasted_iota(jnp.int32, (DA_VDIM, T), 0) < DA_QKDIM
    key = lax.broadcasted_iota(jnp.int32, (HALF, HALF), 0)
    query = lax.broadcasted_iota(jnp.int32, (HALF, HALF), 1)
    visible = (query // CHUNK) >= (key // CHUNK)
    lam = (jnp.exp(jnp.sum(lq1_ref[...] * lk1_ref[...], axis=-1, keepdims=True))
           - jnp.exp(jnp.sum(lq2_ref[...] * lk2_ref[...], axis=-1, keepdims=True))
           + lambda_init)

    for step in range(ATT_STEPS):
        order = [i for k in range(step, tiles // 2, ATT_STEPS) for i in (tiles - 1 - k, k)]
        pl.when(pl.program_id(2) == step)(functools.partial(
            _attn_tasks, [(i, j) for i in order for j in range(i + 1)],
            comp, visible, lam, sw_ref, qt_ref, k_ref, vt_ref, g_ref, o_ref,
            s_buf, p_buf, acc_buf, lambda_init))


def _attn_tasks(tasks, comp, visible, lam, sw_ref, qt_ref, k_ref, vt_ref, g_ref, o_ref,
                s_buf, p_buf, acc_buf, lambda_init):
    T, HALF = ATT_TILE, ATT_TILE // 2
    qt_parts = {}

    def qt_c(i):
        if i not in qt_parts:
            qt = qt_ref[0, 0, i]
            zero = jnp.zeros_like(qt)
            qt_parts[i] = (jnp.where(comp, qt, zero), jnp.where(comp, zero, qt))
        return qt_parts[i]

    def colmax(s):
        return jnp.max(s, axis=0, keepdims=True)

    def scores(n):
        i, j = tasks[n]
        kb = k_ref[0, j * T:(j + 1) * T, :]
        blk_max = []
        for c in range(2):
            q = qt_c(i)[c]
            if j < i:
                s = _dot(kb, q)
                s_buf[n % 2, c] = s
                blk_max.append(colmax(s))
            else:
                s_r = _dot(kb, q[:, HALF:])
                s_b = jnp.where(visible, s_r[HALF:], -jnp.inf)
                s_l = jnp.where(visible, _dot(kb[:HALF], q[:, :HALF]), -jnp.inf)
                s_buf[n % 2, c, :HALF, HALF:] = s_r[:HALF]
                s_buf[n % 2, c, HALF:, HALF:] = s_b
                s_buf[n % 2, c, :HALF, :HALF] = s_l
                blk_max.append(jnp.concatenate(
                    [colmax(s_l), jnp.maximum(colmax(s_r[:HALF]), colmax(s_b))], axis=1))
        return blk_max

    def softmax(n, blk_maxes, maxes):
        i, j = tasks[n]
        new, alphas = [], []
        for c in range(2):
            if j == 0:
                m_new, alpha = blk_maxes[c], None
            else:
                m_new = jnp.maximum(maxes[c], blk_maxes[c])
                alpha = jnp.exp2(maxes[c] - m_new)
            prob = lambda s, m: jnp.exp2((s - m).astype(BF16))
            if j < i:
                p_buf[n % 2, c] = prob(s_buf[n % 2, c], m_new)
            else:
                left, right = m_new[:, :HALF], m_new[:, HALF:]
                p_buf[n % 2, c, :HALF, :HALF] = prob(s_buf[n % 2, c, :HALF, :HALF], left)
                p_buf[n % 2, c, :HALF, HALF:] = prob(s_buf[n % 2, c, :HALF, HALF:], right)
                p_buf[n % 2, c, HALF:, HALF:] = prob(s_buf[n % 2, c, HALF:, HALF:], right)
            new.append(m_new)
            alphas.append(alpha)
        return new, alphas

    ones_rows = (lax.broadcasted_iota(jnp.int32, (ONES_ROWS, T), 0) == 0).astype(BF16)

    def values(n, alphas):
        i, j = tasks[n]
        vtb = jnp.concatenate([vt_ref[0, 0, j], ones_rows], axis=0)
        for c in range(2):
            if j < i:
                pv = _dot(vtb, p_buf[n % 2, c])
            else:
                pv = jnp.concatenate(
                    [_dot(vtb[:, :HALF], p_buf[n % 2, c, :HALF, :HALF]),
                     _dot(vtb, p_buf[n % 2, c, :, HALF:])], axis=1)
            acc_buf[i % 2, c] = pv if j == 0 else alphas[c] * acc_buf[i % 2, c] + pv

    def finish(i):
        num0, num1 = acc_buf[i % 2, 0, :DA_VDIM], acc_buf[i % 2, 1, :DA_VDIM]
        l0 = acc_buf[i % 2, 0, DA_VDIM:DA_VDIM + 1]
        l1 = acc_buf[i % 2, 1, DA_VDIM:DA_VDIM + 1]
        o = (num0 * (1.0 / l0) - num1 * (lam / l1)).T
        o = o * lax.rsqrt(jnp.mean(o * o, axis=-1, keepdims=True) + SUBLN_EPS)
        o = o * sw_ref[...] * (1.0 - lambda_init)
        rows = slice(i * T, (i + 1) * T)
        o_ref[0, rows, :] = (o * _silu(g_ref[0, rows, :].astype(F32))).astype(o_ref.dtype)

    blk_max = scores(0)
    maxes, pending = None, None
    for n, (i, j) in enumerate(tasks):
        next_max = scores(n + 1) if n + 1 < len(tasks) else None
        if pending is not None:
            values(pending[0], pending[1])
            if pending[2]:
                finish(tasks[pending[0]][0])
        maxes, alphas = softmax(n, blk_max, maxes)
        pending = (n, alphas, j == i)
        blk_max = next_max
    values(pending[0], pending[1])
    finish(tasks[-1][0])


def _diff_attention(qt, k, vt, gate, lam_q1, lam_k1, lam_q2, lam_k2, subln_w, lambda_init):
    batch, seq, _ = k.shape
    T = ATT_TILE
    small = lambda n: pl.BlockSpec((1, n), lambda b, h, s: (0, 0))
    rows = pl.BlockSpec((1, seq, DA_VDIM), lambda b, h, s: (b, 0, h))
    cols = pl.BlockSpec((1, 1, seq // T, DA_VDIM, T), lambda b, h, s: (b, h, 0, 0, 0))
    return pl.pallas_call(
        functools.partial(_attn_kernel, lambda_init=lambda_init),
        grid=(batch, DA_HEADS, ATT_STEPS),
        in_specs=[small(DA_QKDIM)] * 4 + [small(DA_VDIM), cols, rows, cols, rows],
        out_specs=rows,
        out_shape=jax.ShapeDtypeStruct((batch, seq, DA_WIDTH), BF16),
        scratch_shapes=[pltpu.VMEM((2, 2, T, T), F32),
                        pltpu.VMEM((2, 2, T, T), BF16),
                        pltpu.VMEM((2, 2, DA_VDIM + ONES_ROWS, T), F32)],
        compiler_params=pltpu.CompilerParams(
            dimension_semantics=("parallel", "parallel", "arbitrary"),
            vmem_limit_bytes=VMEM_LIMIT),
        name="diff_attn",
    )(*(t.reshape(1, -1) for t in (lam_q1, lam_k1, lam_q2, lam_k2, subln_w)),
      qt, k, vt, gate)


def _out_proj_kernel(yr_ref, yd_ref, wr_ref, wd_ref, g_ref, gate_ref, x_ref, o_ref):
    y = _dot(yr_ref[0], wr_ref[...]) + _dot(yd_ref[0], wd_ref[...])
    y = y * lax.rsqrt(jnp.mean(y * y, axis=-1, keepdims=True) + RMS_EPS) * g_ref[...]
    o_ref[0] = x_ref[0] + gate_ref[0] * y


def _out_projection(y_rw, y_da, w_out_bf16, g_post, gate, x):
    batch, seq, _ = x.shape
    tm = OUT_ROWS
    half = lambda j: pl.BlockSpec((RW_WIDTH, D_MODEL), lambda b, i: (j, 0),
                                  pipeline_mode=pl.Buffered(1))
    row_spec = lambda width: pl.BlockSpec((1, tm, width), lambda b, i: (b, i, 0))
    return pl.pallas_call(
        _out_proj_kernel,
        grid=(batch, seq // tm),
        in_specs=[
            row_spec(RW_WIDTH), row_spec(DA_WIDTH), half(0), half(1),
            pl.BlockSpec((1, D_MODEL), lambda b, i: (0, 0)),
            pl.BlockSpec((1, 1, D_MODEL), lambda b, i: (b, 0, 0)),
            row_spec(D_MODEL),
        ],
        out_specs=row_spec(D_MODEL),
        out_shape=jax.ShapeDtypeStruct(x.shape, x.dtype),
        compiler_params=pltpu.CompilerParams(
            dimension_semantics=("parallel", "parallel"),
            vmem_limit_bytes=VMEM_LIMIT),
        name="out_proj",
    )(y_rw, y_da, w_out_bf16, w_out_bf16, g_post, gate, x)


def kernel(x, c, w_ada, b_ada, g_pre, g_post, w_in, w_out, rw_mu, rw_w0, rw_w_w2,
           rw_a0, rw_w_a2, rw_k_k, rw_k_a, rw_r_k, rw_lnx_w, rw_lnx_b,
           da_lam_q1, da_lam_k1, da_lam_q2, da_lam_k2, da_subln_w):
    batch = x.shape[0]
    depth = w_in.shape[0]
    for l in range(depth):
        lambda_init = 0.8 - 0.6 * math.exp(-0.3 * l)
        mod = _modulation(c, w_ada[l], b_ada[l]).reshape(batch, 3, 1, D_MODEL)
        shift, scale, gate = mod[:, 0], mod[:, 1], mod[:, 2]
        feat, rw_gate, dk, dg, dqt, dvt = _in_projection(
            x, shift, scale, g_pre[l].reshape(1, D_MODEL), rw_mu[l].reshape(1, RW_SHIFTED),
            w_in[l].astype(BF16))
        y_rw = _rwkv_branch(feat, rw_gate, rw_w0[l], rw_w_w2[l], rw_a0[l],
                            rw_w_a2[l], rw_k_k[l], rw_k_a[l], rw_r_k[l],
                            rw_lnx_w[l], rw_lnx_b[l])
        y_da = _diff_attention(dqt, dk, dvt, dg, da_lam_q1[l], da_lam_k1[l],
                               da_lam_q2[l], da_lam_k2[l], da_subln_w[l], lambda_init)
        x = _out_projection(y_rw, y_da, w_out[l].astype(BF16),
                            g_post[l].reshape(1, D_MODEL), gate, x)
    return x
```

```python
import functools
import math

import jax
import jax.numpy as jnp
from jax import lax
from jax.experimental import pallas as pl
from jax.experimental.pallas import tpu as pltpu

D_MODEL = 1024
SEQ = 4096
CHUNK = 64
RW_WIDTH = 512
RW_HEAD = 64
RW_HEADS = RW_WIDTH // RW_HEAD
RW_RANK = 64
RW_SHIFTED = 3 * RW_WIDTH + 2 * RW_RANK
RW_COLS = RW_SHIFTED + RW_WIDTH
DA_WIDTH = 512
DA_HEADS = 4
DA_VDIM = DA_WIDTH // DA_HEADS
DA_QKDIM = DA_VDIM // 2
D_IN = RW_COLS + 4 * DA_WIDTH
RMS_EPS = 1e-6
GN_EPS = 64e-5
SUBLN_EPS = 1e-5

IN_ROWS = 512
IN_SUBTILES = 2
OUT_ROWS = 1024
RW_CHUNK = 64
RW_STEP_CHUNKS = 2
RW_GROUP = 2 * RW_HEAD
ATT_TILE = 512
ONES_ROWS = 16
ATT_STEPS = 2
VMEM_LIMIT = 48 * 1024 * 1024

F32 = jnp.float32
BF16 = jnp.bfloat16
HIGHEST = lax.Precision.HIGHEST


def _silu(t):
    return t * jax.nn.sigmoid(t)


def _dot(a, b, precision=None):
    return jnp.dot(a, b, preferred_element_type=F32, precision=precision)


def _dot_nt(a, b):
    return lax.dot_general(a, b, (((1,), (1,)), ((), ())), preferred_element_type=F32)


def _dot_tn(a, b):
    return lax.dot_general(a, b, (((0,), (0,)), ((), ())), preferred_element_type=F32)


def _split_dot(x, w_exact, terms, *, x_is_lhs):
    acc = None
    for _ in range(terms):
        piece = x.astype(BF16)
        part = _dot(piece, w_exact) if x_is_lhs else _dot(w_exact, piece)
        acc = part if acc is None else acc + part
        x = x - piece.astype(F32)
    return acc


def _mod_kernel(c_ref, w_ref, b_ref, o_ref):
    o_ref[...] = _dot(_silu(c_ref[...]), w_ref[...], HIGHEST) + b_ref[...]


def _modulation(c, w_ada, b_ada):
    batch = c.shape[0]
    return pl.pallas_call(
        _mod_kernel,
        grid=(3,),
        in_specs=[
            pl.BlockSpec((batch, D_MODEL), lambda j: (0, 0)),
            pl.BlockSpec((D_MODEL, D_MODEL), lambda j: (0, j)),
            pl.BlockSpec((1, D_MODEL), lambda j: (0, j)),
        ],
        out_specs=pl.BlockSpec((batch, D_MODEL), lambda j: (0, j)),
        out_shape=jax.ShapeDtypeStruct((batch, 3 * D_MODEL), F32),
        compiler_params=pltpu.CompilerParams(vmem_limit_bytes=VMEM_LIMIT),
        name="adaln_mod",
    )(c, w_ada, b_ada.reshape(1, 3 * D_MODEL))


_Q_COLS = (RW_COLS, RW_COLS + DA_WIDTH)
_V_COLS = (RW_COLS + 2 * DA_WIDTH, RW_COLS + 3 * DA_WIDTH)
_ROW_SEGMENTS = (
    (0, RW_SHIFTED),
    (RW_SHIFTED, RW_COLS),
    (RW_COLS + DA_WIDTH, RW_COLS + 2 * DA_WIDTH),
    (RW_COLS + 3 * DA_WIDTH, D_IN),
)
_ROW_WIDTHS = tuple(hi - lo for lo, hi in _ROW_SEGMENTS)
_ROW_DTYPES = (F32, BF16, BF16, BF16)
Q_SCALE = math.log2(math.e) / math.sqrt(DA_QKDIM)


def _in_proj_kernel(x_ref, shift_ref, scale_ref, g_ref, mu_ref, w_ref, *rest):
    out_refs, wt_ref, ubuf = rest[:-2], rest[-2], rest[-1]
    tm = x_ref.shape[1]

    @pl.when((pl.program_id(0) == 0) & (pl.program_id(1) == 0))
    def _():
        wt_ref[0:DA_WIDTH, :] = w_ref[:, _Q_COLS[0]:_Q_COLS[1]].T
        wt_ref[DA_WIDTH:, :] = w_ref[:, _V_COLS[0]:_V_COLS[1]].T

    @pl.when(pl.program_id(1) == 0)
    def _():
        ubuf[0:8, :] = jnp.zeros((8, RW_SHIFTED), F32)

    qt_ref, vt_ref = out_refs[len(_ROW_SEGMENTS):]
    sub = tm // IN_SUBTILES
    for rows in (slice(t * sub, (t + 1) * sub) for t in range(IN_SUBTILES)):
        x = x_ref[0, rows, :]
        y = x * lax.rsqrt(jnp.mean(x * x, axis=-1, keepdims=True) + RMS_EPS) * g_ref[...]
        h = (y * (1.0 + scale_ref[0]) + shift_ref[0]).astype(BF16)
        for (lo, hi), o_ref in zip(_ROW_SEGMENTS, out_refs):
            u = _dot(h, w_ref[:, lo:hi])
            if lo == 0:
                ubuf[8 + rows.start:8 + rows.stop, :] = u
                prev = ubuf[7 + rows.start:7 + rows.stop, :]
                u = u + (prev - u) * mu_ref[...]
            o_ref[0, rows, :] = u.astype(o_ref.dtype)
        qt = _dot_nt(wt_ref[0:DA_WIDTH, :], h) * Q_SCALE
        vt = _dot_nt(wt_ref[DA_WIDTH:, :], h)
        qt_ref[0, :, 0, :, rows] = qt.astype(BF16).reshape(DA_HEADS, DA_VDIM, sub)
        vt_ref[0, :, 0, :, rows] = vt.astype(BF16).reshape(DA_HEADS, DA_VDIM, sub)
    ubuf[0:8, :] = ubuf[tm:tm + 8, :]


def _in_projection(x, shift, scale, g_pre, mu, w_in_bf16):
    batch, seq, _ = x.shape
    tm = IN_ROWS
    row_spec = lambda width: pl.BlockSpec((1, tm, width), lambda b, i: (b, i, 0))
    vec_spec = pl.BlockSpec((1, 1, D_MODEL), lambda b, i: (b, 0, 0))
    per_tile = ATT_TILE // tm
    t_spec = pl.BlockSpec((1, DA_HEADS, 1, DA_VDIM, tm),
                          lambda b, i: (b, 0, i // per_tile, 0, i % per_tile))
    t_shape = jax.ShapeDtypeStruct((batch, DA_HEADS, seq // ATT_TILE, DA_VDIM, ATT_TILE), BF16)
    return pl.pallas_call(
        _in_proj_kernel,
        grid=(batch, seq // tm),
        in_specs=[
            row_spec(D_MODEL), vec_spec, vec_spec,
            pl.BlockSpec((1, D_MODEL), lambda b, i: (0, 0)),
            pl.BlockSpec((1, RW_SHIFTED), lambda b, i: (0, 0)),
            pl.BlockSpec((D_MODEL, D_IN), lambda b, i: (0, 0), pipeline_mode=pl.Buffered(1)),
        ],
        out_specs=[row_spec(w) for w in _ROW_WIDTHS] + [t_spec, t_spec],
        out_shape=[jax.ShapeDtypeStruct((batch, seq, w), dt)
                   for w, dt in zip(_ROW_WIDTHS, _ROW_DTYPES)] + [t_shape, t_shape],
        scratch_shapes=[pltpu.VMEM((2 * DA_WIDTH, D_MODEL), BF16),
                        pltpu.VMEM((tm + 8, RW_SHIFTED), F32)],
        compiler_params=pltpu.CompilerParams(
            dimension_semantics=("arbitrary", "arbitrary"),
            vmem_limit_bytes=VMEM_LIMIT),
        name="in_proj",
    )(x, shift, scale, g_pre, mu, w_in_bf16)


def _rwkv_kernel(feat_ref, gate_ref, w0_ref, ww2_ref, a0_ref, wa2_ref,
                 kk_ref, ka_ref, rk_ref, lnw_ref, lnb_ref, y_ref, state):
    C, W, N, L = RW_CHUNK, RW_WIDTH, RW_HEAD, RW_GROUP
    heads = L // N
    batch, rows = feat_ref.shape[0], feat_ref.shape[1]
    chunks = rows // C
    ci = pl.program_id(0)

    @pl.when(ci == 0)
    def _():
        state[...] = jnp.zeros(state.shape, F32)

    iota = lambda shape, axis: lax.broadcasted_iota(jnp.int32, shape, axis)
    row2, col2 = iota((2 * C, L), 0), iota((2 * C, L), 1) & (N - 1)
    tri = ((row2 & (C - 1)) > col2) | ((row2 >= C) & ((row2 & (C - 1)) == col2))
    r1, c1 = iota((C, L), 0), iota((C, L), 1) & (N - 1)
    eye = (r1 == c1).astype(F32)
    levels = []
    for bit in range(C.bit_length() - 1):
        levels.append(((r1 >> (bit + 1)) == (c1 >> (bit + 1)))
                      & (((r1 >> bit) & 1) == 1) & (((c1 >> bit) & 1) == 0))
    head_bits = N.bit_length() - 1
    same_head = (iota((L, L), 0) >> head_bits) == (iota((L, L), 1) >> head_bits)
    seg = same_head.astype(BF16)
    cum = (iota((C, C), 0) >= iota((C, C), 1)).astype(BF16)

    def block_diag(t, keep=same_head):
        return jnp.where(keep, jnp.concatenate([t] * heads, axis=0), 0.0).astype(BF16)

    level_keep = [same_head & jnp.concatenate([level] * heads, axis=0) for level in levels]

    def head_sums(t):
        return jnp.concatenate(
            [_dot(t[:, g * L:(g + 1) * L].astype(BF16), seg) for g in range(W // L)], axis=-1)

    roots = []
    for b in range(batch):
        feat = feat_ref[b]
        r, k, v = feat[:, 0:W], feat[:, W:2 * W], feat[:, 2 * W:3 * W]
        w_lo = feat[:, 3 * W:3 * W + RW_RANK]
        a_lo = feat[:, 3 * W + RW_RANK:]
        z = w0_ref[...] + _dot(jnp.tanh(w_lo).astype(BF16), ww2_ref[...])
        logw = -math.exp(-0.5) * jax.nn.sigmoid(z)
        a_pre = a0_ref[...] + _dot(a_lo.astype(BF16), wa2_ref[...])
        kk_raw = k * kk_ref[...]
        norm2 = head_sums(kk_raw * kk_raw)
        cs = [_split_dot(logw[ch * C:(ch + 1) * C], cum, 2, x_is_lhs=False)
              for ch in range(chunks)]
        roots.append((r, k, v, logw, a_pre, kk_raw, norm2, cs))

    prep = {}

    def prepare(ch, anchor=None):
        cr = slice(ch * C, (ch + 1) * C)
        zero = 0.0
        if anchor is not None:
            bits = pltpu.bitcast(anchor[0:1, :], jnp.uint32)
            half_word = jnp.uint32(16)
            bits = lax.shift_right_logical(lax.shift_right_logical(bits, half_word), half_word)
            zero = jnp.concatenate([pltpu.bitcast(bits, F32)] * (W // L), axis=1)
        grs = []
        for b in range(batch):
            r, k, v, logw, a_pre, kk_raw, norm2, cs_all = roots[b]
            r, k, v, logw, cs = r[cr], k[cr], v[cr], logw[cr], cs_all[ch] + zero
            a = jax.nn.sigmoid(a_pre[cr] + zero)
            kk = kk_raw[cr] * lax.rsqrt(jnp.maximum(norm2[cr], 1e-24))
            bb = a * kk
            k2 = k * (1.0 + (a - 1.0) * ka_ref[...])
            cs_last = cs[C - 1:C, :]
            g_all = jnp.exp(cs_last)
            g_inv = jnp.exp(-cs)
            g_end = jnp.exp(cs_last - cs)
            kk_t = (kk * jnp.exp(cs - logw)).astype(BF16)
            r_t = (r * jnp.exp(cs)).astype(BF16)
            k_inv, b_inv = k2 * g_inv, bb * g_inv
            k_end, nb_end = (k2 * g_end).astype(BF16), (-(bb * g_end)).astype(BF16)
            prep[b, ch] = (r, k2, v)
            for g in range(W // L):
                gl = slice(g * L, (g + 1) * L)
                grs.append(dict(
                    b=b, g=g, g_all=g_all[:, gl], v=v[:, gl],
                    lhs=jnp.concatenate([kk_t[:, gl], r_t[:, gl]], axis=0),
                    rhs=jnp.concatenate([block_diag(b_inv[:, gl]), block_diag(k_inv[:, gl])],
                                        axis=0),
                    end=jnp.concatenate([k_end[:, gl], nb_end[:, gl]], axis=0)))
        return grs

    def output_stage(ch, outs):
        cr = slice(ch * C, (ch + 1) * C)
        o = [jnp.concatenate([outs[b, g] for g in range(W // L)], axis=-1) for b in range(batch)]
        mean = [head_sums(t) * (1.0 / N) for t in o]
        yield
        d = [t - m for t, m in zip(o, mean)]
        var = [head_sums(t * t) * (1.0 / N) for t in d]
        bonus = [head_sums(prep[b, ch][0] * prep[b, ch][1] * rk_ref[...]) * prep[b, ch][2]
                 for b in range(batch)]
        yield
        for b in range(batch):
            t = d[b] * lax.rsqrt(var[b] + GN_EPS) * lnw_ref[...] + lnb_ref[...]
            y_ref[b, cr, :] = ((t + bonus[b]) * _silu(gate_ref[b, cr, :].astype(F32))
                               ).astype(y_ref.dtype)

    carried = {(b, g): state[b, g] for b in range(batch) for g in range(W // L)}
    pending = iter(())
    nxt = prepare(0)
    for ch in range(chunks):
        grs = nxt
        for gr in grs:
            m = _dot_nt(gr["lhs"], gr["rhs"])
            gr["a_b"] = jnp.where(tri, m[:, :L], 0.0)
            gr["a_k"] = jnp.where(tri, m[:, L:], 0.0).astype(BF16)
            gr["t"] = eye - jnp.where(levels[0], gr["a_b"][:C], 0.0)
        for gr in grs:
            gr["st"] = carried[gr["b"], gr["g"]]
            gr["x"] = _dot(jnp.concatenate([gr["a_k"], gr["lhs"]], axis=1),
                           jnp.concatenate([block_diag(gr["v"]), block_diag(gr["st"]).T], axis=0))
        if ch + 1 < chunks:
            nxt = prepare(ch + 1, anchor=grs[-1]["a_b"])
        for keep in level_keep[1:]:
            for gr in grs:
                gr["inner"] = _dot(gr["t"].astype(BF16),
                                   block_diag(gr["a_b"][:C], keep)).astype(BF16)
            for gr in grs:
                gr["t"] = gr["t"] - _dot(gr["inner"], block_diag(gr["t"]))
            next(pending, None)
        for gr in grs:
            gr["uu"] = _dot(gr["t"].astype(BF16), block_diag(gr["x"][:C]))
        outs = {}
        for gr in grs:
            outs[gr["b"], gr["g"]] = gr["x"][C:] - _dot(gr["a_b"][C:].astype(BF16),
                                                       block_diag(gr["uu"]))
        for gr in grs:
            vu = jnp.concatenate([gr["v"], gr["uu"]], axis=0).astype(BF16)
            delta = jnp.where(same_head, _dot_tn(vu, gr["end"]), 0.0)
            carried[gr["b"], gr["g"]] = gr["st"] * gr["g_all"] + sum(
                delta[h * N:(h + 1) * N] for h in range(heads))
        for _ in pending:
            pass
        pending = output_stage(ch, outs)
    for (b, g), st in carried.items():
        state[b, g] = st
    for _ in pending:
        pass


def _rwkv_branch(feat, gate, w0, w_w2, a0, w_a2, k_k, k_a, r_k, lnx_w, lnx_b):
    batch, seq, _ = feat.shape
    rows = RW_CHUNK * RW_STEP_CHUNKS
    full = lambda shape: pl.BlockSpec(shape, lambda i: (0,) * len(shape))
    vec = lambda t: t.reshape(1, -1)
    return pl.pallas_call(
        _rwkv_kernel,
        grid=(seq // rows,),
        in_specs=[
            pl.BlockSpec((batch, rows, RW_SHIFTED), lambda i: (0, i, 0)),
            pl.BlockSpec((batch, rows, RW_WIDTH), lambda i: (0, i, 0)),
            full((1, RW_WIDTH)), full((RW_RANK, RW_WIDTH)),
            full((1, RW_WIDTH)), full((RW_RANK, RW_WIDTH)),
            full((1, RW_WIDTH)), full((1, RW_WIDTH)), full((1, RW_WIDTH)),
            full((1, RW_WIDTH)), full((1, RW_WIDTH)),
        ],
        out_specs=pl.BlockSpec((batch, rows, RW_WIDTH), lambda i: (0, i, 0)),
        out_shape=jax.ShapeDtypeStruct((batch, seq, RW_WIDTH), BF16),
        scratch_shapes=[
            pltpu.VMEM((batch, RW_WIDTH // RW_GROUP, RW_HEAD, RW_GROUP), F32),
        ],
        compiler_params=pltpu.CompilerParams(
            dimension_semantics=("arbitrary",),
            vmem_limit_bytes=VMEM_LIMIT),
        name="rwkv7_chunked",
    )(feat, gate, vec(w0), w_w2.astype(BF16), vec(a0), w_a2.astype(BF16),
      vec(k_k), vec(k_a), vec(r_k), vec(lnx_w), vec(lnx_b))


def _attn_kernel(lq1_ref, lk1_ref, lq2_ref, lk2_ref, sw_ref, qt_ref, k_ref, vt_ref,
                 g_ref, o_ref, s_buf, p_buf, acc_buf, *, lambda_init):
    T, HALF = ATT_TILE, ATT_TILE // 2
    tiles = k_ref.shape[1] // T

    comp = lax.broadcasted_iota(jnp.int32, (DA_VDIM, T), 0) < DA_QKDIM
    key = lax.broadcasted_iota(jnp.int32, (HALF, HALF), 0)
    query = lax.broadcasted_iota(jnp.int32, (HALF, HALF), 1)
    visible = (query // CHUNK) >= (key // CHUNK)
    lam = (jnp.exp(jnp.sum(lq1_ref[...] * lk1_ref[...], axis=-1, keepdims=True))
           - jnp.exp(jnp.sum(lq2_ref[...] * lk2_ref[...], axis=-1, keepdims=True))
           + lambda_init)

    for step in range(ATT_STEPS):
        order = [i for k in range(step, tiles // 2, ATT_STEPS) for i in (tiles - 1 - k, k)]
        pl.when(pl.program_id(2) == step)(functools.partial(
            _attn_tasks, [(i, j) for i in order for j in range(i + 1)],
            comp, visible, lam, sw_ref, qt_ref, k_ref, vt_ref, g_ref, o_ref,
            s_buf, p_buf, acc_buf, lambda_init))


def _attn_tasks(tasks, comp, visible, lam, sw_ref, qt_ref, k_ref, vt_ref, g_ref, o_ref,
                s_buf, p_buf, acc_buf, lambda_init):
    T, HALF = ATT_TILE, ATT_TILE // 2
    qt_parts = {}

    def qt_c(i):
        if i not in qt_parts:
            qt = qt_ref[0, 0, i]
            zero = jnp.zeros_like(qt)
            qt_parts[i] = (jnp.where(comp, qt, zero), jnp.where(comp, zero, qt))
        return qt_parts[i]

    def colmax(s):
        return jnp.max(s, axis=0, keepdims=True)

    def scores(n):
        i, j = tasks[n]
        kb = k_ref[0, j * T:(j + 1) * T, :]
        blk_max = []
        for c in range(2):
            q = qt_c(i)[c]
            if j < i:
                s = _dot(kb, q)
                s_buf[n % 2, c] = s
                blk_max.append(colmax(s))
            else:
                s_r = _dot(kb, q[:, HALF:])
                s_b = jnp.where(visible, s_r[HALF:], -jnp.inf)
                s_l = jnp.where(visible, _dot(kb[:HALF], q[:, :HALF]), -jnp.inf)
                s_buf[n % 2, c, :HALF, HALF:] = s_r[:HALF]
                s_buf[n % 2, c, HALF:, HALF:] = s_b
                s_buf[n % 2, c, :HALF, :HALF] = s_l
                blk_max.append(jnp.concatenate(
                    [colmax(s_l), jnp.maximum(colmax(s_r[:HALF]), colmax(s_b))], axis=1))
        return blk_max

    def softmax(n, blk_maxes, maxes):
        i, j = tasks[n]
        new, alphas = [], []
        for c in range(2):
            if j == 0:
                m_new, alpha = blk_maxes[c], None
            else:
                m_new = jnp.maximum(maxes[c], blk_maxes[c])
                alpha = jnp.exp2(maxes[c] - m_new)
            prob = lambda s, m: jnp.exp2((s - m).astype(BF16))
            if j < i:
                p_buf[n % 2, c] = prob(s_buf[n % 2, c], m_new)
            else:
                left, right = m_new[:, :HALF], m_new[:, HALF:]
                p_buf[n % 2, c, :HALF, :HALF] = prob(s_buf[n % 2, c, :HALF, :HALF], left)
                p_buf[n % 2, c, :HALF, HALF:] = prob(s_buf[n % 2, c, :HALF, HALF:], right)
                p_buf[n % 2, c, HALF:, HALF:] = prob(s_buf[n % 2, c, HALF:, HALF:], right)
            new.append(m_new)
            alphas.append(alpha)
        return new, alphas

    ones_rows = (lax.broadcasted_iota(jnp.int32, (ONES_ROWS, T), 0) == 0).astype(BF16)

    def values(n, alphas):
        i, j = tasks[n]
        vtb = jnp.concatenate([vt_ref[0, 0, j], ones_rows], axis=0)
        for c in range(2):
            if j < i:
                pv = _dot(vtb, p_buf[n % 2, c])
            else:
                pv = jnp.concatenate(
                    [_dot(vtb[:, :HALF], p_buf[n % 2, c, :HALF, :HALF]),
                     _dot(vtb, p_buf[n % 2, c, :, HALF:])], axis=1)
            acc_buf[i % 2, c] = pv if j == 0 else alphas[c] * acc_buf[i % 2, c] + pv

    def finish(i):
        num0, num1 = acc_buf[i % 2, 0, :DA_VDIM], acc_buf[i % 2, 1, :DA_VDIM]
        l0 = acc_buf[i % 2, 0, DA_VDIM:DA_VDIM + 1]
        l1 = acc_buf[i % 2, 1, DA_VDIM:DA_VDIM + 1]
        o = (num0 * (1.0 / l0) - num1 * (lam / l1)).T
        o = o * lax.rsqrt(jnp.mean(o * o, axis=-1, keepdims=True) + SUBLN_EPS)
        o = o * sw_ref[...] * (1.0 - lambda_init)
        rows = slice(i * T, (i + 1) * T)
        o_ref[0, rows, :] = (o * _silu(g_ref[0, rows, :].astype(F32))).astype(o_ref.dtype)

    blk_max = scores(0)
    maxes, pending = None, None
    for n, (i, j) in enumerate(tasks):
        next_max = scores(n + 1) if n + 1 < len(tasks) else None
        if pending is not None:
            values(pending[0], pending[1])
            if pending[2]:
                finish(tasks[pending[0]][0])
        maxes, alphas = softmax(n, blk_max, maxes)
        pending = (n, alphas, j == i)
        blk_max = next_max
    values(pending[0], pending[1])
    finish(tasks[-1][0])


def _diff_attention(qt, k, vt, gate, lam_q1, lam_k1, lam_q2, lam_k2, subln_w, lambda_init):
    batch, seq, _ = k.shape
    T = ATT_TILE
    small = lambda n: pl.BlockSpec((1, n), lambda b, h, s: (0, 0))
    rows = pl.BlockSpec((1, seq, DA_VDIM), lambda b, h, s: (b, 0, h))
    cols = pl.BlockSpec((1, 1, seq // T, DA_VDIM, T), lambda b, h, s: (b, h, 0, 0, 0))
    return pl.pallas_call(
        functools.partial(_attn_kernel, lambda_init=lambda_init),
        grid=(batch, DA_HEADS, ATT_STEPS),
        in_specs=[small(DA_QKDIM)] * 4 + [small(DA_VDIM), cols, rows, cols, rows],
        out_specs=rows,
        out_shape=jax.ShapeDtypeStruct((batch, seq, DA_WIDTH), BF16),
        scratch_shapes=[pltpu.VMEM((2, 2, T, T), F32),
                        pltpu.VMEM((2, 2, T, T), BF16),
                        pltpu.VMEM((2, 2, DA_VDIM + ONES_ROWS, T), F32)],
        compiler_params=pltpu.CompilerParams(
            dimension_semantics=("parallel", "parallel", "arbitrary"),
            vmem_limit_bytes=VMEM_LIMIT),
        name="diff_attn",
    )(*(t.reshape(1, -1) for t in (lam_q1, lam_k1, lam_q2, lam_k2, subln_w)),
      qt, k, vt, gate)


def _out_proj_kernel(yr_ref, yd_ref, wr_ref, wd_ref, g_ref, gate_ref, x_ref, o_ref):
    y = _dot(yr_ref[0], wr_ref[...]) + _dot(yd_ref[0], wd_ref[...])
    y = y * lax.rsqrt(jnp.mean(y * y, axis=-1, keepdims=True) + RMS_EPS) * g_ref[...]
    o_ref[0] = x_ref[0] + gate_ref[0] * y


def _out_projection(y_rw, y_da, w_out_bf16, g_post, gate, x):
    batch, seq, _ = x.shape
    tm = OUT_ROWS
    half = lambda j: pl.BlockSpec((RW_WIDTH, D_MODEL), lambda b, i: (j, 0),
                                  pipeline_mode=pl.Buffered(1))
    row_spec = lambda width: pl.BlockSpec((1, tm, width), lambda b, i: (b, i, 0))
    return pl.pallas_call(
        _out_proj_kernel,
        grid=(batch, seq // tm),
        in_specs=[
            row_spec(RW_WIDTH), row_spec(DA_WIDTH), half(0), half(1),
            pl.BlockSpec((1, D_MODEL), lambda b, i: (0, 0)),
            pl.BlockSpec((1, 1, D_MODEL), lambda b, i: (b, 0, 0)),
            row_spec(D_MODEL),
        ],
        out_specs=row_spec(D_MODEL),
        out_shape=jax.ShapeDtypeStruct(x.shape, x.dtype),
        compiler_params=pltpu.CompilerParams(
            dimension_semantics=("parallel", "parallel"),
            vmem_limit_bytes=VMEM_LIMIT),
        name="out_proj",
    )(y_rw, y_da, w_out_bf16, w_out_bf16, g_post, gate, x)


def kernel(x, c, w_ada, b_ada, g_pre, g_post, w_in, w_out, rw_mu, rw_w0, rw_w_w2,
           rw_a0, rw_w_a2, rw_k_k, rw_k_a, rw_r_k, rw_lnx_w, rw_lnx_b,
           da_lam_q1, da_lam_k1, da_lam_q2, da_lam_k2, da_subln_w):
    batch = x.shape[0]
    depth = w_in.shape[0]
    for l in range(depth):
        lambda_init = 0.8 - 0.6 * math.exp(-0.3 * l)
        mod = _modulation(c, w_ada[l], b_ada[l]).reshape(batch, 3, 1, D_MODEL)
        shift, scale, gate = mod[:, 0], mod[:, 1], mod[:, 2]
        feat, rw_gate, dk, dg, dqt, dvt = _in_projection(
            x, shift, scale, g_pre[l].reshape(1, D_MODEL), rw_mu[l].reshape(1, RW_SHIFTED),
            w_in[l].astype(BF16))
        y_rw = _rwkv_branch(feat, rw_gate, rw_w0[l], rw_w_w2[l], rw_a0[l],
                            rw_w_a2[l], rw_k_k[l], rw_k_a[l], rw_r_k[l],
                            rw_lnx_w[l], rw_lnx_b[l])
        y_da = _diff_attention(dqt, dk, dvt, dg, da_lam_q1[l], da_lam_k1[l],
                               da_lam_q2[l], da_lam_k2[l], da_subln_w[l], lambda_init)
        x = _out_projection(y_rw, y_da, w_out[l].astype(BF16),
                            g_post[l].reshape(1, D_MODEL), gate, x)
    return x
```

```python
import functools
import math

import jax
import jax.numpy as jnp
from jax import lax
from jax.experimental import pallas as pl
from jax.experimental.pallas import tpu as pltpu

D_MODEL = 1024
SEQ = 4096
CHUNK = 64
RW_WIDTH = 512
RW_HEAD = 64
RW_HEADS = RW_WIDTH // RW_HEAD
RW_RANK = 64
RW_SHIFTED = 3 * RW_WIDTH + 2 * RW_RANK
RW_COLS = RW_SHIFTED + RW_WIDTH
DA_WIDTH = 512
DA_HEADS = 4
DA_VDIM = DA_WIDTH // DA_HEADS
DA_QKDIM = DA_VDIM // 2
D_IN = RW_COLS + 4 * DA_WIDTH
RMS_EPS = 1e-6
GN_EPS = 64e-5
SUBLN_EPS = 1e-5

IN_ROWS = 512
IN_SUBTILES = 2
OUT_ROWS = 1024
RW_CHUNK = 64
RW_STEP_CHUNKS = 2
RW_GROUP = 2 * RW_HEAD
ATT_TILE = 512
ONES_ROWS = 16
ATT_STEPS = 2
VMEM_LIMIT = 48 * 1024 * 1024

F32 = jnp.float32
BF16 = jnp.bfloat16
HIGHEST = lax.Precision.HIGHEST


def _silu(t):
    return t * jax.nn.sigmoid(t)


def _dot(a, b, precision=None):
    return jnp.dot(a, b, preferred_element_type=F32, precision=precision)


def _dot_nt(a, b):
    return lax.dot_general(a, b, (((1,), (1,)), ((), ())), preferred_element_type=F32)


def _dot_tn(a, b):
    return lax.dot_general(a, b, (((0,), (0,)), ((), ())), preferred_element_type=F32)


def _split_dot(x, w_exact, terms, *, x_is_lhs):
    acc = None
    for _ in range(terms):
        piece = x.astype(BF16)
        part = _dot(piece, w_exact) if x_is_lhs else _dot(w_exact, piece)
        acc = part if acc is None else acc + part
        x = x - piece.astype(F32)
    return acc


def _mod_kernel(c_ref, w_ref, b_ref, o_ref):
    o_ref[...] = _dot(_silu(c_ref[...]), w_ref[...], HIGHEST) + b_ref[...]


def _modulation(c, w_ada, b_ada):
    batch = c.shape[0]
    return pl.pallas_call(
        _mod_kernel,
        grid=(3,),
        in_specs=[
            pl.BlockSpec((batch, D_MODEL), lambda j: (0, 0)),
            pl.BlockSpec((D_MODEL, D_MODEL), lambda j: (0, j)),
            pl.BlockSpec((1, D_MODEL), lambda j: (0, j)),
        ],
        out_specs=pl.BlockSpec((batch, D_MODEL), lambda j: (0, j)),
        out_shape=jax.ShapeDtypeStruct((batch, 3 * D_MODEL), F32),
        compiler_params=pltpu.CompilerParams(vmem_limit_bytes=VMEM_LIMIT),
        name="adaln_mod",
    )(c, w_ada, b_ada.reshape(1, 3 * D_MODEL))


_Q_COLS = (RW_COLS, RW_COLS + DA_WIDTH)
_V_COLS = (RW_COLS + 2 * DA_WIDTH, RW_COLS + 3 * DA_WIDTH)
_ROW_SEGMENTS = (
    (0, RW_SHIFTED),
    (RW_SHIFTED, RW_COLS),
    (RW_COLS + DA_WIDTH, RW_COLS + 2 * DA_WIDTH),
    (RW_COLS + 3 * DA_WIDTH, D_IN),
)
_ROW_WIDTHS = tuple(hi - lo for lo, hi in _ROW_SEGMENTS)
_ROW_DTYPES = (F32, BF16, BF16, BF16)
Q_SCALE = math.log2(math.e) / math.sqrt(DA_QKDIM)


def _in_proj_kernel(x_ref, shift_ref, scale_ref, g_ref, mu_ref, w_ref, *rest):
    out_refs, wt_ref, ubuf = rest[:-2], rest[-2], rest[-1]
    tm = x_ref.shape[1]

    @pl.when((pl.program_id(0) == 0) & (pl.program_id(1) == 0))
    def _():
        wt_ref[0:DA_WIDTH, :] = w_ref[:, _Q_COLS[0]:_Q_COLS[1]].T
        wt_ref[DA_WIDTH:, :] = w_ref[:, _V_COLS[0]:_V_COLS[1]].T

    @pl.when(pl.program_id(1) == 0)
    def _():
        ubuf[0:8, :] = jnp.zeros((8, RW_SHIFTED), F32)

    qt_ref, vt_ref = out_refs[len(_ROW_SEGMENTS):]
    sub = tm // IN_SUBTILES
    for rows in (slice(t * sub, (t + 1) * sub) for t in range(IN_SUBTILES)):
        x = x_ref[0, rows, :]
        y = x * lax.rsqrt(jnp.mean(x * x, axis=-1, keepdims=True) + RMS_EPS) * g_ref[...]
        h = (y * (1.0 + scale_ref[0]) + shift_ref[0]).astype(BF16)
        for (lo, hi), o_ref in zip(_ROW_SEGMENTS, out_refs):
            u = _dot(h, w_ref[:, lo:hi])
            if lo == 0:
                ubuf[8 + rows.start:8 + rows.stop, :] = u
                prev = ubuf[7 + rows.start:7 + rows.stop, :]
                u = u + (prev - u) * mu_ref[...]
            o_ref[0, rows, :] = u.astype(o_ref.dtype)
        qt = _dot_nt(wt_ref[0:DA_WIDTH, :], h) * Q_SCALE
        vt = _dot_nt(wt_ref[DA_WIDTH:, :], h)
        qt_ref[0, :, 0, :, rows] = qt.astype(BF16).reshape(DA_HEADS, DA_VDIM, sub)
        vt_ref[0, :, 0, :, rows] = vt.astype(BF16).reshape(DA_HEADS, DA_VDIM, sub)
    ubuf[0:8, :] = ubuf[tm:tm + 8, :]


def _in_projection(x, shift, scale, g_pre, mu, w_in_bf16):
    batch, seq, _ = x.shape
    tm = IN_ROWS
    row_spec = lambda width: pl.BlockSpec((1, tm, width), lambda b, i: (b, i, 0))
    vec_spec = pl.BlockSpec((1, 1, D_MODEL), lambda b, i: (b, 0, 0))
    per_tile = ATT_TILE // tm
    t_spec = pl.BlockSpec((1, DA_HEADS, 1, DA_VDIM, tm),
                          lambda b, i: (b, 0, i // per_tile, 0, i % per_tile))
    t_shape = jax.ShapeDtypeStruct((batch, DA_HEADS, seq // ATT_TILE, DA_VDIM, ATT_TILE), BF16)
    return pl.pallas_call(
        _in_proj_kernel,
        grid=(batch, seq // tm),
        in_specs=[
            row_spec(D_MODEL), vec_spec, vec_spec,
            pl.BlockSpec((1, D_MODEL), lambda b, i: (0, 0)),
            pl.BlockSpec((1, RW_SHIFTED), lambda b, i: (0, 0)),
            pl.BlockSpec((D_MODEL, D_IN), lambda b, i: (0, 0), pipeline_mode=pl.Buffered(1)),
        ],
        out_specs=[row_spec(w) for w in _ROW_WIDTHS] + [t_spec, t_spec],
        out_shape=[jax.ShapeDtypeStruct((batch, seq, w), dt)
                   for w, dt in zip(_ROW_WIDTHS, _ROW_DTYPES)] + [t_shape, t_shape],
        scratch_shapes=[pltpu.VMEM((2 * DA_WIDTH, D_MODEL), BF16),
                        pltpu.VMEM((tm + 8, RW_SHIFTED), F32)],
        compiler_params=pltpu.CompilerParams(
            dimension_semantics=("arbitrary", "arbitrary"),
            vmem_limit_bytes=VMEM_LIMIT),
        name="in_proj",
    )(x, shift, scale, g_pre, mu, w_in_bf16)


def _rwkv_kernel(feat_ref, gate_ref, w0_ref, ww2_ref, a0_ref, wa2_ref,
                 kk_ref, ka_ref, rk_ref, lnw_ref, lnb_ref, y_ref, state):
    C, W, N, L = RW_CHUNK, RW_WIDTH, RW_HEAD, RW_GROUP
    heads = L // N
    batch, rows = feat_ref.shape[0], feat_ref.shape[1]
    chunks = rows // C
    ci = pl.program_id(0)

    @pl.when(ci == 0)
    def _():
        state[...] = jnp.zeros(state.shape, F32)

    iota = lambda shape, axis: lax.broadcasted_iota(jnp.int32, shape, axis)
    row2, col2 = iota((2 * C, L), 0), iota((2 * C, L), 1) & (N - 1)
    tri = ((row2 & (C - 1)) > col2) | ((row2 >= C) & ((row2 & (C - 1)) == col2))
    r1, c1 = iota((C, L), 0), iota((C, L), 1) & (N - 1)
    eye = (r1 == c1).astype(F32)
    levels = []
    for bit in range(C.bit_length() - 1):
        levels.append(((r1 >> (bit + 1)) == (c1 >> (bit + 1)))
                      & (((r1 >> bit) & 1) == 1) & (((c1 >> bit) & 1) == 0))
    head_bits = N.bit_length() - 1
    same_head = (iota((L, L), 0) >> head_bits) == (iota((L, L), 1) >> head_bits)
    seg = same_head.astype(BF16)
    cum = (iota((C, C), 0) >= iota((C, C), 1)).astype(BF16)

    def block_diag(t, keep=same_head):
        return jnp.where(keep, jnp.concatenate([t] * heads, axis=0), 0.0).astype(BF16)

    level_keep = [same_head & jnp.concatenate([level] * heads, axis=0) for level in levels]

    def head_sums(t):
        return jnp.concatenate(
            [_dot(t[:, g * L:(g + 1) * L].astype(BF16), seg) for g in range(W // L)], axis=-1)

    roots = []
    for b in range(batch):
        feat = feat_ref[b]
        r, k, v = feat[:, 0:W], feat[:, W:2 * W], feat[:, 2 * W:3 * W]
        w_lo = feat[:, 3 * W:3 * W + RW_RANK]
        a_lo = feat[:, 3 * W + RW_RANK:]
        z = w0_ref[...] + _dot(jnp.tanh(w_lo).astype(BF16), ww2_ref[...])
        logw = -math.exp(-0.5) * jax.nn.sigmoid(z)
        a_pre = a0_ref[...] + _dot(a_lo.astype(BF16), wa2_ref[...])
        kk_raw = k * kk_ref[...]
        norm2 = head_sums(kk_raw * kk_raw)
        cs = [_split_dot(logw[ch * C:(ch + 1) * C], cum, 2, x_is_lhs=False)
              for ch in range(chunks)]
        roots.append((r, k, v, logw, a_pre, kk_raw, norm2, cs))

    prep = {}

    def prepare(ch, anchor=None):
        cr = slice(ch * C, (ch + 1) * C)
        zero = 0.0
        if anchor is not None:
            bits = pltpu.bitcast(anchor[0:1, :], jnp.uint32)
            half_word = jnp.uint32(16)
            bits = lax.shift_right_logical(lax.shift_right_logical(bits, half_word), half_word)
            zero = jnp.concatenate([pltpu.bitcast(bits, F32)] * (W // L), axis=1)
        grs = []
        for b in range(batch):
            r, k, v, logw, a_pre, kk_raw, norm2, cs_all = roots[b]
            r, k, v, logw, cs = r[cr], k[cr], v[cr], logw[cr], cs_all[ch] + zero
            a = jax.nn.sigmoid(a_pre[cr] + zero)
            kk = kk_raw[cr] * lax.rsqrt(jnp.maximum(norm2[cr], 1e-24))
            bb = a * kk
            k2 = k * (1.0 + (a - 1.0) * ka_ref[...])
            cs_last = cs[C - 1:C, :]
            g_all = jnp.exp(cs_last)
            g_inv = jnp.exp(-cs)
            g_end = jnp.exp(cs_last - cs)
            kk_t = (kk * jnp.exp(cs - logw)).astype(BF16)
            r_t = (r * jnp.exp(cs)).astype(BF16)
            k_inv, b_inv = k2 * g_inv, bb * g_inv
            k_end, nb_end = (k2 * g_end).astype(BF16), (-(bb * g_end)).astype(BF16)
            prep[b, ch] = (r, k2, v)
            for g in range(W // L):
                gl = slice(g * L, (g + 1) * L)
                grs.append(dict(
                    b=b, g=g, g_all=g_all[:, gl], v=v[:, gl],
                    lhs=jnp.concatenate([kk_t[:, gl], r_t[:, gl]], axis=0),
                    rhs=jnp.concatenate([block_diag(b_inv[:, gl]), block_diag(k_inv[:, gl])],
                                        axis=0),
                    end=jnp.concatenate([k_end[:, gl], nb_end[:, gl]], axis=0)))
        return grs

    def output_stage(ch, outs):
        cr = slice(ch * C, (ch + 1) * C)
        o = [jnp.concatenate([outs[b, g] for g in range(W // L)], axis=-1) for b in range(batch)]
        mean = [head_sums(t) * (1.0 / N) for t in o]
        yield
        d = [t - m for t, m in zip(o, mean)]
        var = [head_sums(t * t) * (1.0 / N) for t in d]
        bonus = [head_sums(prep[b, ch][0] * prep[b, ch][1] * rk_ref[...]) * prep[b, ch][2]
                 for b in range(batch)]
        yield
        for b in range(batch):
            t = d[b] * lax.rsqrt(var[b] + GN_EPS) * lnw_ref[...] + lnb_ref[...]
            y_ref[b, cr, :] = ((t + bonus[b]) * _silu(gate_ref[b, cr, :].astype(F32))
                               ).astype(y_ref.dtype)

    carried = {(b, g): state[b, g] for b in range(batch) for g in range(W // L)}
    pending = iter(())
    nxt = prepare(0)
    for ch in range(chunks):
        grs = nxt
        for gr in grs:
            m = _dot_nt(gr["lhs"], gr["rhs"])
            gr["a_b"] = jnp.where(tri, m[:, :L], 0.0)
            gr["a_k"] = jnp.where(tri, m[:, L:], 0.0).astype(BF16)
            gr["t"] = eye - jnp.where(levels[0], gr["a_b"][:C], 0.0)
        for gr in grs:
            gr["st"] = carried[gr["b"], gr["g"]]
            gr["x"] = _dot(jnp.concatenate([gr["a_k"], gr["lhs"]], axis=1),
                           jnp.concatenate([block_diag(gr["v"]), block_diag(gr["st"]).T], axis=0))
        if ch + 1 < chunks:
            nxt = prepare(ch + 1, anchor=grs[-1]["a_b"])
        for bit, keep in enumerate(level_keep[1:], start=1):
            size = 1 << bit
            packed = size >= 16
            odd = [slice(lo, lo + size) for lo in range(size, C, 2 * size)] if packed \
                else [slice(0, C)]
            for gr in grs:
                rows_t = jnp.concatenate([gr["t"][sl] for sl in odd], axis=0)
                gr["inner"] = _dot(rows_t.astype(BF16),
                                   block_diag(gr["a_b"][:C], keep)).astype(BF16)
            for gr in grs:
                t, upd = gr["t"], _dot(gr["inner"], block_diag(gr["t"]))
                pieces, pos = [], 0
                for n, sl in enumerate(odd):
                    pieces += [t[pos:sl.start], t[sl] - upd[n * size:n * size + sl.stop - sl.start]]
                    pos = sl.stop
                pieces.append(t[pos:C])
                gr["t"] = jnp.concatenate([p for p in pieces if p.shape[0]], axis=0)
            next(pending, None)
        for gr in grs:
            gr["uu"] = _dot(gr["t"].astype(BF16), block_diag(gr["x"][:C]))
        outs = {}
        for gr in grs:
            outs[gr["b"], gr["g"]] = gr["x"][C:] - _dot(gr["a_b"][C:].astype(BF16),
                                                       block_diag(gr["uu"]))
        for gr in grs:
            vu = jnp.concatenate([gr["v"], gr["uu"]], axis=0).astype(BF16)
            delta = jnp.where(same_head, _dot_tn(vu, gr["end"]), 0.0)
            carried[gr["b"], gr["g"]] = gr["st"] * gr["g_all"] + sum(
                delta[h * N:(h + 1) * N] for h in range(heads))
        for _ in pending:
            pass
        pending = output_stage(ch, outs)
    for (b, g), st in carried.items():
        state[b, g] = st
    for _ in pending:
        pass


def _rwkv_branch(feat, gate, w0, w_w2, a0, w_a2, k_k, k_a, r_k, lnx_w, lnx_b):
    batch, seq, _ = feat.shape
    rows = RW_CHUNK * RW_STEP_CHUNKS
    full = lambda shape: pl.BlockSpec(shape, lambda i: (0,) * len(shape))
    vec = lambda t: t.reshape(1, -1)
    return pl.pallas_call(
        _rwkv_kernel,
        grid=(seq // rows,),
        in_specs=[
            pl.BlockSpec((batch, rows, RW_SHIFTED), lambda i: (0, i, 0)),
            pl.BlockSpec((batch, rows, RW_WIDTH), lambda i: (0, i, 0)),
            full((1, RW_WIDTH)), full((RW_RANK, RW_WIDTH)),
            full((1, RW_WIDTH)), full((RW_RANK, RW_WIDTH)),
            full((1, RW_WIDTH)), full((1, RW_WIDTH)), full((1, RW_WIDTH)),
            full((1, RW_WIDTH)), full((1, RW_WIDTH)),
        ],
        out_specs=pl.BlockSpec((batch, rows, RW_WIDTH), lambda i: (0, i, 0)),
        out_shape=jax.ShapeDtypeStruct((batch, seq, RW_WIDTH), BF16),
        scratch_shapes=[
            pltpu.VMEM((batch, RW_WIDTH // RW_GROUP, RW_HEAD, RW_GROUP), F32),
        ],
        compiler_params=pltpu.CompilerParams(
            dimension_semantics=("arbitrary",),
            vmem_limit_bytes=VMEM_LIMIT),
        name="rwkv7_chunked",
    )(feat, gate, vec(w0), w_w2.astype(BF16), vec(a0), w_a2.astype(BF16),
      vec(k_k), vec(k_a), vec(r_k), vec(lnx_w), vec(lnx_b))


def _attn_kernel(lq1_ref, lk1_ref, lq2_ref, lk2_ref, sw_ref, qt_ref, k_ref, vt_ref,
                 g_ref, o_ref, s_buf, p_buf, acc_buf, *, lambda_init):
    T, HALF = ATT_TILE, ATT_TILE // 2
    tiles = k_ref.shape[1] // T

    comp = lax.broadcasted_iota(jnp.int32, (DA_VDIM, T), 0) < DA_QKDIM
    key = lax.broadcasted_iota(jnp.int32, (HALF, HALF), 0)
    query = lax.broadcasted_iota(jnp.int32, (HALF, HALF), 1)
    visible = (query // CHUNK) >= (key // CHUNK)
    lam = (jnp.exp(jnp.sum(lq1_ref[...] * lk1_ref[...], axis=-1, keepdims=True))
           - jnp.exp(jnp.sum(lq2_ref[...] * lk2_ref[...], axis=-1, keepdims=True))
           + lambda_init)

    for step in range(ATT_STEPS):
        order = [i for k in range(step, tiles // 2, ATT_STEPS) for i in (tiles - 1 - k, k)]
        pl.when(pl.program_id(2) == step)(functools.partial(
            _attn_tasks, [(i, j) for i in order for j in range(i + 1)],
            comp, visible, lam, sw_ref, qt_ref, k_ref, vt_ref, g_ref, o_ref,
            s_buf, p_buf, acc_buf, lambda_init))


def _attn_tasks(tasks, comp, visible, lam, sw_ref, qt_ref, k_ref, vt_ref, g_ref, o_ref,
                s_buf, p_buf, acc_buf, lambda_init):
    T, HALF = ATT_TILE, ATT_TILE // 2
    qt_parts = {}

    def qt_c(i):
        if i not in qt_parts:
            qt = qt_ref[0, 0, i]
            zero = jnp.zeros_like(qt)
            qt_parts[i] = (jnp.where(comp, qt, zero), jnp.where(comp, zero, qt))
        return qt_parts[i]

    def colmax(s):
        return jnp.max(s, axis=0, keepdims=True)

    def scores(n):
        i, j = tasks[n]
        kb = k_ref[0, j * T:(j + 1) * T, :]
        blk_max = []
        for c in range(2):
            q = qt_c(i)[c]
            if j < i:
                s = _dot(kb, q)
                s_buf[n % 2, c] = s
                blk_max.append(colmax(s))
            else:
                s_r = _dot(kb, q[:, HALF:])
                s_b = jnp.where(visible, s_r[HALF:], -jnp.inf)
                s_l = jnp.where(visible, _dot(kb[:HALF], q[:, :HALF]), -jnp.inf)
                s_buf[n % 2, c, :HALF, HALF:] = s_r[:HALF]
                s_buf[n % 2, c, HALF:, HALF:] = s_b
                s_buf[n % 2, c, :HALF, :HALF] = s_l
                blk_max.append(jnp.concatenate(
                    [colmax(s_l), jnp.maximum(colmax(s_r[:HALF]), colmax(s_b))], axis=1))
        return blk_max

    def softmax(n, blk_maxes, maxes):
        i, j = tasks[n]
        new, alphas = [], []
        for c in range(2):
            if j == 0:
                m_new, alpha = blk_maxes[c], None
            else:
                m_new = jnp.maximum(maxes[c], blk_maxes[c])
                alpha = jnp.exp2(maxes[c] - m_new)
            prob = lambda s, m: jnp.exp2((s - m).astype(BF16))
            if j < i:
                p_buf[n % 2, c] = prob(s_buf[n % 2, c], m_new)
            else:
                left, right = m_new[:, :HALF], m_new[:, HALF:]
                p_buf[n % 2, c, :HALF, :HALF] = prob(s_buf[n % 2, c, :HALF, :HALF], left)
                p_buf[n % 2, c, :HALF, HALF:] = prob(s_buf[n % 2, c, :HALF, HALF:], right)
                p_buf[n % 2, c, HALF:, HALF:] = prob(s_buf[n % 2, c, HALF:, HALF:], right)
            new.append(m_new)
            alphas.append(alpha)
        return new, alphas

    ones_rows = (lax.broadcasted_iota(jnp.int32, (ONES_ROWS, T), 0) == 0).astype(BF16)

    def values(n, alphas):
        i, j = tasks[n]
        vtb = jnp.concatenate([vt_ref[0, 0, j], ones_rows], axis=0)
        for c in range(2):
            if j < i:
                pv = _dot(vtb, p_buf[n % 2, c])
            else:
                pv = jnp.concatenate(
                    [_dot(vtb[:, :HALF], p_buf[n % 2, c, :HALF, :HALF]),
                     _dot(vtb, p_buf[n % 2, c, :, HALF:])], axis=1)
            acc_buf[i % 2, c] = pv if j == 0 else alphas[c] * acc_buf[i % 2, c] + pv

    def finish(i):
        num0, num1 = acc_buf[i % 2, 0, :DA_VDIM], acc_buf[i % 2, 1, :DA_VDIM]
        l0 = acc_buf[i % 2, 0, DA_VDIM:DA_VDIM + 1]
        l1 = acc_buf[i % 2, 1, DA_VDIM:DA_VDIM + 1]
        o = (num0 * (1.0 / l0) - num1 * (lam / l1)).T
        o = o * lax.rsqrt(jnp.mean(o * o, axis=-1, keepdims=True) + SUBLN_EPS)
        o = o * sw_ref[...] * (1.0 - lambda_init)
        rows = slice(i * T, (i + 1) * T)
        o_ref[0, rows, :] = (o * _silu(g_ref[0, rows, :].astype(F32))).astype(o_ref.dtype)

    blk_max = scores(0)
    maxes, pending = None, None
    for n, (i, j) in enumerate(tasks):
        next_max = scores(n + 1) if n + 1 < len(tasks) else None
        if pending is not None:
            values(pending[0], pending[1])
            if pending[2]:
                finish(tasks[pending[0]][0])
        maxes, alphas = softmax(n, blk_max, maxes)
        pending = (n, alphas, j == i)
        blk_max = next_max
    values(pending[0], pending[1])
    finish(tasks[-1][0])


def _diff_attention(qt, k, vt, gate, lam_q1, lam_k1, lam_q2, lam_k2, subln_w, lambda_init):
    batch, seq, _ = k.shape
    T = ATT_TILE
    small = lambda n: pl.BlockSpec((1, n), lambda b, h, s: (0, 0))
    rows = pl.BlockSpec((1, seq, DA_VDIM), lambda b, h, s: (b, 0, h))
    cols = pl.BlockSpec((1, 1, seq // T, DA_VDIM, T), lambda b, h, s: (b, h, 0, 0, 0))
    return pl.pallas_call(
        functools.partial(_attn_kernel, lambda_init=lambda_init),
        grid=(batch, DA_HEADS, ATT_STEPS),
        in_specs=[small(DA_QKDIM)] * 4 + [small(DA_VDIM), cols, rows, cols, rows],
        out_specs=rows,
        out_shape=jax.ShapeDtypeStruct((batch, seq, DA_WIDTH), BF16),
        scratch_shapes=[pltpu.VMEM((2, 2, T, T), F32),
                        pltpu.VMEM((2, 2, T, T), BF16),
                        pltpu.VMEM((2, 2, DA_VDIM + ONES_ROWS, T), F32)],
        compiler_params=pltpu.CompilerParams(
            dimension_semantics=("parallel", "parallel", "arbitrary"),
            vmem_limit_bytes=VMEM_LIMIT),
        name="diff_attn",
    )(*(t.reshape(1, -1) for t in (lam_q1, lam_k1, lam_q2, lam_k2, subln_w)),
      qt, k, vt, gate)


def _out_proj_kernel(yr_ref, yd_ref, wr_ref, wd_ref, g_ref, gate_ref, x_ref, o_ref):
    y = _dot(yr_ref[0], wr_ref[...]) + _dot(yd_ref[0], wd_ref[...])
    y = y * lax.rsqrt(jnp.mean(y * y, axis=-1, keepdims=True) + RMS_EPS) * g_ref[...]
    o_ref[0] = x_ref[0] + gate_ref[0] * y


def _out_projection(y_rw, y_da, w_out_bf16, g_post, gate, x):
    batch, seq, _ = x.shape
    tm = OUT_ROWS
    half = lambda j: pl.BlockSpec((RW_WIDTH, D_MODEL), lambda b, i: (j, 0),
                                  pipeline_mode=pl.Buffered(1))
    row_spec = lambda width: pl.BlockSpec((1, tm, width), lambda b, i: (b, i, 0))
    return pl.pallas_call(
        _out_proj_kernel,
        grid=(batch, seq // tm),
        in_specs=[
            row_spec(RW_WIDTH), row_spec(DA_WIDTH), half(0), half(1),
            pl.BlockSpec((1, D_MODEL), lambda b, i: (0, 0)),
            pl.BlockSpec((1, 1, D_MODEL), lambda b, i: (b, 0, 0)),
            row_spec(D_MODEL),
        ],
        out_specs=row_spec(D_MODEL),
        out_shape=jax.ShapeDtypeStruct(x.shape, x.dtype),
        compiler_params=pltpu.CompilerParams(
            dimension_semantics=("parallel", "parallel"),
            vmem_limit_bytes=VMEM_LIMIT),
        name="out_proj",
    )(y_rw, y_da, w_out_bf16, w_out_bf16, g_post, gate, x)


def kernel(x, c, w_ada, b_ada, g_pre, g_post, w_in, w_out, rw_mu, rw_w0, rw_w_w2,
           rw_a0, rw_w_a2, rw_k_k, rw_k_a, rw_r_k, rw_lnx_w, rw_lnx_b,
           da_lam_q1, da_lam_k1, da_lam_q2, da_lam_k2, da_subln_w):
    batch = x.shape[0]
    depth = w_in.shape[0]
    for l in range(depth):
        lambda_init = 0.8 - 0.6 * math.exp(-0.3 * l)
        mod = _modulation(c, w_ada[l], b_ada[l]).reshape(batch, 3, 1, D_MODEL)
        shift, scale, gate = mod[:, 0], mod[:, 1], mod[:, 2]
        feat, rw_gate, dk, dg, dqt, dvt = _in_projection(
            x, shift, scale, g_pre[l].reshape(1, D_MODEL), rw_mu[l].reshape(1, RW_SHIFTED),
            w_in[l].astype(BF16))
        y_rw = _rwkv_branch(feat, rw_gate, rw_w0[l], rw_w_w2[l], rw_a0[l],
                            rw_w_a2[l], rw_k_k[l], rw_k_a[l], rw_r_k[l],
                            rw_lnx_w[l], rw_lnx_b[l])
        y_da = _diff_attention(dqt, dk, dvt, dg, da_lam_q1[l], da_lam_k1[l],
                               da_lam_q2[l], da_lam_k2[l], da_subln_w[l], lambda_init)
        x = _out_projection(y_rw, y_da, w_out[l].astype(BF16),
                            g_post[l].reshape(1, D_MODEL), gate, x)
    return x
```

```python
import functools
import math

import jax
import jax.numpy as jnp
from jax import lax
from jax.experimental import pallas as pl
from jax.experimental.pallas import tpu as pltpu

D_MODEL = 1024
SEQ = 4096
CHUNK = 64
RW_WIDTH = 512
RW_HEAD = 64
RW_HEADS = RW_WIDTH // RW_HEAD
RW_RANK = 64
RW_SHIFTED = 3 * RW_WIDTH + 2 * RW_RANK
RW_COLS = RW_SHIFTED + RW_WIDTH
DA_WIDTH = 512
DA_HEADS = 4
DA_VDIM = DA_WIDTH // DA_HEADS
DA_QKDIM = DA_VDIM // 2
D_IN = RW_COLS + 4 * DA_WIDTH
RMS_EPS = 1e-6
GN_EPS = 64e-5
SUBLN_EPS = 1e-5

IN_ROWS = 512
IN_SUBTILES = 2
RW_CHUNK = 64
RW_STEP_CHUNKS = 2
RW_GROUP = 2 * RW_HEAD
ATT_TILE = 512
ONES_ROWS = 16
ATT_STEPS = 2
VMEM_LIMIT = 48 * 1024 * 1024

F32 = jnp.float32
BF16 = jnp.bfloat16
HIGHEST = lax.Precision.HIGHEST


def _silu(t):
    return t * jax.nn.sigmoid(t)


def _dot(a, b, precision=None):
    return jnp.dot(a, b, preferred_element_type=F32, precision=precision)


def _dot_nt(a, b):
    return lax.dot_general(a, b, (((1,), (1,)), ((), ())), preferred_element_type=F32)


def _dot_tn(a, b):
    return lax.dot_general(a, b, (((0,), (0,)), ((), ())), preferred_element_type=F32)


def _split_dot(x, w_exact, terms, *, x_is_lhs):
    acc = None
    for _ in range(terms):
        piece = x.astype(BF16)
        part = _dot(piece, w_exact) if x_is_lhs else _dot(w_exact, piece)
        acc = part if acc is None else acc + part
        x = x - piece.astype(F32)
    return acc


def _mod_kernel(c_ref, w_ref, b_ref, o_ref):
    o_ref[...] = _dot(_silu(c_ref[...]), w_ref[...], HIGHEST) + b_ref[...]


def _modulation(c, w_ada, b_ada):
    batch = c.shape[0]
    return pl.pallas_call(
        _mod_kernel,
        grid=(3,),
        in_specs=[
            pl.BlockSpec((batch, D_MODEL), lambda j: (0, 0)),
            pl.BlockSpec((D_MODEL, D_MODEL), lambda j: (0, j)),
            pl.BlockSpec((1, D_MODEL), lambda j: (0, j)),
        ],
        out_specs=pl.BlockSpec((batch, D_MODEL), lambda j: (0, j)),
        out_shape=jax.ShapeDtypeStruct((batch, 3 * D_MODEL), F32),
        compiler_params=pltpu.CompilerParams(vmem_limit_bytes=VMEM_LIMIT),
        name="adaln_mod",
    )(c, w_ada, b_ada.reshape(1, 3 * D_MODEL))


_Q_COLS = (RW_COLS, RW_COLS + DA_WIDTH)
_V_COLS = (RW_COLS + 2 * DA_WIDTH, RW_COLS + 3 * DA_WIDTH)
_ROW_SEGMENTS = (
    (0, RW_SHIFTED),
    (RW_SHIFTED, RW_COLS),
    (RW_COLS + DA_WIDTH, RW_COLS + 2 * DA_WIDTH),
    (RW_COLS + 3 * DA_WIDTH, D_IN),
)
_ROW_WIDTHS = tuple(hi - lo for lo, hi in _ROW_SEGMENTS)
_ROW_DTYPES = (F32, BF16, BF16, BF16)
Q_SCALE = math.log2(math.e) / math.sqrt(DA_QKDIM)


def _in_proj_kernel(x_ref, shift_ref, scale_ref, g_ref, mu_ref, w_ref, *rest):
    out_refs, wt_ref, ubuf = rest[:-2], rest[-2], rest[-1]
    tm = x_ref.shape[1]

    @pl.when((pl.program_id(0) == 0) & (pl.program_id(1) == 0))
    def _():
        wt_ref[0:DA_WIDTH, :] = w_ref[:, _Q_COLS[0]:_Q_COLS[1]].T
        wt_ref[DA_WIDTH:, :] = w_ref[:, _V_COLS[0]:_V_COLS[1]].T

    @pl.when(pl.program_id(1) == 0)
    def _():
        ubuf[0:8, :] = jnp.zeros((8, RW_SHIFTED), F32)

    qt_ref, vt_ref = out_refs[len(_ROW_SEGMENTS):]
    sub = tm // IN_SUBTILES
    for rows in (slice(t * sub, (t + 1) * sub) for t in range(IN_SUBTILES)):
        x = x_ref[0, rows, :]
        y = x * lax.rsqrt(jnp.mean(x * x, axis=-1, keepdims=True) + RMS_EPS) * g_ref[...]
        h = (y * (1.0 + scale_ref[0]) + shift_ref[0]).astype(BF16)
        for (lo, hi), o_ref in zip(_ROW_SEGMENTS, out_refs):
            u = _dot(h, w_ref[:, lo:hi])
            if lo == 0:
                ubuf[8 + rows.start:8 + rows.stop, :] = u
                prev = ubuf[7 + rows.start:7 + rows.stop, :]
                u = u + (prev - u) * mu_ref[...]
            o_ref[0, rows, :] = u.astype(o_ref.dtype)
        qt = _dot_nt(wt_ref[0:DA_WIDTH, :], h) * Q_SCALE
        vt = _dot_nt(wt_ref[DA_WIDTH:, :], h)
        qt_ref[0, :, 0, :, rows] = qt.astype(BF16).reshape(DA_HEADS, DA_VDIM, sub)
        vt_ref[0, :, 0, :, rows] = vt.astype(BF16).reshape(DA_HEADS, DA_VDIM, sub)
    ubuf[0:8, :] = ubuf[tm:tm + 8, :]


def _in_projection(x, shift, scale, g_pre, mu, w_in_bf16):
    batch, seq, _ = x.shape
    tm = IN_ROWS
    row_spec = lambda width: pl.BlockSpec((1, tm, width), lambda b, i: (b, i, 0))
    vec_spec = pl.BlockSpec((1, 1, D_MODEL), lambda b, i: (b, 0, 0))
    per_tile = ATT_TILE // tm
    t_spec = pl.BlockSpec((1, DA_HEADS, 1, DA_VDIM, tm),
                          lambda b, i: (b, 0, i // per_tile, 0, i % per_tile))
    t_shape = jax.ShapeDtypeStruct((batch, DA_HEADS, seq // ATT_TILE, DA_VDIM, ATT_TILE), BF16)
    return pl.pallas_call(
        _in_proj_kernel,
        grid=(batch, seq // tm),
        in_specs=[
            row_spec(D_MODEL), vec_spec, vec_spec,
            pl.BlockSpec((1, D_MODEL), lambda b, i: (0, 0)),
            pl.BlockSpec((1, RW_SHIFTED), lambda b, i: (0, 0)),
            pl.BlockSpec((D_MODEL, D_IN), lambda b, i: (0, 0), pipeline_mode=pl.Buffered(1)),
        ],
        out_specs=[row_spec(w) for w in _ROW_WIDTHS] + [t_spec, t_spec],
        out_shape=[jax.ShapeDtypeStruct((batch, seq, w), dt)
                   for w, dt in zip(_ROW_WIDTHS, _ROW_DTYPES)] + [t_shape, t_shape],
        scratch_shapes=[pltpu.VMEM((2 * DA_WIDTH, D_MODEL), BF16),
                        pltpu.VMEM((tm + 8, RW_SHIFTED), F32)],
        compiler_params=pltpu.CompilerParams(
            dimension_semantics=("arbitrary", "arbitrary"),
            vmem_limit_bytes=VMEM_LIMIT),
        name="in_proj",
    )(x, shift, scale, g_pre, mu, w_in_bf16)


def _rwkv_kernel(feat_ref, gate_ref, w0_ref, ww2_ref, a0_ref, wa2_ref,
                 kk_ref, ka_ref, rk_ref, lnw_ref, lnb_ref,
                 yda_ref, x_ref, wout_ref, gpost_ref, mgate_ref, o_ref, state):
    C, W, N, L = RW_CHUNK, RW_WIDTH, RW_HEAD, RW_GROUP
    heads = L // N
    batch, rows = feat_ref.shape[0], feat_ref.shape[1]
    chunks = rows // C
    ci = pl.program_id(0)

    @pl.when(ci == 0)
    def _():
        state[...] = jnp.zeros(state.shape, F32)

    iota = lambda shape, axis: lax.broadcasted_iota(jnp.int32, shape, axis)
    row2, col2 = iota((2 * C, L), 0), iota((2 * C, L), 1) & (N - 1)
    tri = ((row2 & (C - 1)) > col2) | ((row2 >= C) & ((row2 & (C - 1)) == col2))
    r1, c1 = iota((C, L), 0), iota((C, L), 1) & (N - 1)
    eye = (r1 == c1).astype(F32)
    levels = []
    for bit in range(C.bit_length() - 1):
        levels.append(((r1 >> (bit + 1)) == (c1 >> (bit + 1)))
                      & (((r1 >> bit) & 1) == 1) & (((c1 >> bit) & 1) == 0))
    head_bits = N.bit_length() - 1
    same_head = (iota((L, L), 0) >> head_bits) == (iota((L, L), 1) >> head_bits)
    seg = same_head.astype(BF16)
    cum = (iota((C, C), 0) >= iota((C, C), 1)).astype(BF16)

    def block_diag(t, keep=same_head):
        return jnp.where(keep, jnp.concatenate([t] * heads, axis=0), 0.0).astype(BF16)

    level_keep = [same_head & jnp.concatenate([level] * heads, axis=0) for level in levels]

    def head_sums(t):
        return jnp.concatenate(
            [_dot(t[:, g * L:(g + 1) * L].astype(BF16), seg) for g in range(W // L)], axis=-1)

    roots = []
    for b in range(batch):
        feat = feat_ref[b]
        r, k, v = feat[:, 0:W], feat[:, W:2 * W], feat[:, 2 * W:3 * W]
        w_lo = feat[:, 3 * W:3 * W + RW_RANK]
        a_lo = feat[:, 3 * W + RW_RANK:]
        z = w0_ref[...] + _dot(jnp.tanh(w_lo).astype(BF16), ww2_ref[...])
        logw = -math.exp(-0.5) * jax.nn.sigmoid(z)
        a_pre = a0_ref[...] + _dot(a_lo.astype(BF16), wa2_ref[...])
        kk_raw = k * kk_ref[...]
        norm2 = head_sums(kk_raw * kk_raw)
        cs = [_split_dot(logw[ch * C:(ch + 1) * C], cum, 2, x_is_lhs=False)
              for ch in range(chunks)]
        roots.append((r, k, v, logw, a_pre, kk_raw, norm2, cs))

    prep = {}

    def prepare(ch, anchor=None):
        cr = slice(ch * C, (ch + 1) * C)
        zero = 0.0
        if anchor is not None:
            bits = pltpu.bitcast(anchor[0:1, :], jnp.uint32)
            half_word = jnp.uint32(16)
            bits = lax.shift_right_logical(lax.shift_right_logical(bits, half_word), half_word)
            zero = jnp.concatenate([pltpu.bitcast(bits, F32)] * (W // L), axis=1)
        grs = []
        for b in range(batch):
            r, k, v, logw, a_pre, kk_raw, norm2, cs_all = roots[b]
            r, k, v, logw, cs = r[cr], k[cr], v[cr], logw[cr], cs_all[ch] + zero
            a = jax.nn.sigmoid(a_pre[cr] + zero)
            kk = kk_raw[cr] * lax.rsqrt(jnp.maximum(norm2[cr], 1e-24))
            bb = a * kk
            k2 = k * (1.0 + (a - 1.0) * ka_ref[...])
            cs_last = cs[C - 1:C, :]
            g_all = jnp.exp(cs_last)
            g_inv = jnp.exp(-cs)
            g_end = jnp.exp(cs_last - cs)
            kk_t = (kk * jnp.exp(cs - logw)).astype(BF16)
            r_t = (r * jnp.exp(cs)).astype(BF16)
            k_inv, b_inv = k2 * g_inv, bb * g_inv
            k_end, nb_end = (k2 * g_end).astype(BF16), (-(bb * g_end)).astype(BF16)
            prep[b, ch] = (r, k2, v)
            for g in range(W // L):
                gl = slice(g * L, (g + 1) * L)
                grs.append(dict(
                    b=b, g=g, g_all=g_all[:, gl], v=v[:, gl],
                    lhs=jnp.concatenate([kk_t[:, gl], r_t[:, gl]], axis=0),
                    rhs=jnp.concatenate([block_diag(b_inv[:, gl]), block_diag(k_inv[:, gl])],
                                        axis=0),
                    end=jnp.concatenate([k_end[:, gl], nb_end[:, gl]], axis=0)))
        return grs

    def output_stage(ch, outs):
        cr = slice(ch * C, (ch + 1) * C)
        o = [jnp.concatenate([outs[b, g] for g in range(W // L)], axis=-1) for b in range(batch)]
        mean = [head_sums(t) * (1.0 / N) for t in o]
        yield
        d = [t - m for t, m in zip(o, mean)]
        var = [head_sums(t * t) * (1.0 / N) for t in d]
        bonus = [head_sums(prep[b, ch][0] * prep[b, ch][1] * rk_ref[...]) * prep[b, ch][2]
                 for b in range(batch)]
        yield
        rows_in = []
        for b in range(batch):
            t = d[b] * lax.rsqrt(var[b] + GN_EPS) * lnw_ref[...] + lnb_ref[...]
            y_rw = ((t + bonus[b]) * _silu(gate_ref[b, cr, :].astype(F32))).astype(BF16)
            rows_in.append(jnp.concatenate([y_rw, yda_ref[b, cr, :]], axis=-1))
        y = _dot(jnp.concatenate(rows_in, axis=0), wout_ref[...])
        yield
        y = y * lax.rsqrt(jnp.mean(y * y, axis=-1, keepdims=True) + RMS_EPS) * gpost_ref[...]
        for b in range(batch):
            o_ref[b, cr, :] = x_ref[b, cr, :] + mgate_ref[b] * y[b * C:(b + 1) * C]

    carried = {(b, g): state[b, g] for b in range(batch) for g in range(W // L)}
    pending = iter(())
    nxt = prepare(0)
    for ch in range(chunks):
        grs = nxt
        for gr in grs:
            m = _dot_nt(gr["lhs"], gr["rhs"])
            gr["a_b"] = jnp.where(tri, m[:, :L], 0.0)
            gr["a_k"] = jnp.where(tri, m[:, L:], 0.0).astype(BF16)
            gr["t"] = eye - jnp.where(levels[0], gr["a_b"][:C], 0.0)
        for gr in grs:
            gr["st"] = carried[gr["b"], gr["g"]]
            gr["x"] = _dot(jnp.concatenate([gr["a_k"], gr["lhs"]], axis=1),
                           jnp.concatenate([block_diag(gr["v"]), block_diag(gr["st"]).T], axis=0))
        if ch + 1 < chunks:
            nxt = prepare(ch + 1, anchor=grs[-1]["a_b"])
        for bit, keep in enumerate(level_keep[1:], start=1):
            size = 1 << bit
            packed = size >= 16
            odd = [slice(lo, lo + size) for lo in range(size, C, 2 * size)] if packed \
                else [slice(0, C)]
            for gr in grs:
                rows_t = jnp.concatenate([gr["t"][sl] for sl in odd], axis=0)
                gr["inner"] = _dot(rows_t.astype(BF16),
                                   block_diag(gr["a_b"][:C], keep)).astype(BF16)
            for gr in grs:
                t, upd = gr["t"], _dot(gr["inner"], block_diag(gr["t"]))
                pieces, pos = [], 0
                for n, sl in enumerate(odd):
                    pieces += [t[pos:sl.start], t[sl] - upd[n * size:n * size + sl.stop - sl.start]]
                    pos = sl.stop
                pieces.append(t[pos:C])
                gr["t"] = jnp.concatenate([p for p in pieces if p.shape[0]], axis=0)
            next(pending, None)
        for gr in grs:
            gr["uu"] = _dot(gr["t"].astype(BF16), block_diag(gr["x"][:C]))
        outs = {}
        for gr in grs:
            outs[gr["b"], gr["g"]] = gr["x"][C:] - _dot(gr["a_b"][C:].astype(BF16),
                                                       block_diag(gr["uu"]))
        for gr in grs:
            vu = jnp.concatenate([gr["v"], gr["uu"]], axis=0).astype(BF16)
            delta = jnp.where(same_head, _dot_tn(vu, gr["end"]), 0.0)
            carried[gr["b"], gr["g"]] = gr["st"] * gr["g_all"] + sum(
                delta[h * N:(h + 1) * N] for h in range(heads))
        for _ in pending:
            pass
        pending = output_stage(ch, outs)
    for (b, g), st in carried.items():
        state[b, g] = st
    for _ in pending:
        pass


def _rwkv_and_output(feat, gate, w0, w_w2, a0, w_a2, k_k, k_a, r_k, lnx_w, lnx_b,
                     y_da, x, w_out_bf16, g_post, mod_gate):
    batch, seq, _ = feat.shape
    rows = RW_CHUNK * RW_STEP_CHUNKS
    full = lambda shape: pl.BlockSpec(shape, lambda i: (0,) * len(shape))
    vec = lambda t: t.reshape(1, -1)
    return pl.pallas_call(
        _rwkv_kernel,
        grid=(seq // rows,),
        in_specs=[
            pl.BlockSpec((batch, rows, RW_SHIFTED), lambda i: (0, i, 0)),
            pl.BlockSpec((batch, rows, RW_WIDTH), lambda i: (0, i, 0)),
            full((1, RW_WIDTH)), full((RW_RANK, RW_WIDTH)),
            full((1, RW_WIDTH)), full((RW_RANK, RW_WIDTH)),
            full((1, RW_WIDTH)), full((1, RW_WIDTH)), full((1, RW_WIDTH)),
            full((1, RW_WIDTH)), full((1, RW_WIDTH)),
            pl.BlockSpec((batch, rows, DA_WIDTH), lambda i: (0, i, 0)),
            pl.BlockSpec((batch, rows, D_MODEL), lambda i: (0, i, 0)),
            pl.BlockSpec((D_MODEL, D_MODEL), lambda i: (0, 0), pipeline_mode=pl.Buffered(1)),
            full((1, D_MODEL)), full((batch, 1, D_MODEL)),
        ],
        out_specs=pl.BlockSpec((batch, rows, D_MODEL), lambda i: (0, i, 0)),
        out_shape=jax.ShapeDtypeStruct(x.shape, x.dtype),
        scratch_shapes=[
            pltpu.VMEM((batch, RW_WIDTH // RW_GROUP, RW_HEAD, RW_GROUP), F32),
        ],
        compiler_params=pltpu.CompilerParams(
            dimension_semantics=("arbitrary",),
            vmem_limit_bytes=VMEM_LIMIT),
        name="rwkv7_chunked",
    )(feat, gate, vec(w0), w_w2.astype(BF16), vec(a0), w_a2.astype(BF16),
      vec(k_k), vec(k_a), vec(r_k), vec(lnx_w), vec(lnx_b),
      y_da, x, w_out_bf16, g_post, mod_gate)


def _attn_kernel(lq1_ref, lk1_ref, lq2_ref, lk2_ref, sw_ref, qt_ref, k_ref, vt_ref,
                 g_ref, o_ref, s_buf, p_buf, acc_buf, *, lambda_init):
    T, HALF = ATT_TILE, ATT_TILE // 2
    tiles = k_ref.shape[1] // T

    comp = lax.broadcasted_iota(jnp.int32, (DA_VDIM, T), 0) < DA_QKDIM
    key = lax.broadcasted_iota(jnp.int32, (HALF, HALF), 0)
    query = lax.broadcasted_iota(jnp.int32, (HALF, HALF), 1)
    visible = (query // CHUNK) >= (key // CHUNK)
    lam = (jnp.exp(jnp.sum(lq1_ref[...] * lk1_ref[...], axis=-1, keepdims=True))
           - jnp.exp(jnp.sum(lq2_ref[...] * lk2_ref[...], axis=-1, keepdims=True))
           + lambda_init)

    for step in range(ATT_STEPS):
        order = [i for k in range(step, tiles // 2, ATT_STEPS) for i in (tiles - 1 - k, k)]
        pl.when(pl.program_id(2) == step)(functools.partial(
            _attn_tasks, [(i, j) for i in order for j in range(i + 1)],
            comp, visible, lam, sw_ref, qt_ref, k_ref, vt_ref, g_ref, o_ref,
            s_buf, p_buf, acc_buf, lambda_init))


def _attn_tasks(tasks, comp, visible, lam, sw_ref, qt_ref, k_ref, vt_ref, g_ref, o_ref,
                s_buf, p_buf, acc_buf, lambda_init):
    T, HALF = ATT_TILE, ATT_TILE // 2
    qt_parts = {}

    def qt_c(i):
        if i not in qt_parts:
            qt = qt_ref[0, 0, i]
            zero = jnp.zeros_like(qt)
            qt_parts[i] = (jnp.where(comp, qt, zero), jnp.where(comp, zero, qt))
        return qt_parts[i]

    def colmax(s):
        return jnp.max(s, axis=0, keepdims=True)

    def scores(n):
        i, j = tasks[n]
        kb = k_ref[0, j * T:(j + 1) * T, :]
        blk_max = []
        for c in range(2):
            q = qt_c(i)[c]
            if j < i:
                s = _dot(kb, q)
                s_buf[n % 2, c] = s
                blk_max.append(colmax(s))
            else:
                s_r = _dot(kb, q[:, HALF:])
                s_b = jnp.where(visible, s_r[HALF:], -jnp.inf)
                s_l = jnp.where(visible, _dot(kb[:HALF], q[:, :HALF]), -jnp.inf)
                s_buf[n % 2, c, :HALF, HALF:] = s_r[:HALF]
                s_buf[n % 2, c, HALF:, HALF:] = s_b
                s_buf[n % 2, c, :HALF, :HALF] = s_l
                blk_max.append(jnp.concatenate(
                    [colmax(s_l), jnp.maximum(colmax(s_r[:HALF]), colmax(s_b))], axis=1))
        return blk_max

    def softmax(n, blk_maxes, maxes):
        i, j = tasks[n]
        new, alphas = [], []
        for c in range(2):
            if j == 0:
                m_new, alpha = blk_maxes[c], None
            else:
                m_new = jnp.maximum(maxes[c], blk_maxes[c])
                alpha = jnp.exp2(maxes[c] - m_new)
            prob = lambda s, m: jnp.exp2((s - m).astype(BF16))
            if j < i:
                p_buf[n % 2, c] = prob(s_buf[n % 2, c], m_new)
            else:
                left, right = m_new[:, :HALF], m_new[:, HALF:]
                p_buf[n % 2, c, :HALF, :HALF] = prob(s_buf[n % 2, c, :HALF, :HALF], left)
                p_buf[n % 2, c, :HALF, HALF:] = prob(s_buf[n % 2, c, :HALF, HALF:], right)
                p_buf[n % 2, c, HALF:, HALF:] = prob(s_buf[n % 2, c, HALF:, HALF:], right)
            new.append(m_new)
            alphas.append(alpha)
        return new, alphas

    ones_rows = (lax.broadcasted_iota(jnp.int32, (ONES_ROWS, T), 0) == 0).astype(BF16)

    def values(n, alphas):
        i, j = tasks[n]
        vtb = jnp.concatenate([vt_ref[0, 0, j], ones_rows], axis=0)
        for c in range(2):
            if j < i:
                pv = _dot(vtb, p_buf[n % 2, c])
            else:
                pv = jnp.concatenate(
                    [_dot(vtb[:, :HALF], p_buf[n % 2, c, :HALF, :HALF]),
                     _dot(vtb, p_buf[n % 2, c, :, HALF:])], axis=1)
            acc_buf[i % 2, c] = pv if j == 0 else alphas[c] * acc_buf[i % 2, c] + pv

    def finish(i):
        num0, num1 = acc_buf[i % 2, 0, :DA_VDIM], acc_buf[i % 2, 1, :DA_VDIM]
        l0 = acc_buf[i % 2, 0, DA_VDIM:DA_VDIM + 1]
        l1 = acc_buf[i % 2, 1, DA_VDIM:DA_VDIM + 1]
        o = (num0 * (1.0 / l0) - num1 * (lam / l1)).T
        o = o * lax.rsqrt(jnp.mean(o * o, axis=-1, keepdims=True) + SUBLN_EPS)
        o = o * sw_ref[...] * (1.0 - lambda_init)
        rows = slice(i * T, (i + 1) * T)
        o_ref[0, rows, :] = (o * _silu(g_ref[0, rows, :].astype(F32))).astype(o_ref.dtype)

    blk_max = scores(0)
    maxes, pending = None, None
    for n, (i, j) in enumerate(tasks):
        next_max = scores(n + 1) if n + 1 < len(tasks) else None
        if pending is not None:
            values(pending[0], pending[1])
            if pending[2]:
                finish(tasks[pending[0]][0])
        maxes, alphas = softmax(n, blk_max, maxes)
        pending = (n, alphas, j == i)
        blk_max = next_max
    values(pending[0], pending[1])
    finish(tasks[-1][0])


def _diff_attention(qt, k, vt, gate, lam_q1, lam_k1, lam_q2, lam_k2, subln_w, lambda_init):
    batch, seq, _ = k.shape
    T = ATT_TILE
    small = lambda n: pl.BlockSpec((1, n), lambda b, h, s: (0, 0))
    rows = pl.BlockSpec((1, seq, DA_VDIM), lambda b, h, s: (b, 0, h))
    cols = pl.BlockSpec((1, 1, seq // T, DA_VDIM, T), lambda b, h, s: (b, h, 0, 0, 0))
    return pl.pallas_call(
        functools.partial(_attn_kernel, lambda_init=lambda_init),
        grid=(batch, DA_HEADS, ATT_STEPS),
        in_specs=[small(DA_QKDIM)] * 4 + [small(DA_VDIM), cols, rows, cols, rows],
        out_specs=rows,
        out_shape=jax.ShapeDtypeStruct((batch, seq, DA_WIDTH), BF16),
        scratch_shapes=[pltpu.VMEM((2, 2, T, T), F32),
                        pltpu.VMEM((2, 2, T, T), BF16),
                        pltpu.VMEM((2, 2, DA_VDIM + ONES_ROWS, T), F32)],
        compiler_params=pltpu.CompilerParams(
            dimension_semantics=("parallel", "parallel", "arbitrary"),
            vmem_limit_bytes=VMEM_LIMIT),
        name="diff_attn",
    )(*(t.reshape(1, -1) for t in (lam_q1, lam_k1, lam_q2, lam_k2, subln_w)),
      qt, k, vt, gate)


def kernel(x, c, w_ada, b_ada, g_pre, g_post, w_in, w_out, rw_mu, rw_w0, rw_w_w2,
           rw_a0, rw_w_a2, rw_k_k, rw_k_a, rw_r_k, rw_lnx_w, rw_lnx_b,
           da_lam_q1, da_lam_k1, da_lam_q2, da_lam_k2, da_subln_w):
    batch = x.shape[0]
    depth = w_in.shape[0]
    for l in range(depth):
        lambda_init = 0.8 - 0.6 * math.exp(-0.3 * l)
        mod = _modulation(c, w_ada[l], b_ada[l]).reshape(batch, 3, 1, D_MODEL)
        shift, scale, gate = mod[:, 0], mod[:, 1], mod[:, 2]
        feat, rw_gate, dk, dg, dqt, dvt = _in_projection(
            x, shift, scale, g_pre[l].reshape(1, D_MODEL), rw_mu[l].reshape(1, RW_SHIFTED),
            w_in[l].astype(BF16))
        y_da = _diff_attention(dqt, dk, dvt, dg, da_lam_q1[l], da_lam_k1[l],
                               da_lam_q2[l], da_lam_k2[l], da_subln_w[l], lambda_init)
        x = _rwkv_and_output(feat, rw_gate, rw_w0[l], rw_w_w2[l], rw_a0[l], rw_w_a2[l],
                             rw_k_k[l], rw_k_a[l], rw_r_k[l], rw_lnx_w[l], rw_lnx_b[l],
                             y_da, x, w_out[l].astype(BF16), g_post[l].reshape(1, D_MODEL), gate)
    return x
```

```python
import functools
import math

import jax
import jax.numpy as jnp
from jax import lax
from jax.experimental import pallas as pl
from jax.experimental.pallas import tpu as pltpu

D_MODEL = 1024
CHUNK = 64
RW_WIDTH = 512
RW_HEAD = 64
RW_RANK = 64
RW_SHIFTED = 3 * RW_WIDTH + 2 * RW_RANK
RW_COLS = RW_SHIFTED + RW_WIDTH
DA_WIDTH = 512
DA_HEADS = 4
DA_VDIM = DA_WIDTH // DA_HEADS
DA_QKDIM = DA_VDIM // 2
D_IN = RW_COLS + 4 * DA_WIDTH
RMS_EPS = 1e-6
GN_EPS = 64e-5
SUBLN_EPS = 1e-5

IN_ROWS = 512
IN_SUBTILES = 2
RW_CHUNK = 64
RW_STEP_CHUNKS = 2
RW_GROUP = 2 * RW_HEAD
ATT_TILE = 512
ONES_ROWS = 16
ATT_STEPS = 2
VMEM_LIMIT = 48 * 1024 * 1024
SUBLANES = 8
MOD_PARTS = 3

F32 = jnp.float32
BF16 = jnp.bfloat16
HIGHEST = lax.Precision.HIGHEST


def _silu(t):
    return t * jax.nn.sigmoid(t)


def _dot(a, b, precision=None):
    return jnp.dot(a, b, preferred_element_type=F32, precision=precision)


def _dot_nt(a, b):
    return lax.dot_general(a, b, (((1,), (1,)), ((), ())), preferred_element_type=F32)


def _dot_tn(a, b):
    return lax.dot_general(a, b, (((0,), (0,)), ((), ())), preferred_element_type=F32)


def _split_dot(x, w_exact, terms, *, x_is_lhs):
    acc = None
    for _ in range(terms):
        piece = x.astype(BF16)
        part = _dot(piece, w_exact) if x_is_lhs else _dot(w_exact, piece)
        acc = part if acc is None else acc + part
        x = x - piece.astype(F32)
    return acc


def _mod_kernel(c_ref, w_ref, b_ref, o_ref):
    o_ref[...] = _dot(_silu(c_ref[...]), w_ref[...], HIGHEST) + b_ref[...]


def _modulation(c, w_ada, b_ada):
    batch = c.shape[0]
    return pl.pallas_call(
        _mod_kernel,
        grid=(MOD_PARTS,),
        in_specs=[
            pl.BlockSpec((batch, D_MODEL), lambda j: (0, 0)),
            pl.BlockSpec((D_MODEL, D_MODEL), lambda j: (0, j)),
            pl.BlockSpec((1, D_MODEL), lambda j: (0, j)),
        ],
        out_specs=pl.BlockSpec((batch, D_MODEL), lambda j: (0, j)),
        out_shape=jax.ShapeDtypeStruct((batch, MOD_PARTS * D_MODEL), F32),
        compiler_params=pltpu.CompilerParams(vmem_limit_bytes=VMEM_LIMIT),
        name="adaln_mod",
    )(c, w_ada, b_ada.reshape(1, MOD_PARTS * D_MODEL))


_Q_COLS = (RW_COLS, RW_COLS + DA_WIDTH)
_V_COLS = (RW_COLS + 2 * DA_WIDTH, RW_COLS + 3 * DA_WIDTH)
_ROW_SEGMENTS = (
    (0, RW_SHIFTED),
    (RW_SHIFTED, RW_COLS),
    (RW_COLS + DA_WIDTH, RW_COLS + 2 * DA_WIDTH),
    (RW_COLS + 3 * DA_WIDTH, D_IN),
)
_ROW_WIDTHS = tuple(hi - lo for lo, hi in _ROW_SEGMENTS)
_ROW_DTYPES = (F32, BF16, BF16, BF16)
Q_SCALE = math.log2(math.e) / math.sqrt(DA_QKDIM)


def _in_proj_kernel(x_ref, shift_ref, scale_ref, g_ref, mu_ref, w_ref, *rest):
    out_refs, wt_ref, ubuf = rest[:-2], rest[-2], rest[-1]
    tm = x_ref.shape[1]

    @pl.when((pl.program_id(0) == 0) & (pl.program_id(1) == 0))
    def _():
        wt_ref[0:DA_WIDTH, :] = w_ref[:, _Q_COLS[0]:_Q_COLS[1]].T
        wt_ref[DA_WIDTH:, :] = w_ref[:, _V_COLS[0]:_V_COLS[1]].T

    @pl.when(pl.program_id(1) == 0)
    def _():
        ubuf[0:SUBLANES, :] = jnp.zeros((SUBLANES, RW_SHIFTED), F32)

    qt_ref, vt_ref = out_refs[len(_ROW_SEGMENTS):]
    sub = tm // IN_SUBTILES
    for rows in (slice(t * sub, (t + 1) * sub) for t in range(IN_SUBTILES)):
        x = x_ref[0, rows, :]
        y = x * lax.rsqrt(jnp.mean(x * x, axis=-1, keepdims=True) + RMS_EPS) * g_ref[...]
        h = (y * (1.0 + scale_ref[0]) + shift_ref[0]).astype(BF16)
        for (lo, hi), o_ref in zip(_ROW_SEGMENTS, out_refs):
            u = _dot(h, w_ref[:, lo:hi])
            if lo == 0:
                ubuf[SUBLANES + rows.start:SUBLANES + rows.stop, :] = u
                prev = ubuf[SUBLANES - 1 + rows.start:SUBLANES - 1 + rows.stop, :]
                u = u + (prev - u) * mu_ref[...]
            o_ref[0, rows, :] = u.astype(o_ref.dtype)
        qt = _dot_nt(wt_ref[0:DA_WIDTH, :], h) * Q_SCALE
        vt = _dot_nt(wt_ref[DA_WIDTH:, :], h)
        qt_ref[0, :, 0, :, rows] = qt.astype(BF16).reshape(DA_HEADS, DA_VDIM, sub)
        vt_ref[0, :, 0, :, rows] = vt.astype(BF16).reshape(DA_HEADS, DA_VDIM, sub)
    ubuf[0:SUBLANES, :] = ubuf[tm:tm + SUBLANES, :]


def _in_projection(x, shift, scale, g_pre, mu, w_in_bf16):
    batch, seq, _ = x.shape
    tm = IN_ROWS
    row_spec = lambda width: pl.BlockSpec((1, tm, width), lambda b, i: (b, i, 0))
    vec_spec = pl.BlockSpec((1, 1, D_MODEL), lambda b, i: (b, 0, 0))
    per_tile = ATT_TILE // tm
    t_spec = pl.BlockSpec((1, DA_HEADS, 1, DA_VDIM, tm),
                          lambda b, i: (b, 0, i // per_tile, 0, i % per_tile))
    t_shape = jax.ShapeDtypeStruct((batch, DA_HEADS, seq // ATT_TILE, DA_VDIM, ATT_TILE), BF16)
    return pl.pallas_call(
        _in_proj_kernel,
        grid=(batch, seq // tm),
        in_specs=[
            row_spec(D_MODEL), vec_spec, vec_spec,
            pl.BlockSpec((1, D_MODEL), lambda b, i: (0, 0)),
            pl.BlockSpec((1, RW_SHIFTED), lambda b, i: (0, 0)),
            pl.BlockSpec((D_MODEL, D_IN), lambda b, i: (0, 0), pipeline_mode=pl.Buffered(1)),
        ],
        out_specs=[row_spec(w) for w in _ROW_WIDTHS] + [t_spec, t_spec],
        out_shape=[jax.ShapeDtypeStruct((batch, seq, w), dt)
                   for w, dt in zip(_ROW_WIDTHS, _ROW_DTYPES)] + [t_shape, t_shape],
        scratch_shapes=[pltpu.VMEM((2 * DA_WIDTH, D_MODEL), BF16),
                        pltpu.VMEM((tm + SUBLANES, RW_SHIFTED), F32)],
        compiler_params=pltpu.CompilerParams(
            dimension_semantics=("arbitrary", "arbitrary"),
            vmem_limit_bytes=VMEM_LIMIT),
        name="in_proj",
    )(x, shift, scale, g_pre, mu, w_in_bf16)


def _rwkv_kernel(feat_ref, gate_ref, w0_ref, ww2_ref, a0_ref, wa2_ref,
                 kk_ref, ka_ref, rk_ref, lnw_ref, lnb_ref,
                 yda_ref, x_ref, wout_ref, gpost_ref, mgate_ref, o_ref, state):
    C, W, N, L = RW_CHUNK, RW_WIDTH, RW_HEAD, RW_GROUP
    heads = L // N
    batch, rows = feat_ref.shape[0], feat_ref.shape[1]
    chunks = rows // C
    ci = pl.program_id(0)

    @pl.when(ci == 0)
    def _():
        state[...] = jnp.zeros(state.shape, F32)

    iota = lambda shape, axis: lax.broadcasted_iota(jnp.int32, shape, axis)
    row2, col2 = iota((2 * C, L), 0), iota((2 * C, L), 1) & (N - 1)
    tri = ((row2 & (C - 1)) > col2) | ((row2 >= C) & ((row2 & (C - 1)) == col2))
    r1, c1 = iota((C, L), 0), iota((C, L), 1) & (N - 1)
    eye = (r1 == c1).astype(F32)
    levels = []
    for bit in range(C.bit_length() - 1):
        levels.append(((r1 >> (bit + 1)) == (c1 >> (bit + 1)))
                      & (((r1 >> bit) & 1) == 1) & (((c1 >> bit) & 1) == 0))
    head_bits = N.bit_length() - 1
    same_head = (iota((L, L), 0) >> head_bits) == (iota((L, L), 1) >> head_bits)
    seg = same_head.astype(BF16)
    cum = (iota((C, C), 0) >= iota((C, C), 1)).astype(BF16)

    def block_diag(t, keep=same_head):
        return jnp.where(keep, jnp.concatenate([t] * heads, axis=0), 0.0).astype(BF16)

    level_keep = [same_head & jnp.concatenate([level] * heads, axis=0) for level in levels]

    def head_sums(t):
        return jnp.concatenate(
            [_dot(t[:, g * L:(g + 1) * L].astype(BF16), seg) for g in range(W // L)], axis=-1)

    roots = []
    for b in range(batch):
        feat = feat_ref[b]
        r, k, v = feat[:, 0:W], feat[:, W:2 * W], feat[:, 2 * W:3 * W]
        w_lo = feat[:, 3 * W:3 * W + RW_RANK]
        a_lo = feat[:, 3 * W + RW_RANK:]
        z = w0_ref[...] + _dot(jnp.tanh(w_lo).astype(BF16), ww2_ref[...])
        logw = -math.exp(-0.5) * jax.nn.sigmoid(z)
        a_pre = a0_ref[...] + _dot(a_lo.astype(BF16), wa2_ref[...])
        kk_raw = k * kk_ref[...]
        norm2 = head_sums(kk_raw * kk_raw)
        cs = [_split_dot(logw[ch * C:(ch + 1) * C], cum, 2, x_is_lhs=False)
              for ch in range(chunks)]
        roots.append((r, k, v, logw, a_pre, kk_raw, norm2, cs))

    prep = {}

    def prepare(ch, anchor=None):
        cr = slice(ch * C, (ch + 1) * C)
        zero = 0.0
        if anchor is not None:
            bits = pltpu.bitcast(anchor[0:1, :], jnp.uint32)
            half_word = jnp.uint32(16)
            bits = lax.shift_right_logical(lax.shift_right_logical(bits, half_word), half_word)
            zero = jnp.concatenate([pltpu.bitcast(bits, F32)] * (W // L), axis=1)
        grs = []
        for b in range(batch):
            r, k, v, logw, a_pre, kk_raw, norm2, cs_all = roots[b]
            r, k, v, logw, cs = r[cr], k[cr], v[cr], logw[cr], cs_all[ch] + zero
            a = jax.nn.sigmoid(a_pre[cr] + zero)
            kk = kk_raw[cr] * lax.rsqrt(jnp.maximum(norm2[cr], 1e-24))
            bb = a * kk
            k2 = k * (1.0 + (a - 1.0) * ka_ref[...])
            cs_last = cs[C - 1:C, :]
            g_all = jnp.exp(cs_last)
            g_inv = jnp.exp(-cs)
            g_end = jnp.exp(cs_last - cs)
            kk_t = (kk * jnp.exp(cs - logw)).astype(BF16)
            r_t = (r * jnp.exp(cs)).astype(BF16)
            k_inv, b_inv = k2 * g_inv, bb * g_inv
            k_end, nb_end = (k2 * g_end).astype(BF16), (-(bb * g_end)).astype(BF16)
            prep[b, ch] = (r, k2, v)
            for g in range(W // L):
                gl = slice(g * L, (g + 1) * L)
                grs.append(dict(
                    b=b, g=g, g_all=g_all[:, gl], v=v[:, gl],
                    lhs=jnp.concatenate([kk_t[:, gl], r_t[:, gl]], axis=0),
                    rhs=jnp.concatenate([block_diag(b_inv[:, gl]), block_diag(k_inv[:, gl])],
                                        axis=0),
                    end=jnp.concatenate([k_end[:, gl], nb_end[:, gl]], axis=0)))
        return grs

    def output_stage(ch, outs):
        cr = slice(ch * C, (ch + 1) * C)
        o = [jnp.concatenate([outs[b, g] for g in range(W // L)], axis=-1) for b in range(batch)]
        mean = [head_sums(t) * (1.0 / N) for t in o]
        yield
        d = [t - m for t, m in zip(o, mean)]
        var = [head_sums(t * t) * (1.0 / N) for t in d]
        bonus = [head_sums(prep[b, ch][0] * prep[b, ch][1] * rk_ref[...]) * prep[b, ch][2]
                 for b in range(batch)]
        yield
        rows_in = []
        for b in range(batch):
            t = d[b] * lax.rsqrt(var[b] + GN_EPS) * lnw_ref[...] + lnb_ref[...]
            y_rw = ((t + bonus[b]) * _silu(gate_ref[b, cr, :].astype(F32))).astype(BF16)
            rows_in.append(jnp.concatenate([y_rw, yda_ref[b, cr, :]], axis=-1))
        y = _dot(jnp.concatenate(rows_in, axis=0), wout_ref[...])
        yield
        y = y * lax.rsqrt(jnp.mean(y * y, axis=-1, keepdims=True) + RMS_EPS) * gpost_ref[...]
        for b in range(batch):
            o_ref[b, cr, :] = x_ref[b, cr, :] + mgate_ref[b] * y[b * C:(b + 1) * C]

    carried = {(b, g): state[b, g] for b in range(batch) for g in range(W // L)}
    pending = iter(())
    nxt = prepare(0)
    for ch in range(chunks):
        grs = nxt
        for gr in grs:
            m = _dot_nt(gr["lhs"], gr["rhs"])
            gr["a_b"] = jnp.where(tri, m[:, :L], 0.0)
            gr["a_k"] = jnp.where(tri, m[:, L:], 0.0).astype(BF16)
            gr["t"] = eye - jnp.where(levels[0], gr["a_b"][:C], 0.0)
        for gr in grs:
            gr["st"] = carried[gr["b"], gr["g"]]
            gr["x"] = _dot(jnp.concatenate([gr["a_k"], gr["lhs"]], axis=1),
                           jnp.concatenate([block_diag(gr["v"]), block_diag(gr["st"]).T], axis=0))
        if ch + 1 < chunks:
            nxt = prepare(ch + 1, anchor=grs[-1]["a_b"])
        for keep in level_keep[1:]:
            for gr in grs:
                gr["inner"] = _dot(gr["t"].astype(BF16),
                                   block_diag(gr["a_b"][:C], keep)).astype(BF16)
            for gr in grs:
                gr["t"] = gr["t"] - _dot(gr["inner"], block_diag(gr["t"]))
            next(pending, None)
        for gr in grs:
            gr["uu"] = _dot(gr["t"].astype(BF16), block_diag(gr["x"][:C]))
        outs = {}
        for gr in grs:
            outs[gr["b"], gr["g"]] = gr["x"][C:] - _dot(gr["a_b"][C:].astype(BF16),
                                                       block_diag(gr["uu"]))
        for gr in grs:
            vu = jnp.concatenate([gr["v"], gr["uu"]], axis=0).astype(BF16)
            delta = jnp.where(same_head, _dot_tn(vu, gr["end"]), 0.0)
            carried[gr["b"], gr["g"]] = gr["st"] * gr["g_all"] + sum(
                delta[h * N:(h + 1) * N] for h in range(heads))
        for _ in pending:
            pass
        pending = output_stage(ch, outs)
    for (b, g), st in carried.items():
        state[b, g] = st
    for _ in pending:
        pass


def _rwkv_and_output(feat, gate, w0, w_w2, a0, w_a2, k_k, k_a, r_k, lnx_w, lnx_b,
                     y_da, x, w_out_bf16, g_post, mod_gate):
    batch, seq, _ = feat.shape
    rows = RW_CHUNK * RW_STEP_CHUNKS
    full = lambda shape: pl.BlockSpec(shape, lambda i: (0,) * len(shape))
    vec = lambda t: t.reshape(1, -1)
    return pl.pallas_call(
        _rwkv_kernel,
        grid=(seq // rows,),
        in_specs=[
            pl.BlockSpec((batch, rows, RW_SHIFTED), lambda i: (0, i, 0)),
            pl.BlockSpec((batch, rows, RW_WIDTH), lambda i: (0, i, 0)),
            full((1, RW_WIDTH)), full((RW_RANK, RW_WIDTH)),
            full((1, RW_WIDTH)), full((RW_RANK, RW_WIDTH)),
            full((1, RW_WIDTH)), full((1, RW_WIDTH)), full((1, RW_WIDTH)),
            full((1, RW_WIDTH)), full((1, RW_WIDTH)),
            pl.BlockSpec((batch, rows, DA_WIDTH), lambda i: (0, i, 0)),
            pl.BlockSpec((batch, rows, D_MODEL), lambda i: (0, i, 0)),
            pl.BlockSpec((D_MODEL, D_MODEL), lambda i: (0, 0), pipeline_mode=pl.Buffered(1)),
            full((1, D_MODEL)), full((batch, 1, D_MODEL)),
        ],
        out_specs=pl.BlockSpec((batch, rows, D_MODEL), lambda i: (0, i, 0)),
        out_shape=jax.ShapeDtypeStruct(x.shape, x.dtype),
        scratch_shapes=[
            pltpu.VMEM((batch, RW_WIDTH // RW_GROUP, RW_HEAD, RW_GROUP), F32),
        ],
        compiler_params=pltpu.CompilerParams(
            dimension_semantics=("arbitrary",),
            vmem_limit_bytes=VMEM_LIMIT),
        name="rwkv7_chunked",
    )(feat, gate, vec(w0), w_w2.astype(BF16), vec(a0), w_a2.astype(BF16),
      vec(k_k), vec(k_a), vec(r_k), vec(lnx_w), vec(lnx_b),
      y_da, x, w_out_bf16, g_post, mod_gate)


def _attn_kernel(lq1_ref, lk1_ref, lq2_ref, lk2_ref, sw_ref, qt_ref, k_ref, vt_ref,
                 g_ref, o_ref, s_buf, p_buf, acc_buf, *, lambda_init):
    T, HALF = ATT_TILE, ATT_TILE // 2
    tiles = k_ref.shape[1] // T

    comp = lax.broadcasted_iota(jnp.int32, (DA_VDIM, T), 0) < DA_QKDIM
    key = lax.broadcasted_iota(jnp.int32, (HALF, HALF), 0)
    query = lax.broadcasted_iota(jnp.int32, (HALF, HALF), 1)
    visible = (query // CHUNK) >= (key // CHUNK)
    lam = (jnp.exp(jnp.sum(lq1_ref[...] * lk1_ref[...], axis=-1, keepdims=True))
           - jnp.exp(jnp.sum(lq2_ref[...] * lk2_ref[...], axis=-1, keepdims=True))
           + lambda_init)

    for step in range(ATT_STEPS):
        order = [i for k in range(step, tiles // 2, ATT_STEPS) for i in (tiles - 1 - k, k)]
        pl.when(pl.program_id(2) == step)(functools.partial(
            _attn_tasks, [(i, j) for i in order for j in range(i + 1)],
            comp, visible, lam, sw_ref, qt_ref, k_ref, vt_ref, g_ref, o_ref,
            s_buf, p_buf, acc_buf, lambda_init))


def _attn_tasks(tasks, comp, visible, lam, sw_ref, qt_ref, k_ref, vt_ref, g_ref, o_ref,
                s_buf, p_buf, acc_buf, lambda_init):
    T, HALF = ATT_TILE, ATT_TILE // 2
    qt_parts = {}

    def qt_c(i):
        if i not in qt_parts:
            qt = qt_ref[0, 0, i]
            zero = jnp.zeros_like(qt)
            qt_parts[i] = (jnp.where(comp, qt, zero), jnp.where(comp, zero, qt))
        return qt_parts[i]

    def colmax(s):
        return jnp.max(s, axis=0, keepdims=True)

    def scores(n):
        i, j = tasks[n]
        kb = k_ref[0, j * T:(j + 1) * T, :]
        blk_max = []
        for c in range(2):
            q = qt_c(i)[c]
            if j < i:
                s = _dot(kb, q)
                s_buf[n % 2, c] = s
                blk_max.append(colmax(s))
            else:
                s_r = _dot(kb, q[:, HALF:])
                s_b = jnp.where(visible, s_r[HALF:], -jnp.inf)
                s_l = jnp.where(visible, _dot(kb[:HALF], q[:, :HALF]), -jnp.inf)
                s_buf[n % 2, c, :HALF, HALF:] = s_r[:HALF]
                s_buf[n % 2, c, HALF:, HALF:] = s_b
                s_buf[n % 2, c, :HALF, :HALF] = s_l
                blk_max.append(jnp.concatenate(
                    [colmax(s_l), jnp.maximum(colmax(s_r[:HALF]), colmax(s_b))], axis=1))
        return blk_max

    def softmax(n, blk_maxes, maxes):
        i, j = tasks[n]
        new, alphas = [], []
        for c in range(2):
            if j == 0:
                m_new, alpha = blk_maxes[c], None
            else:
                m_new = jnp.maximum(maxes[c], blk_maxes[c])
                alpha = jnp.exp2(maxes[c] - m_new)
            prob = lambda s, m: jnp.exp2((s - m).astype(BF16))
            if j < i:
                p_buf[n % 2, c] = prob(s_buf[n % 2, c], m_new)
            else:
                left, right = m_new[:, :HALF], m_new[:, HALF:]
                p_buf[n % 2, c, :HALF, :HALF] = prob(s_buf[n % 2, c, :HALF, :HALF], left)
                p_buf[n % 2, c, :HALF, HALF:] = prob(s_buf[n % 2, c, :HALF, HALF:], right)
                p_buf[n % 2, c, HALF:, HALF:] = prob(s_buf[n % 2, c, HALF:, HALF:], right)
            new.append(m_new)
            alphas.append(alpha)
        return new, alphas

    ones_rows = (lax.broadcasted_iota(jnp.int32, (ONES_ROWS, T), 0) == 0).astype(BF16)

    def values(n, alphas):
        i, j = tasks[n]
        vtb = jnp.concatenate([vt_ref[0, 0, j], ones_rows], axis=0)
        for c in range(2):
            if j < i:
                pv = _dot(vtb, p_buf[n % 2, c])
            else:
                pv = jnp.concatenate(
                    [_dot(vtb[:, :HALF], p_buf[n % 2, c, :HALF, :HALF]),
                     _dot(vtb, p_buf[n % 2, c, :, HALF:])], axis=1)
            acc_buf[i % 2, c] = pv if j == 0 else alphas[c] * acc_buf[i % 2, c] + pv

    def finish(i):
        num0, num1 = acc_buf[i % 2, 0, :DA_VDIM], acc_buf[i % 2, 1, :DA_VDIM]
        l0 = acc_buf[i % 2, 0, DA_VDIM:DA_VDIM + 1]
        l1 = acc_buf[i % 2, 1, DA_VDIM:DA_VDIM + 1]
        o = (num0 * (1.0 / l0) - num1 * (lam / l1)).T
        o = o * lax.rsqrt(jnp.mean(o * o, axis=-1, keepdims=True) + SUBLN_EPS)
        o = o * sw_ref[...] * (1.0 - lambda_init)
        rows = slice(i * T, (i + 1) * T)
        o_ref[0, rows, :] = (o * _silu(g_ref[0, rows, :].astype(F32))).astype(o_ref.dtype)

    blk_max = scores(0)
    maxes, pending = None, None
    for n, (i, j) in enumerate(tasks):
        next_max = scores(n + 1) if n + 1 < len(tasks) else None
        if pending is not None:
            values(pending[0], pending[1])
            if pending[2]:
                finish(tasks[pending[0]][0])
        maxes, alphas = softmax(n, blk_max, maxes)
        pending = (n, alphas, j == i)
        blk_max = next_max
    values(pending[0], pending[1])
    finish(tasks[-1][0])


def _diff_attention(qt, k, vt, gate, lam_q1, lam_k1, lam_q2, lam_k2, subln_w, lambda_init):
    batch, seq, _ = k.shape
    T = ATT_TILE
    small = lambda n: pl.BlockSpec((1, n), lambda b, h, s: (0, 0))
    rows = pl.BlockSpec((1, seq, DA_VDIM), lambda b, h, s: (b, 0, h))
    cols = pl.BlockSpec((1, 1, seq // T, DA_VDIM, T), lambda b, h, s: (b, h, 0, 0, 0))
    return pl.pallas_call(
        functools.partial(_attn_kernel, lambda_init=lambda_init),
        grid=(batch, DA_HEADS, ATT_STEPS),
        in_specs=[small(DA_QKDIM)] * 4 + [small(DA_VDIM), cols, rows, cols, rows],
        out_specs=rows,
        out_shape=jax.ShapeDtypeStruct((batch, seq, DA_WIDTH), BF16),
        scratch_shapes=[pltpu.VMEM((2, 2, T, T), F32),
                        pltpu.VMEM((2, 2, T, T), BF16),
                        pltpu.VMEM((2, 2, DA_VDIM + ONES_ROWS, T), F32)],
        compiler_params=pltpu.CompilerParams(
            dimension_semantics=("parallel", "parallel", "arbitrary"),
            vmem_limit_bytes=VMEM_LIMIT),
        name="diff_attn",
    )(*(t.reshape(1, -1) for t in (lam_q1, lam_k1, lam_q2, lam_k2, subln_w)),
      qt, k, vt, gate)


def kernel(x, c, w_ada, b_ada, g_pre, g_post, w_in, w_out, rw_mu, rw_w0, rw_w_w2,
           rw_a0, rw_w_a2, rw_k_k, rw_k_a, rw_r_k, rw_lnx_w, rw_lnx_b,
           da_lam_q1, da_lam_k1, da_lam_q2, da_lam_k2, da_subln_w):
    batch = x.shape[0]
    depth = w_in.shape[0]
    for l in range(depth):
        lambda_init = 0.8 - 0.6 * math.exp(-0.3 * l)
        mod = _modulation(c, w_ada[l], b_ada[l]).reshape(batch, MOD_PARTS, 1, D_MODEL)
        shift, scale, gate = mod[:, 0], mod[:, 1], mod[:, 2]
        feat, rw_gate, dk, dg, dqt, dvt = _in_projection(
            x, shift, scale, g_pre[l].reshape(1, D_MODEL), rw_mu[l].reshape(1, RW_SHIFTED),
            w_in[l].astype(BF16))
        y_da = _diff_attention(dqt, dk, dvt, dg, da_lam_q1[l], da_lam_k1[l],
                               da_lam_q2[l], da_lam_k2[l], da_subln_w[l], lambda_init)
        x = _rwkv_and_output(feat, rw_gate, rw_w0[l], rw_w_w2[l], rw_a0[l], rw_w_a2[l],
                             rw_k_k[l], rw_k_a[l], rw_r_k[l], rw_lnx_w[l], rw_lnx_b[l],
                             y_da, x, w_out[l].astype(BF16), g_post[l].reshape(1, D_MODEL), gate)
    return x
```

```python
import functools
import math

import jax
import jax.numpy as jnp
from jax import lax
from jax.experimental import pallas as pl
from jax.experimental.pallas import tpu as pltpu

D_MODEL = 1024
CHUNK = 64
RW_WIDTH = 512
RW_HEAD = 64
RW_RANK = 64
RW_SHIFTED = 3 * RW_WIDTH + 2 * RW_RANK
RW_COLS = RW_SHIFTED + RW_WIDTH
DA_WIDTH = 512
DA_HEADS = 4
DA_VDIM = DA_WIDTH // DA_HEADS
DA_QKDIM = DA_VDIM // 2
D_IN = RW_COLS + 4 * DA_WIDTH
RMS_EPS = 1e-6
GN_EPS = 64e-5
SUBLN_EPS = 1e-5

IN_ROWS = 512
IN_SUBTILES = 2
RW_CHUNK = 64
RW_STEP_CHUNKS = 2
RW_GROUP = 2 * RW_HEAD
ATT_TILE = 512
ONES_ROWS = 16
ATT_STEPS = 2
VMEM_LIMIT = 48 * 1024 * 1024
SUBLANES = 8
MOD_PARTS = 3

F32 = jnp.float32
BF16 = jnp.bfloat16
HIGHEST = lax.Precision.HIGHEST


def _silu(t):
    return t * jax.nn.sigmoid(t)


def _dot(a, b, precision=None):
    return jnp.dot(a, b, preferred_element_type=F32, precision=precision)


def _dot_nt(a, b):
    return lax.dot_general(a, b, (((1,), (1,)), ((), ())), preferred_element_type=F32)


def _dot_tn(a, b):
    return lax.dot_general(a, b, (((0,), (0,)), ((), ())), preferred_element_type=F32)


def _split_dot(x, w_exact, terms, *, x_is_lhs):
    acc = None
    for _ in range(terms):
        piece = x.astype(BF16)
        part = _dot(piece, w_exact) if x_is_lhs else _dot(w_exact, piece)
        acc = part if acc is None else acc + part
        x = x - piece.astype(F32)
    return acc


def _mod_kernel(c_ref, w_ref, b_ref, o_ref):
    o_ref[...] = _dot(_silu(c_ref[...]), w_ref[...], HIGHEST) + b_ref[...]


def _modulation(c, w_ada, b_ada):
    batch = c.shape[0]
    return pl.pallas_call(
        _mod_kernel,
        grid=(MOD_PARTS,),
        in_specs=[
            pl.BlockSpec((batch, D_MODEL), lambda j: (0, 0)),
            pl.BlockSpec((D_MODEL, D_MODEL), lambda j: (0, j)),
            pl.BlockSpec((1, D_MODEL), lambda j: (0, j)),
        ],
        out_specs=pl.BlockSpec((batch, D_MODEL), lambda j: (0, j)),
        out_shape=jax.ShapeDtypeStruct((batch, MOD_PARTS * D_MODEL), F32),
        compiler_params=pltpu.CompilerParams(vmem_limit_bytes=VMEM_LIMIT),
        name="adaln_mod",
    )(c, w_ada, b_ada.reshape(1, MOD_PARTS * D_MODEL))


_Q_COLS = (RW_COLS, RW_COLS + DA_WIDTH)
_V_COLS = (RW_COLS + 2 * DA_WIDTH, RW_COLS + 3 * DA_WIDTH)
_ROW_SEGMENTS = (
    (0, RW_SHIFTED),
    (RW_SHIFTED, RW_COLS),
    (RW_COLS + DA_WIDTH, RW_COLS + 2 * DA_WIDTH),
    (RW_COLS + 3 * DA_WIDTH, D_IN),
)
_ROW_WIDTHS = tuple(hi - lo for lo, hi in _ROW_SEGMENTS)
_ROW_DTYPES = (F32, BF16, BF16, BF16)
Q_SCALE = math.log2(math.e) / math.sqrt(DA_QKDIM)


def _in_proj_kernel(x_ref, shift_ref, scale_ref, g_ref, mu_ref, w_ref, *rest):
    out_refs, wt_ref, ubuf = rest[:-2], rest[-2], rest[-1]
    tm = x_ref.shape[1]

    @pl.when((pl.program_id(0) == 0) & (pl.program_id(1) == 0))
    def _():
        wt_ref[0:DA_WIDTH, :] = w_ref[:, _Q_COLS[0]:_Q_COLS[1]].T
        wt_ref[DA_WIDTH:, :] = w_ref[:, _V_COLS[0]:_V_COLS[1]].T

    @pl.when(pl.program_id(1) == 0)
    def _():
        ubuf[0:SUBLANES, :] = jnp.zeros((SUBLANES, RW_SHIFTED), F32)

    qt_ref, vt_ref = out_refs[len(_ROW_SEGMENTS):]
    sub = tm // IN_SUBTILES
    for rows in (slice(t * sub, (t + 1) * sub) for t in range(IN_SUBTILES)):
        x = x_ref[0, rows, :]
        y = x * lax.rsqrt(jnp.mean(x * x, axis=-1, keepdims=True) + RMS_EPS) * g_ref[...]
        h = (y * (1.0 + scale_ref[0]) + shift_ref[0]).astype(BF16)
        for (lo, hi), o_ref in zip(_ROW_SEGMENTS, out_refs):
            u = _dot(h, w_ref[:, lo:hi])
            if lo == 0:
                ubuf[SUBLANES + rows.start:SUBLANES + rows.stop, :] = u
                prev = ubuf[SUBLANES - 1 + rows.start:SUBLANES - 1 + rows.stop, :]
                u = u + (prev - u) * mu_ref[...]
            o_ref[0, rows, :] = u.astype(o_ref.dtype)
        qt = _dot_nt(wt_ref[0:DA_WIDTH, :], h) * Q_SCALE
        vt = _dot_nt(wt_ref[DA_WIDTH:, :], h)
        qt_ref[0, :, 0, :, rows] = qt.astype(BF16).reshape(DA_HEADS, DA_VDIM, sub)
        vt_ref[0, :, 0, :, rows] = vt.astype(BF16).reshape(DA_HEADS, DA_VDIM, sub)
    ubuf[0:SUBLANES, :] = ubuf[tm:tm + SUBLANES, :]


def _in_projection(x, shift, scale, g_pre, mu, w_in_bf16):
    batch, seq, _ = x.shape
    tm = IN_ROWS
    row_spec = lambda width: pl.BlockSpec((1, tm, width), lambda b, i: (b, i, 0))
    vec_spec = pl.BlockSpec((1, 1, D_MODEL), lambda b, i: (b, 0, 0))
    per_tile = ATT_TILE // tm
    t_spec = pl.BlockSpec((1, DA_HEADS, 1, DA_VDIM, tm),
                          lambda b, i: (b, 0, i // per_tile, 0, i % per_tile))
    t_shape = jax.ShapeDtypeStruct((batch, DA_HEADS, seq // ATT_TILE, DA_VDIM, ATT_TILE), BF16)
    return pl.pallas_call(
        _in_proj_kernel,
        grid=(batch, seq // tm),
        in_specs=[
            row_spec(D_MODEL), vec_spec, vec_spec,
            pl.BlockSpec((1, D_MODEL), lambda b, i: (0, 0)),
            pl.BlockSpec((1, RW_SHIFTED), lambda b, i: (0, 0)),
            pl.BlockSpec((D_MODEL, D_IN), lambda b, i: (0, 0), pipeline_mode=pl.Buffered(1)),
        ],
        out_specs=[row_spec(w) for w in _ROW_WIDTHS] + [t_spec, t_spec],
        out_shape=[jax.ShapeDtypeStruct((batch, seq, w), dt)
                   for w, dt in zip(_ROW_WIDTHS, _ROW_DTYPES)] + [t_shape, t_shape],
        scratch_shapes=[pltpu.VMEM((2 * DA_WIDTH, D_MODEL), BF16),
                        pltpu.VMEM((tm + SUBLANES, RW_SHIFTED), F32)],
        compiler_params=pltpu.CompilerParams(
            dimension_semantics=("arbitrary", "arbitrary"),
            vmem_limit_bytes=VMEM_LIMIT),
        name="in_proj",
    )(x, shift, scale, g_pre, mu, w_in_bf16)


def _rwkv_kernel(feat_ref, gate_ref, w0_ref, ww2_ref, a0_ref, wa2_ref,
                 kk_ref, ka_ref, rk_ref, lnw_ref, lnb_ref,
                 yda_ref, x_ref, wout_ref, gpost_ref, mgate_ref, o_ref, state):
    C, W, N, L = RW_CHUNK, RW_WIDTH, RW_HEAD, RW_GROUP
    heads = L // N
    batch, rows = feat_ref.shape[0], feat_ref.shape[1]
    chunks = rows // C
    ci = pl.program_id(0)

    @pl.when(ci == 0)
    def _():
        state[...] = jnp.zeros(state.shape, F32)

    iota = lambda shape, axis: lax.broadcasted_iota(jnp.int32, shape, axis)
    row2, col2 = iota((2 * C, L), 0), iota((2 * C, L), 1) & (N - 1)
    tri = ((row2 & (C - 1)) > col2) | ((row2 >= C) & ((row2 & (C - 1)) == col2))
    r1, c1 = iota((C, L), 0), iota((C, L), 1) & (N - 1)
    eye = (r1 == c1).astype(F32)
    levels = []
    for bit in range(C.bit_length() - 1):
        levels.append(((r1 >> (bit + 1)) == (c1 >> (bit + 1)))
                      & (((r1 >> bit) & 1) == 1) & (((c1 >> bit) & 1) == 0))
    head_bits = N.bit_length() - 1
    same_head = (iota((L, L), 0) >> head_bits) == (iota((L, L), 1) >> head_bits)
    seg = same_head.astype(BF16)
    cum = (iota((C, C), 0) >= iota((C, C), 1)).astype(BF16)

    def block_diag(t, keep=same_head):
        return jnp.where(keep, jnp.concatenate([t] * heads, axis=0), 0.0).astype(BF16)

    level_keep = [same_head & jnp.concatenate([level] * heads, axis=0) for level in levels]

    def head_sums(ts):
        n = ts[0].shape[0]
        sums = jnp.concatenate(
            [_dot(jnp.concatenate([t[:, g * L:(g + 1) * L] for t in ts], axis=0).astype(BF16), seg)
             for g in range(W // L)], axis=-1)
        return [sums[i * n:(i + 1) * n] for i in range(len(ts))]

    roots = []
    for b in range(batch):
        feat = feat_ref[b]
        r, k, v = feat[:, 0:W], feat[:, W:2 * W], feat[:, 2 * W:3 * W]
        w_lo = feat[:, 3 * W:3 * W + RW_RANK]
        a_lo = feat[:, 3 * W + RW_RANK:]
        z = w0_ref[...] + _dot(jnp.tanh(w_lo).astype(BF16), ww2_ref[...])
        logw = -math.exp(-0.5) * jax.nn.sigmoid(z)
        a_pre = a0_ref[...] + _dot(a_lo.astype(BF16), wa2_ref[...])
        kk_raw = k * kk_ref[...]
        cs = [_split_dot(logw[ch * C:(ch + 1) * C], cum, 2, x_is_lhs=False)
              for ch in range(chunks)]
        roots.append([r, k, v, logw, a_pre, kk_raw, cs])
    for root, norm2 in zip(roots, head_sums([root[5] * root[5] for root in roots])):
        root.append(norm2)

    prep = {}

    def prepare(ch, anchor=None):
        cr = slice(ch * C, (ch + 1) * C)
        zero = 0.0
        if anchor is not None:
            bits = pltpu.bitcast(anchor[0:1, :], jnp.uint32)
            half_word = jnp.uint32(16)
            bits = lax.shift_right_logical(lax.shift_right_logical(bits, half_word), half_word)
            zero = jnp.concatenate([pltpu.bitcast(bits, F32)] * (W // L), axis=1)
        grs = []
        for b in range(batch):
            r, k, v, logw, a_pre, kk_raw, cs_all, norm2 = roots[b]
            r, k, v, logw, cs = r[cr], k[cr], v[cr], logw[cr], cs_all[ch] + zero
            a = jax.nn.sigmoid(a_pre[cr] + zero)
            kk = kk_raw[cr] * lax.rsqrt(jnp.maximum(norm2[cr], 1e-24))
            bb = a * kk
            k2 = k * (1.0 + (a - 1.0) * ka_ref[...])
            cs_last = cs[C - 1:C, :]
            g_all = jnp.exp(cs_last)
            g_inv = jnp.exp(-cs)
            g_end = jnp.exp(cs_last - cs)
            kk_t = (kk * jnp.exp(cs - logw)).astype(BF16)
            r_t = (r * jnp.exp(cs)).astype(BF16)
            k_inv, b_inv = k2 * g_inv, bb * g_inv
            k_end, nb_end = (k2 * g_end).astype(BF16), (-(bb * g_end)).astype(BF16)
            prep[b, ch] = (r, k2, v)
            for g in range(W // L):
                gl = slice(g * L, (g + 1) * L)
                grs.append(dict(
                    b=b, g=g, g_all=g_all[:, gl], v=v[:, gl],
                    lhs=jnp.concatenate([kk_t[:, gl], r_t[:, gl]], axis=0),
                    rhs=jnp.concatenate([block_diag(b_inv[:, gl]), block_diag(k_inv[:, gl])],
                                        axis=0),
                    end=jnp.concatenate([k_end[:, gl], nb_end[:, gl]], axis=0)))
        return grs

    def output_stage(ch, outs):
        cr = slice(ch * C, (ch + 1) * C)
        o = [jnp.concatenate([outs[b, g] for g in range(W // L)], axis=-1) for b in range(batch)]
        mean = head_sums(o)
        yield
        d = [t - m * (1.0 / N) for t, m in zip(o, mean)]
        sums = head_sums([t * t for t in d]
                         + [prep[b, ch][0] * prep[b, ch][1] * rk_ref[...] for b in range(batch)])
        var, r_k = sums[:batch], sums[batch:]
        yield
        rows_in = []
        for b in range(batch):
            t = d[b] * lax.rsqrt(var[b] * (1.0 / N) + GN_EPS) * lnw_ref[...] + lnb_ref[...]
            bonus = r_k[b] * prep[b, ch][2]
            y_rw = ((t + bonus) * _silu(gate_ref[b, cr, :].astype(F32))).astype(BF16)
            rows_in.append(jnp.concatenate([y_rw, yda_ref[b, cr, :]], axis=-1))
        y = _dot(jnp.concatenate(rows_in, axis=0), wout_ref[...])
        yield
        y = y * lax.rsqrt(jnp.mean(y * y, axis=-1, keepdims=True) + RMS_EPS) * gpost_ref[...]
        for b in range(batch):
            o_ref[b, cr, :] = x_ref[b, cr, :] + mgate_ref[b] * y[b * C:(b + 1) * C]

    carried = {(b, g): state[b, g] for b in range(batch) for g in range(W // L)}
    pending = iter(())
    nxt = prepare(0)
    for ch in range(chunks):
        grs = nxt
        for gr in grs:
            m = _dot_nt(gr["lhs"], gr["rhs"])
            gr["a_b"] = jnp.where(tri, m[:, :L], 0.0)
            gr["a_k"] = jnp.where(tri, m[:, L:], 0.0).astype(BF16)
            gr["t"] = eye - jnp.where(levels[0], gr["a_b"][:C], 0.0)
        for gr in grs:
            gr["st"] = carried[gr["b"], gr["g"]]
            gr["x"] = _dot(jnp.concatenate([gr["a_k"], gr["lhs"]], axis=1),
                           jnp.concatenate([block_diag(gr["v"]), block_diag(gr["st"]).T], axis=0))
        if ch + 1 < chunks:
            nxt = prepare(ch + 1, anchor=grs[-1]["a_b"])
        for keep in level_keep[1:]:
            for gr in grs:
                gr["inner"] = _dot(gr["t"].astype(BF16),
                                   block_diag(gr["a_b"][:C], keep)).astype(BF16)
            for gr in grs:
                gr["t"] = gr["t"] - _dot(gr["inner"], block_diag(gr["t"]))
            next(pending, None)
        for gr in grs:
            gr["uu"] = _dot(gr["t"].astype(BF16), block_diag(gr["x"][:C]))
        outs = {}
        for gr in grs:
            outs[gr["b"], gr["g"]] = gr["x"][C:] - _dot(gr["a_b"][C:].astype(BF16),
                                                       block_diag(gr["uu"]))
        for gr in grs:
            vu = jnp.concatenate([gr["v"], gr["uu"]], axis=0).astype(BF16)
            delta = jnp.where(same_head, _dot_tn(vu, gr["end"]), 0.0)
            carried[gr["b"], gr["g"]] = gr["st"] * gr["g_all"] + sum(
                delta[h * N:(h + 1) * N] for h in range(heads))
        for _ in pending:
            pass
        pending = output_stage(ch, outs)
    for (b, g), st in carried.items():
        state[b, g] = st
    for _ in pending:
        pass


def _rwkv_and_output(feat, gate, w0, w_w2, a0, w_a2, k_k, k_a, r_k, lnx_w, lnx_b,
                     y_da, x, w_out_bf16, g_post, mod_gate):
    batch, seq, _ = feat.shape
    rows = RW_CHUNK * RW_STEP_CHUNKS
    full = lambda shape: pl.BlockSpec(shape, lambda i: (0,) * len(shape))
    vec = lambda t: t.reshape(1, -1)
    return pl.pallas_call(
        _rwkv_kernel,
        grid=(seq // rows,),
        in_specs=[
            pl.BlockSpec((batch, rows, RW_SHIFTED), lambda i: (0, i, 0)),
            pl.BlockSpec((batch, rows, RW_WIDTH), lambda i: (0, i, 0)),
            full((1, RW_WIDTH)), full((RW_RANK, RW_WIDTH)),
            full((1, RW_WIDTH)), full((RW_RANK, RW_WIDTH)),
            full((1, RW_WIDTH)), full((1, RW_WIDTH)), full((1, RW_WIDTH)),
            full((1, RW_WIDTH)), full((1, RW_WIDTH)),
            pl.BlockSpec((batch, rows, DA_WIDTH), lambda i: (0, i, 0)),
            pl.BlockSpec((batch, rows, D_MODEL), lambda i: (0, i, 0)),
            pl.BlockSpec((D_MODEL, D_MODEL), lambda i: (0, 0), pipeline_mode=pl.Buffered(1)),
            full((1, D_MODEL)), full((batch, 1, D_MODEL)),
        ],
        out_specs=pl.BlockSpec((batch, rows, D_MODEL), lambda i: (0, i, 0)),
        out_shape=jax.ShapeDtypeStruct(x.shape, x.dtype),
        scratch_shapes=[
            pltpu.VMEM((batch, RW_WIDTH // RW_GROUP, RW_HEAD, RW_GROUP), F32),
        ],
        compiler_params=pltpu.CompilerParams(
            dimension_semantics=("arbitrary",),
            vmem_limit_bytes=VMEM_LIMIT),
        name="rwkv7_chunked",
    )(feat, gate, vec(w0), w_w2.astype(BF16), vec(a0), w_a2.astype(BF16),
      vec(k_k), vec(k_a), vec(r_k), vec(lnx_w), vec(lnx_b),
      y_da, x, w_out_bf16, g_post, mod_gate)


def _attn_kernel(lq1_ref, lk1_ref, lq2_ref, lk2_ref, sw_ref, qt_ref, k_ref, vt_ref,
                 g_ref, o_ref, s_buf, p_buf, acc_buf, *, lambda_init):
    T, HALF = ATT_TILE, ATT_TILE // 2
    tiles = k_ref.shape[1] // T

    comp = lax.broadcasted_iota(jnp.int32, (DA_VDIM, T), 0) < DA_QKDIM
    key = lax.broadcasted_iota(jnp.int32, (HALF, HALF), 0)
    query = lax.broadcasted_iota(jnp.int32, (HALF, HALF), 1)
    visible = (query // CHUNK) >= (key // CHUNK)
    lam = (jnp.exp(jnp.sum(lq1_ref[...] * lk1_ref[...], axis=-1, keepdims=True))
           - jnp.exp(jnp.sum(lq2_ref[...] * lk2_ref[...], axis=-1, keepdims=True))
           + lambda_init)

    for step in range(ATT_STEPS):
        order = [i for k in range(step, tiles // 2, ATT_STEPS) for i in (tiles - 1 - k, k)]
        pl.when(pl.program_id(2) == step)(functools.partial(
            _attn_tasks, [(i, j) for i in order for j in range(i + 1)],
            comp, visible, lam, sw_ref, qt_ref, k_ref, vt_ref, g_ref, o_ref,
            s_buf, p_buf, acc_buf, lambda_init))


def _attn_tasks(tasks, comp, visible, lam, sw_ref, qt_ref, k_ref, vt_ref, g_ref, o_ref,
                s_buf, p_buf, acc_buf, lambda_init):
    T, HALF = ATT_TILE, ATT_TILE // 2
    qt_parts = {}

    def qt_c(i):
        if i not in qt_parts:
            qt = qt_ref[0, 0, i]
            zero = jnp.zeros_like(qt)
            qt_parts[i] = (jnp.where(comp, qt, zero), jnp.where(comp, zero, qt))
        return qt_parts[i]

    def colmax(s):
        return jnp.max(s, axis=0, keepdims=True)

    def scores(n):
        i, j = tasks[n]
        kb = k_ref[0, j * T:(j + 1) * T, :]
        blk_max = []
        for c in range(2):
            q = qt_c(i)[c]
            if j < i:
                s = _dot(kb, q)
                s_buf[n % 2, c] = s
                blk_max.append(colmax(s))
            else:
                s_r = _dot(kb, q[:, HALF:])
                s_b = jnp.where(visible, s_r[HALF:], -jnp.inf)
                s_l = jnp.where(visible, _dot(kb[:HALF], q[:, :HALF]), -jnp.inf)
                s_buf[n % 2, c, :HALF, HALF:] = s_r[:HALF]
                s_buf[n % 2, c, HALF:, HALF:] = s_b
                s_buf[n % 2, c, :HALF, :HALF] = s_l
                blk_max.append(jnp.concatenate(
                    [colmax(s_l), jnp.maximum(colmax(s_r[:HALF]), colmax(s_b))], axis=1))
        return blk_max

    def softmax(n, blk_maxes, maxes):
        i, j = tasks[n]
        new, alphas = [], []
        for c in range(2):
            if j == 0:
                m_new, alpha = blk_maxes[c], None
            else:
                m_new = jnp.maximum(maxes[c], blk_maxes[c])
                alpha = jnp.exp2(maxes[c] - m_new)
            prob = lambda s, m: jnp.exp2((s - m).astype(BF16))
            if j < i:
                p_buf[n % 2, c] = prob(s_buf[n % 2, c], m_new)
            else:
                left, right = m_new[:, :HALF], m_new[:, HALF:]
                p_buf[n % 2, c, :HALF, :HALF] = prob(s_buf[n % 2, c, :HALF, :HALF], left)
                p_buf[n % 2, c, :HALF, HALF:] = prob(s_buf[n % 2, c, :HALF, HALF:], right)
                p_buf[n % 2, c, HALF:, HALF:] = prob(s_buf[n % 2, c, HALF:, HALF:], right)
            new.append(m_new)
            alphas.append(alpha)
        return new, alphas

    ones_rows = (lax.broadcasted_iota(jnp.int32, (ONES_ROWS, T), 0) == 0).astype(BF16)

    def values(n, alphas):
        i, j = tasks[n]
        vtb = jnp.concatenate([vt_ref[0, 0, j], ones_rows], axis=0)
        for c in range(2):
            if j < i:
                pv = _dot(vtb, p_buf[n % 2, c])
            else:
                pv = jnp.concatenate(
                    [_dot(vtb[:, :HALF], p_buf[n % 2, c, :HALF, :HALF]),
                     _dot(vtb, p_buf[n % 2, c, :, HALF:])], axis=1)
            acc_buf[i % 2, c] = pv if j == 0 else alphas[c] * acc_buf[i % 2, c] + pv

    def finish(i):
        num0, num1 = acc_buf[i % 2, 0, :DA_VDIM], acc_buf[i % 2, 1, :DA_VDIM]
        l0 = acc_buf[i % 2, 0, DA_VDIM:DA_VDIM + 1]
        l1 = acc_buf[i % 2, 1, DA_VDIM:DA_VDIM + 1]
        o = (num0 * (1.0 / l0) - num1 * (lam / l1)).T
        o = o * lax.rsqrt(jnp.mean(o * o, axis=-1, keepdims=True) + SUBLN_EPS)
        o = o * sw_ref[...] * (1.0 - lambda_init)
        rows = slice(i * T, (i + 1) * T)
        o_ref[0, rows, :] = (o * _silu(g_ref[0, rows, :].astype(F32))).astype(o_ref.dtype)

    blk_max = scores(0)
    maxes, pending = None, None
    for n, (i, j) in enumerate(tasks):
        next_max = scores(n + 1) if n + 1 < len(tasks) else None
        if pending is not None:
            values(pending[0], pending[1])
            if pending[2]:
                finish(tasks[pending[0]][0])
        maxes, alphas = softmax(n, blk_max, maxes)
        pending = (n, alphas, j == i)
        blk_max = next_max
    values(pending[0], pending[1])
    finish(tasks[-1][0])


def _diff_attention(qt, k, vt, gate, lam_q1, lam_k1, lam_q2, lam_k2, subln_w, lambda_init):
    batch, seq, _ = k.shape
    T = ATT_TILE
    small = lambda n: pl.BlockSpec((1, n), lambda b, h, s: (0, 0))
    rows = pl.BlockSpec((1, seq, DA_VDIM), lambda b, h, s: (b, 0, h))
    cols = pl.BlockSpec((1, 1, seq // T, DA_VDIM, T), lambda b, h, s: (b, h, 0, 0, 0))
    return pl.pallas_call(
        functools.partial(_attn_kernel, lambda_init=lambda_init),
        grid=(batch, DA_HEADS, ATT_STEPS),
        in_specs=[small(DA_QKDIM)] * 4 + [small(DA_VDIM), cols, rows, cols, rows],
        out_specs=rows,
        out_shape=jax.ShapeDtypeStruct((batch, seq, DA_WIDTH), BF16),
        scratch_shapes=[pltpu.VMEM((2, 2, T, T), F32),
                        pltpu.VMEM((2, 2, T, T), BF16),
                        pltpu.VMEM((2, 2, DA_VDIM + ONES_ROWS, T), F32)],
        compiler_params=pltpu.CompilerParams(
            dimension_semantics=("parallel", "parallel", "arbitrary"),
            vmem_limit_bytes=VMEM_LIMIT),
        name="diff_attn",
    )(*(t.reshape(1, -1) for t in (lam_q1, lam_k1, lam_q2, lam_k2, subln_w)),
      qt, k, vt, gate)


def kernel(x, c, w_ada, b_ada, g_pre, g_post, w_in, w_out, rw_mu, rw_w0, rw_w_w2,
           rw_a0, rw_w_a2, rw_k_k, rw_k_a, rw_r_k, rw_lnx_w, rw_lnx_b,
           da_lam_q1, da_lam_k1, da_lam_q2, da_lam_k2, da_subln_w):
    batch = x.shape[0]
    depth = w_in.shape[0]
    for l in range(depth):
        lambda_init = 0.8 - 0.6 * math.exp(-0.3 * l)
        mod = _modulation(c, w_ada[l], b_ada[l]).reshape(batch, MOD_PARTS, 1, D_MODEL)
        shift, scale, gate = mod[:, 0], mod[:, 1], mod[:, 2]
        feat, rw_gate, dk, dg, dqt, dvt = _in_projection(
            x, shift, scale, g_pre[l].reshape(1, D_MODEL), rw_mu[l].reshape(1, RW_SHIFTED),
            w_in[l].astype(BF16))
        y_da = _diff_attention(dqt, dk, dvt, dg, da_lam_q1[l], da_lam_k1[l],
                               da_lam_q2[l], da_lam_k2[l], da_subln_w[l], lambda_init)
        x = _rwkv_and_output(feat, rw_gate, rw_w0[l], rw_w_w2[l], rw_a0[l], rw_w_a2[l],
                             rw_k_k[l], rw_k_a[l], rw_r_k[l], rw_lnx_w[l], rw_lnx_b[l],
                             y_da, x, w_out[l].astype(BF16), g_post[l].reshape(1, D_MODEL), gate)
    return x
```

```python
import functools
import math

import jax
import jax.numpy as jnp
from jax import lax
from jax.experimental import pallas as pl
from jax.experimental.pallas import tpu as pltpu

D_MODEL = 1024
CHUNK = 64
RW_WIDTH = 512
RW_HEAD = 64
RW_RANK = 64
RW_SHIFTED = 3 * RW_WIDTH + 2 * RW_RANK
RW_COLS = RW_SHIFTED + RW_WIDTH
DA_WIDTH = 512
DA_HEADS = 4
DA_VDIM = DA_WIDTH // DA_HEADS
DA_QKDIM = DA_VDIM // 2
D_IN = RW_COLS + 4 * DA_WIDTH
RMS_EPS = 1e-6
GN_EPS = 64e-5
SUBLN_EPS = 1e-5

IN_ROWS = 512
IN_SUBTILES = 2
RW_CHUNK = 64
RW_STEP_CHUNKS = 2
RW_GROUP = 2 * RW_HEAD
ATT_TILE = 512
ONES_ROWS = 16
ATT_STEPS = 2
VMEM_LIMIT = 48 * 1024 * 1024
SUBLANES = 8
MOD_PARTS = 3

F32 = jnp.float32
BF16 = jnp.bfloat16
HIGHEST = lax.Precision.HIGHEST


def _silu(t):
    return t * jax.nn.sigmoid(t)


def _dot(a, b, precision=None):
    return jnp.dot(a, b, preferred_element_type=F32, precision=precision)


def _dot_nt(a, b):
    return lax.dot_general(a, b, (((1,), (1,)), ((), ())), preferred_element_type=F32)


def _dot_tn(a, b):
    return lax.dot_general(a, b, (((0,), (0,)), ((), ())), preferred_element_type=F32)


def _split_dot(x, w_exact, terms, *, x_is_lhs):
    acc = None
    for _ in range(terms):
        piece = x.astype(BF16)
        part = _dot(piece, w_exact) if x_is_lhs else _dot(w_exact, piece)
        acc = part if acc is None else acc + part
        x = x - piece.astype(F32)
    return acc


def _mod_kernel(c_ref, w_ref, b_ref, o_ref):
    o_ref[...] = _dot(_silu(c_ref[...]), w_ref[...], HIGHEST) + b_ref[...]


def _modulation(c, w_ada, b_ada):
    batch = c.shape[0]
    return pl.pallas_call(
        _mod_kernel,
        grid=(MOD_PARTS,),
        in_specs=[
            pl.BlockSpec((batch, D_MODEL), lambda j: (0, 0)),
            pl.BlockSpec((D_MODEL, D_MODEL), lambda j: (0, j)),
            pl.BlockSpec((1, D_MODEL), lambda j: (0, j)),
        ],
        out_specs=pl.BlockSpec((batch, D_MODEL), lambda j: (0, j)),
        out_shape=jax.ShapeDtypeStruct((batch, MOD_PARTS * D_MODEL), F32),
        compiler_params=pltpu.CompilerParams(vmem_limit_bytes=VMEM_LIMIT),
        name="adaln_mod",
    )(c, w_ada, b_ada.reshape(1, MOD_PARTS * D_MODEL))


_Q_COLS = (RW_COLS, RW_COLS + DA_WIDTH)
_V_COLS = (RW_COLS + 2 * DA_WIDTH, RW_COLS + 3 * DA_WIDTH)
_ROW_SEGMENTS = (
    (0, RW_SHIFTED),
    (RW_SHIFTED, RW_COLS),
    (RW_COLS + DA_WIDTH, RW_COLS + 2 * DA_WIDTH),
    (RW_COLS + 3 * DA_WIDTH, D_IN),
)
_ROW_WIDTHS = tuple(hi - lo for lo, hi in _ROW_SEGMENTS)
_ROW_DTYPES = (F32, BF16, BF16, BF16)
Q_SCALE = math.log2(math.e) / math.sqrt(DA_QKDIM)


def _in_proj_kernel(x_ref, shift_ref, scale_ref, g_ref, mu_ref, w_ref, *rest):
    out_refs, wt_ref, ubuf = rest[:-2], rest[-2], rest[-1]
    tm = x_ref.shape[1]

    @pl.when((pl.program_id(0) == 0) & (pl.program_id(1) == 0))
    def _():
        wt_ref[0:DA_WIDTH, :] = w_ref[:, _Q_COLS[0]:_Q_COLS[1]].T
        wt_ref[DA_WIDTH:, :] = w_ref[:, _V_COLS[0]:_V_COLS[1]].T

    @pl.when(pl.program_id(1) == 0)
    def _():
        ubuf[0:SUBLANES, :] = jnp.zeros((SUBLANES, RW_SHIFTED), F32)

    qt_ref, vt_ref = out_refs[len(_ROW_SEGMENTS):]
    sub = tm // IN_SUBTILES
    for rows in (slice(t * sub, (t + 1) * sub) for t in range(IN_SUBTILES)):
        x = x_ref[0, rows, :]
        y = x * lax.rsqrt(jnp.mean(x * x, axis=-1, keepdims=True) + RMS_EPS) * g_ref[...]
        h = (y * (1.0 + scale_ref[0]) + shift_ref[0]).astype(BF16)
        for (lo, hi), o_ref in zip(_ROW_SEGMENTS, out_refs):
            u = _dot(h, w_ref[:, lo:hi])
            if lo == 0:
                ubuf[SUBLANES + rows.start:SUBLANES + rows.stop, :] = u
                prev = ubuf[SUBLANES - 1 + rows.start:SUBLANES - 1 + rows.stop, :]
                u = u + (prev - u) * mu_ref[...]
            o_ref[0, rows, :] = u.astype(o_ref.dtype)
        qt = _dot_nt(wt_ref[0:DA_WIDTH, :], h) * Q_SCALE
        vt = _dot_nt(wt_ref[DA_WIDTH:, :], h)
        qt_ref[0, :, 0, :, rows] = qt.astype(BF16).reshape(DA_HEADS, DA_VDIM, sub)
        vt_ref[0, :, 0, :, rows] = vt.astype(BF16).reshape(DA_HEADS, DA_VDIM, sub)
    ubuf[0:SUBLANES, :] = ubuf[tm:tm + SUBLANES, :]


def _in_projection(x, shift, scale, g_pre, mu, w_in_bf16):
    batch, seq, _ = x.shape
    tm = IN_ROWS
    row_spec = lambda width: pl.BlockSpec((1, tm, width), lambda b, i: (b, i, 0))
    vec_spec = pl.BlockSpec((1, 1, D_MODEL), lambda b, i: (b, 0, 0))
    per_tile = ATT_TILE // tm
    t_spec = pl.BlockSpec((1, DA_HEADS, 1, DA_VDIM, tm),
                          lambda b, i: (b, 0, i // per_tile, 0, i % per_tile))
    t_shape = jax.ShapeDtypeStruct((batch, DA_HEADS, seq // ATT_TILE, DA_VDIM, ATT_TILE), BF16)
    return pl.pallas_call(
        _in_proj_kernel,
        grid=(batch, seq // tm),
        in_specs=[
            row_spec(D_MODEL), vec_spec, vec_spec,
            pl.BlockSpec((1, D_MODEL), lambda b, i: (0, 0)),
            pl.BlockSpec((1, RW_SHIFTED), lambda b, i: (0, 0)),
            pl.BlockSpec((D_MODEL, D_IN), lambda b, i: (0, 0), pipeline_mode=pl.Buffered(1)),
        ],
        out_specs=[row_spec(w) for w in _ROW_WIDTHS] + [t_spec, t_spec],
        out_shape=[jax.ShapeDtypeStruct((batch, seq, w), dt)
                   for w, dt in zip(_ROW_WIDTHS, _ROW_DTYPES)] + [t_shape, t_shape],
        scratch_shapes=[pltpu.VMEM((2 * DA_WIDTH, D_MODEL), BF16),
                        pltpu.VMEM((tm + SUBLANES, RW_SHIFTED), F32)],
        compiler_params=pltpu.CompilerParams(
            dimension_semantics=("arbitrary", "arbitrary"),
            vmem_limit_bytes=VMEM_LIMIT),
        name="in_proj",
    )(x, shift, scale, g_pre, mu, w_in_bf16)


def _rwkv_kernel(feat_ref, gate_ref, w0_ref, ww2_ref, a0_ref, wa2_ref,
                 kk_ref, ka_ref, rk_ref, lnw_ref, lnb_ref,
                 yda_ref, x_ref, wout_ref, gpost_ref, mgate_ref, o_ref, state):
    C, W, N, L = RW_CHUNK, RW_WIDTH, RW_HEAD, RW_GROUP
    heads = L // N
    batch, rows = feat_ref.shape[0], feat_ref.shape[1]
    chunks = rows // C
    ci = pl.program_id(0)

    @pl.when(ci == 0)
    def _():
        state[...] = jnp.zeros(state.shape, F32)

    iota = lambda shape, axis: lax.broadcasted_iota(jnp.int32, shape, axis)
    row2, col2 = iota((2 * C, L), 0), iota((2 * C, L), 1) & (N - 1)
    tri = ((row2 & (C - 1)) > col2) | ((row2 >= C) & ((row2 & (C - 1)) == col2))
    r1, c1 = iota((C, L), 0), iota((C, L), 1) & (N - 1)
    eye = (r1 == c1).astype(F32)
    levels = []
    for bit in range(C.bit_length() - 1):
        levels.append(((r1 >> (bit + 1)) == (c1 >> (bit + 1)))
                      & (((r1 >> bit) & 1) == 1) & (((c1 >> bit) & 1) == 0))
    head_bits = N.bit_length() - 1
    same_head = (iota((L, L), 0) >> head_bits) == (iota((L, L), 1) >> head_bits)
    seg = same_head.astype(BF16)
    cum = (iota((C, C), 0) >= iota((C, C), 1)).astype(BF16)

    def block_diag(t, keep=same_head):
        return jnp.where(keep, jnp.concatenate([t] * heads, axis=0), 0.0).astype(BF16)

    level_keep = [same_head & jnp.concatenate([level] * heads, axis=0) for level in levels]

    def head_sums(t):
        return jnp.concatenate(
            [_dot(t[:, g * L:(g + 1) * L].astype(BF16), seg) for g in range(W // L)], axis=-1)

    roots = []
    for b in range(batch):
        feat = feat_ref[b]
        r, k, v = feat[:, 0:W], feat[:, W:2 * W], feat[:, 2 * W:3 * W]
        w_lo = feat[:, 3 * W:3 * W + RW_RANK]
        a_lo = feat[:, 3 * W + RW_RANK:]
        z = w0_ref[...] + _dot(jnp.tanh(w_lo).astype(BF16), ww2_ref[...])
        logw = -math.exp(-0.5) * jax.nn.sigmoid(z)
        a_pre = a0_ref[...] + _dot(a_lo.astype(BF16), wa2_ref[...])
        kk_raw = k * kk_ref[...]
        norm2 = head_sums(kk_raw * kk_raw)
        cs = [_split_dot(logw[ch * C:(ch + 1) * C], cum, 2, x_is_lhs=False)
              for ch in range(chunks)]
        roots.append((r, k, v, logw, a_pre, kk_raw, norm2, cs))

    prep = {}

    def prepare(ch, anchor=None):
        cr = slice(ch * C, (ch + 1) * C)
        zero = 0.0
        if anchor is not None:
            bits = pltpu.bitcast(anchor[0:1, :], jnp.uint32)
            half_word = jnp.uint32(16)
            bits = lax.shift_right_logical(lax.shift_right_logical(bits, half_word), half_word)
            zero = jnp.concatenate([pltpu.bitcast(bits, F32)] * (W // L), axis=1)
        grs = []
        for b in range(batch):
            r, k, v, logw, a_pre, kk_raw, norm2, cs_all = roots[b]
            r, k, v, logw, cs = r[cr], k[cr], v[cr], logw[cr], cs_all[ch] + zero
            a = jax.nn.sigmoid(a_pre[cr] + zero)
            kk = kk_raw[cr] * lax.rsqrt(jnp.maximum(norm2[cr], 1e-24))
            bb = a * kk
            k2 = k * (1.0 + (a - 1.0) * ka_ref[...])
            cs_last = cs[C - 1:C, :]
            g_all = jnp.exp(cs_last)
            g_inv = jnp.exp(-cs)
            g_end = jnp.exp(cs_last - cs)
            kk_t = (kk * jnp.exp(cs - logw)).astype(BF16)
            r_t = (r * jnp.exp(cs)).astype(BF16)
            k_inv, b_inv = k2 * g_inv, bb * g_inv
            k_end, nb_end = (k2 * g_end).astype(BF16), (-(bb * g_end)).astype(BF16)
            prep[b, ch] = (r, k2, v)
            for g in range(W // L):
                gl = slice(g * L, (g + 1) * L)
                grs.append(dict(
                    b=b, g=g, g_all=g_all[:, gl], v=v[:, gl],
                    lhs=jnp.concatenate([kk_t[:, gl], r_t[:, gl]], axis=0),
                    rhs=jnp.concatenate([block_diag(b_inv[:, gl]), block_diag(k_inv[:, gl])],
                                        axis=0),
                    end=jnp.concatenate([k_end[:, gl], nb_end[:, gl]], axis=0)))
        return grs

    def output_stage(ch, outs):
        cr = slice(ch * C, (ch + 1) * C)
        o = [jnp.concatenate([outs[b, g] for g in range(W // L)], axis=-1) for b in range(batch)]
        mean = [head_sums(t) * (1.0 / N) for t in o]
        yield
        d = [t - m for t, m in zip(o, mean)]
        var = [head_sums(t * t) * (1.0 / N) for t in d]
        bonus = [head_sums(prep[b, ch][0] * prep[b, ch][1] * rk_ref[...]) * prep[b, ch][2]
                 for b in range(batch)]
        yield
        rows_in = []
        for b in range(batch):
            t = d[b] * lax.rsqrt(var[b] + GN_EPS) * lnw_ref[...] + lnb_ref[...]
            y_rw = ((t + bonus[b]) * _silu(gate_ref[b, cr, :].astype(F32))).astype(BF16)
            rows_in.append(jnp.concatenate([y_rw, yda_ref[b, cr, :]], axis=-1))
        y = _dot(jnp.concatenate(rows_in, axis=0), wout_ref[...])
        yield
        y = y * lax.rsqrt(jnp.mean(y * y, axis=-1, keepdims=True) + RMS_EPS) * gpost_ref[...]
        for b in range(batch):
            o_ref[b, cr, :] = x_ref[b, cr, :] + mgate_ref[b] * y[b * C:(b + 1) * C]

    carried = {(b, g): state[b, g] for b in range(batch) for g in range(W // L)}
    pending = iter(())
    nxt = prepare(0)
    for ch in range(chunks):
        grs = nxt
        for gr in grs:
            m = _dot_nt(gr["lhs"], gr["rhs"])
            gr["a_b"] = jnp.where(tri, m[:, :L], 0.0)
            gr["a_k"] = jnp.where(tri, m[:, L:], 0.0).astype(BF16)
            gr["t"] = eye - jnp.where(levels[0], gr["a_b"][:C], 0.0)
        for gr in grs:
            gr["st"] = carried[gr["b"], gr["g"]]
            gr["x"] = _dot(jnp.concatenate([gr["a_k"], gr["lhs"]], axis=1),
                           jnp.concatenate([block_diag(gr["v"]), block_diag(gr["st"]).T], axis=0))
        if ch + 1 < chunks:
            nxt = prepare(ch + 1, anchor=grs[-1]["a_b"])
        for keep in level_keep[1:]:
            for gr in grs:
                gr["inner"] = _dot(gr["t"].astype(BF16),
                                   block_diag(gr["a_b"][:C], keep)).astype(BF16)
            for gr in grs:
                gr["t"] = gr["t"] - _dot(gr["inner"], block_diag(gr["t"]))
            next(pending, None)
        for gr in grs:
            gr["uu"] = _dot(gr["t"].astype(BF16), block_diag(gr["x"][:C]))
        outs = {}
        for gr in grs:
            outs[gr["b"], gr["g"]] = gr["x"][C:] - _dot(gr["a_b"][C:].astype(BF16),
                                                       block_diag(gr["uu"]))
        for gr in grs:
            vu = jnp.concatenate([gr["v"], gr["uu"]], axis=0).astype(BF16)
            delta = jnp.where(same_head, _dot_tn(vu, gr["end"]), 0.0)
            carried[gr["b"], gr["g"]] = gr["st"] * gr["g_all"] + sum(
                delta[h * N:(h + 1) * N] for h in range(heads))
        for _ in pending:
            pass
        pending = output_stage(ch, outs)
    for (b, g), st in carried.items():
        state[b, g] = st
    for _ in pending:
        pass


def _rwkv_and_output(feat, gate, w0, w_w2, a0, w_a2, k_k, k_a, r_k, lnx_w, lnx_b,
                     y_da, x, w_out_bf16, g_post, mod_gate):
    batch, seq, _ = feat.shape
    rows = RW_CHUNK * RW_STEP_CHUNKS
    full = lambda shape: pl.BlockSpec(shape, lambda i: (0,) * len(shape))
    vec = lambda t: t.reshape(1, -1)
    return pl.pallas_call(
        _rwkv_kernel,
        grid=(seq // rows,),
        in_specs=[
            pl.BlockSpec((batch, rows, RW_SHIFTED), lambda i: (0, i, 0)),
            pl.BlockSpec((batch, rows, RW_WIDTH), lambda i: (0, i, 0)),
            full((1, RW_WIDTH)), full((RW_RANK, RW_WIDTH)),
            full((1, RW_WIDTH)), full((RW_RANK, RW_WIDTH)),
            full((1, RW_WIDTH)), full((1, RW_WIDTH)), full((1, RW_WIDTH)),
            full((1, RW_WIDTH)), full((1, RW_WIDTH)),
            pl.BlockSpec((batch, rows, DA_WIDTH), lambda i: (0, i, 0)),
            pl.BlockSpec((batch, rows, D_MODEL), lambda i: (0, i, 0)),
            pl.BlockSpec((D_MODEL, D_MODEL), lambda i: (0, 0), pipeline_mode=pl.Buffered(1)),
            full((1, D_MODEL)), full((batch, 1, D_MODEL)),
        ],
        out_specs=pl.BlockSpec((batch, rows, D_MODEL), lambda i: (0, i, 0)),
        out_shape=jax.ShapeDtypeStruct(x.shape, x.dtype),
        scratch_shapes=[
            pltpu.VMEM((batch, RW_WIDTH // RW_GROUP, RW_HEAD, RW_GROUP), F32),
        ],
        compiler_params=pltpu.CompilerParams(
            dimension_semantics=("arbitrary",),
            vmem_limit_bytes=VMEM_LIMIT),
        name="rwkv7_chunked",
    )(feat, gate, vec(w0), w_w2.astype(BF16), vec(a0), w_a2.astype(BF16),
      vec(k_k), vec(k_a), vec(r_k), vec(lnx_w), vec(lnx_b),
      y_da, x, w_out_bf16, g_post, mod_gate)


def _attn_kernel(lq1_ref, lk1_ref, lq2_ref, lk2_ref, sw_ref, qt_ref, k_ref, vt_ref,
                 g_ref, o_ref, s_buf, p_buf, acc_buf, *, lambda_init):
    T, HALF = ATT_TILE, ATT_TILE // 2
    tiles = k_ref.shape[1] // T

    comp = lax.broadcasted_iota(jnp.int32, (DA_VDIM, T), 0) < DA_QKDIM
    key = lax.broadcasted_iota(jnp.int32, (HALF, HALF), 0)
    query = lax.broadcasted_iota(jnp.int32, (HALF, HALF), 1)
    visible = (query // CHUNK) >= (key // CHUNK)
    lam = (jnp.exp(jnp.sum(lq1_ref[...] * lk1_ref[...], axis=-1, keepdims=True))
           - jnp.exp(jnp.sum(lq2_ref[...] * lk2_ref[...], axis=-1, keepdims=True))
           + lambda_init)

    for step in range(ATT_STEPS):
        order = [i for k in range(step, tiles // 2, ATT_STEPS) for i in (tiles - 1 - k, k)]
        pl.when(pl.program_id(2) == step)(functools.partial(
            _attn_tasks, [(i, j) for i in order for j in range(i + 1)],
            comp, visible, lam, sw_ref, qt_ref, k_ref, vt_ref, g_ref, o_ref,
            s_buf, p_buf, acc_buf, lambda_init))


def _attn_tasks(tasks, comp, visible, lam, sw_ref, qt_ref, k_ref, vt_ref, g_ref, o_ref,
                s_buf, p_buf, acc_buf, lambda_init):
    T, HALF = ATT_TILE, ATT_TILE // 2
    qt_parts = {}

    def qt_c(i):
        if i not in qt_parts:
            qt = qt_ref[0, 0, i]
            zero = jnp.zeros_like(qt)
            qt_parts[i] = (jnp.where(comp, qt, zero), jnp.where(comp, zero, qt))
        return qt_parts[i]

    def colmax(s):
        return jnp.max(s, axis=0, keepdims=True)

    def scores(n):
        i, j = tasks[n]
        kb = k_ref[0, j * T:(j + 1) * T, :]
        blk_max = []
        for c in range(2):
            q = qt_c(i)[c]
            if j < i:
                s = _dot(kb, q)
                s_buf[n % 3, c] = s
                blk_max.append(colmax(s))
            else:
                s_r = _dot(kb, q[:, HALF:])
                s_b = jnp.where(visible, s_r[HALF:], -jnp.inf)
                s_l = jnp.where(visible, _dot(kb[:HALF], q[:, :HALF]), -jnp.inf)
                s_buf[n % 3, c, :HALF, HALF:] = s_r[:HALF]
                s_buf[n % 3, c, HALF:, HALF:] = s_b
                s_buf[n % 3, c, :HALF, :HALF] = s_l
                blk_max.append(jnp.concatenate(
                    [colmax(s_l), jnp.maximum(colmax(s_r[:HALF]), colmax(s_b))], axis=1))
        return blk_max

    def softmax(n, blk_maxes, maxes):
        i, j = tasks[n]
        new, alphas = [], []
        for c in range(2):
            if j == 0:
                m_new, alpha = blk_maxes[c], None
            else:
                m_new = jnp.maximum(maxes[c], blk_maxes[c])
                alpha = jnp.exp2(maxes[c] - m_new)
            prob = lambda s, m: jnp.exp2((s - m).astype(BF16))
            if j < i:
                p_buf[n % 2, c] = prob(s_buf[n % 3, c], m_new)
            else:
                left, right = m_new[:, :HALF], m_new[:, HALF:]
                p_buf[n % 2, c, :HALF, :HALF] = prob(s_buf[n % 3, c, :HALF, :HALF], left)
                p_buf[n % 2, c, :HALF, HALF:] = prob(s_buf[n % 3, c, :HALF, HALF:], right)
                p_buf[n % 2, c, HALF:, HALF:] = prob(s_buf[n % 3, c, HALF:, HALF:], right)
            new.append(m_new)
            alphas.append(alpha)
        return new, alphas

    ones_rows = (lax.broadcasted_iota(jnp.int32, (ONES_ROWS, T), 0) == 0).astype(BF16)

    def values(n, alphas):
        i, j = tasks[n]
        vtb = jnp.concatenate([vt_ref[0, 0, j], ones_rows], axis=0)
        for c in range(2):
            if j < i:
                pv = _dot(vtb, p_buf[n % 2, c])
            else:
                pv = jnp.concatenate(
                    [_dot(vtb[:, :HALF], p_buf[n % 2, c, :HALF, :HALF]),
                     _dot(vtb, p_buf[n % 2, c, :, HALF:])], axis=1)
            acc_buf[i % 2, c] = pv if j == 0 else alphas[c] * acc_buf[i % 2, c] + pv

    def finish(i):
        num0, num1 = acc_buf[i % 2, 0, :DA_VDIM], acc_buf[i % 2, 1, :DA_VDIM]
        l0 = acc_buf[i % 2, 0, DA_VDIM:DA_VDIM + 1]
        l1 = acc_buf[i % 2, 1, DA_VDIM:DA_VDIM + 1]
        o = (num0 * (1.0 / l0) - num1 * (lam / l1)).T
        o = o * lax.rsqrt(jnp.mean(o * o, axis=-1, keepdims=True) + SUBLN_EPS)
        o = o * sw_ref[...] * (1.0 - lambda_init)
        rows = slice(i * T, (i + 1) * T)
        o_ref[0, rows, :] = (o * _silu(g_ref[0, rows, :].astype(F32))).astype(o_ref.dtype)

    blk_max = {0: scores(0)}
    maxes, pending = None, None
    for n, (i, j) in enumerate(tasks):
        if n + 1 < len(tasks) and n + 1 not in blk_max:
            blk_max[n + 1] = scores(n + 1)
        if pending is not None:
            values(pending[0], pending[1])
            if pending[2]:
                finish(tasks[pending[0]][0])
        if n + 2 < len(tasks):
            blk_max[n + 2] = scores(n + 2)
        maxes, alphas = softmax(n, blk_max.pop(n), maxes)
        pending = (n, alphas, j == i)
    values(pending[0], pending[1])
    finish(tasks[-1][0])


def _diff_attention(qt, k, vt, gate, lam_q1, lam_k1, lam_q2, lam_k2, subln_w, lambda_init):
    batch, seq, _ = k.shape
    T = ATT_TILE
    small = lambda n: pl.BlockSpec((1, n), lambda b, h, s: (0, 0))
    rows = pl.BlockSpec((1, seq, DA_VDIM), lambda b, h, s: (b, 0, h))
    cols = pl.BlockSpec((1, 1, seq // T, DA_VDIM, T), lambda b, h, s: (b, h, 0, 0, 0))
    return pl.pallas_call(
        functools.partial(_attn_kernel, lambda_init=lambda_init),
        grid=(batch, DA_HEADS, ATT_STEPS),
        in_specs=[small(DA_QKDIM)] * 4 + [small(DA_VDIM), cols, rows, cols, rows],
        out_specs=rows,
        out_shape=jax.ShapeDtypeStruct((batch, seq, DA_WIDTH), BF16),
        scratch_shapes=[pltpu.VMEM((3, 2, T, T), F32),
                        pltpu.VMEM((2, 2, T, T), BF16),
                        pltpu.VMEM((2, 2, DA_VDIM + ONES_ROWS, T), F32)],
        compiler_params=pltpu.CompilerParams(
            dimension_semantics=("parallel", "parallel", "arbitrary"),
            vmem_limit_bytes=VMEM_LIMIT),
        name="diff_attn",
    )(*(t.reshape(1, -1) for t in (lam_q1, lam_k1, lam_q2, lam_k2, subln_w)),
      qt, k, vt, gate)


def kernel(x, c, w_ada, b_ada, g_pre, g_post, w_in, w_out, rw_mu, rw_w0, rw_w_w2,
           rw_a0, rw_w_a2, rw_k_k, rw_k_a, rw_r_k, rw_lnx_w, rw_lnx_b,
           da_lam_q1, da_lam_k1, da_lam_q2, da_lam_k2, da_subln_w):
    batch = x.shape[0]
    depth = w_in.shape[0]
    for l in range(depth):
        lambda_init = 0.8 - 0.6 * math.exp(-0.3 * l)
        mod = _modulation(c, w_ada[l], b_ada[l]).reshape(batch, MOD_PARTS, 1, D_MODEL)
        shift, scale, gate = mod[:, 0], mod[:, 1], mod[:, 2]
        feat, rw_gate, dk, dg, dqt, dvt = _in_projection(
            x, shift, scale, g_pre[l].reshape(1, D_MODEL), rw_mu[l].reshape(1, RW_SHIFTED),
            w_in[l].astype(BF16))
        y_da = _diff_attention(dqt, dk, dvt, dg, da_lam_q1[l], da_lam_k1[l],
                               da_lam_q2[l], da_lam_k2[l], da_subln_w[l], lambda_init)
        x = _rwkv_and_output(feat, rw_gate, rw_w0[l], rw_w_w2[l], rw_a0[l], rw_w_a2[l],
                             rw_k_k[l], rw_k_a[l], rw_r_k[l], rw_lnx_w[l], rw_lnx_b[l],
                             y_da, x, w_out[l].astype(BF16), g_post[l].reshape(1, D_MODEL), gate)
    return x
```
